```python
import math
import jax, jax.numpy as jnp
from jax import lax
import numpy as np

D_MODEL = 1024
BATCH = 4
SEQ = 4096
DEPTH = 1
DEC_BATCH = 128
DEC_SEQ = 1
PAST_LEN = 8192
PAGE_SIZE = 128

D_RNN = D_MODEL // 2
LRU_BLOCKS = 8
LRU_BW = D_RNN // LRU_BLOCKS
CONV_W = 4
LRU_C = 8.0
N_HEADS = 8
HEAD_DIM = (D_MODEL - D_RNN) // N_HEADS
N_KV = 2
GROUP = N_HEADS // N_KV
D_ATT = N_HEADS * HEAD_DIM
D_MIX = D_RNN + D_ATT
CMP_LEN = 32
CMP_STRIDE = 16
CMP_RATIO = CMP_LEN // CMP_STRIDE
CMP_HID = HEAD_DIM
SEL_BLOCK = 64
SEL_TOP = 16
WINDOW = 512
Q_BLOCK = 128
SCALE = HEAD_DIM ** -0.5
NEG_INF = -1e30
FORCE_SCORE = 1e6
NUM_BUCKETS = 32
MAX_DISTANCE = 1024
PEER_HEADS = 8
PEER_KEYS = 128
PEER_EXPERTS = PEER_KEYS * PEER_KEYS
PEER_DKEY = 256
PEER_TOPK = 16
PEER_CHUNK = 128
NORM_EPS = 1e-6
COL_SIZES = (D_RNN, D_RNN, D_ATT, 2 * N_KV * HEAD_DIM, 2 * N_KV * HEAD_DIM, 2 * N_KV * HEAD_DIM, 3 * N_HEADS)
IN_COLS = sum(COL_SIZES)

kernel_name = 'hymba_rglru_nsa_peer_step'


def rmsnorm(x, g):
    xf = x.astype(jnp.float32)
    y = xf * lax.rsqrt(jnp.mean(xf * xf, axis=-1, keepdims=True) + NORM_EPS)
    return (y * g.astype(jnp.float32)).astype(x.dtype)


def ada_mod(c, w, b):
    m = (jax.nn.silu(c) @ w + b)[:, None, :]
    return jnp.split(m, 6, axis=-1)


def split_columns(z):
    offs = np.cumsum((0,) + COL_SIZES)
    return [z[..., int(offs[i]):int(offs[i + 1])] for i in range(len(COL_SIZES))]


def t5_bucket(dist):
    d = jnp.maximum(dist, 0)
    max_exact = NUM_BUCKETS // 2
    df = jnp.maximum(d, 1).astype(jnp.float32)
    large = max_exact + (jnp.log(df / max_exact) / math.log(MAX_DISTANCE / max_exact)
                         * (NUM_BUCKETS - max_exact)).astype(jnp.int32)
    large = jnp.minimum(large, NUM_BUCKETS - 1)
    return jnp.where(d < max_exact, d, large)


def causal_conv(x, buf, w, b):
    T = x.shape[1]
    xp = jnp.concatenate([buf.astype(x.dtype), x], axis=1)
    y = sum(xp[:, k:k + T] * w[k] for k in range(CONV_W)) + b
    return y, xp[:, -(CONV_W - 1):]


def _linear_combine(left, right):
    a1, b1 = left
    a2, b2 = right
    return a1 * a2, a2 * b1 + b2


def rg_lru(xc, h0, wa, ba, wi, bi, lam):
    B, T, _ = xc.shape
    xb = xc.reshape(B, T, LRU_BLOCKS, LRU_BW)
    r = jax.nn.sigmoid((jnp.einsum('btnd,nde->btne', xb, wa).reshape(B, T, D_RNN) + ba).astype(jnp.float32))
    i = jax.nn.sigmoid((jnp.einsum('btnd,nde->btne', xb, wi).reshape(B, T, D_RNN) + bi).astype(jnp.float32))
    log_a = -LRU_C * r * jax.nn.softplus(-lam.astype(jnp.float32))
    a = jnp.exp(log_a)
    u = jnp.sqrt(-jnp.expm1(2.0 * log_a)) * i * xc.astype(jnp.float32)
    u = u.at[:, 0].add(a[:, 0] * h0.astype(jnp.float32))
    _, h = lax.associative_scan(_linear_combine, (a, u), axis=1)
    return h.astype(xc.dtype), h[:, -1].astype(xc.dtype)


def chunk_proj(rows, w1):
    B, T = rows.shape[:2]
    n = T // CMP_STRIDE
    ch = rows[:, :n * CMP_STRIDE].reshape(B, n, CMP_STRIDE, N_KV, 2, HEAD_DIM)
    w1r = w1.reshape(2, CMP_RATIO, CMP_STRIDE, HEAD_DIM, CMP_HID)
    return jnp.einsum('bnsgcd,cmsdh->bnmgch', ch, w1r)


def compress(proj, w1, w2, pos):
    nb = proj.shape[1] - CMP_RATIO + 1
    hid = jnp.einsum('csd,csdh->ch', pos, w1)
    for m in range(CMP_RATIO):
        hid = hid + proj[:, m:m + nb, m]
    out = jnp.einsum('bngch,che->bngce', jax.nn.gelu(hid), w2)
    ends = jnp.arange(nb, dtype=jnp.int32) * CMP_STRIDE + (CMP_LEN - 1)
    return out, ends


def dense_attend(q, q_pos, k, v, k_pos, rel_bias, window):
    Tq, Nk = q.shape[1], k.shape[1]
    dist = q_pos[:, None] - k_pos[None, :]
    valid = (dist >= 0) & (k_pos[None, :] >= 0)
    if window is not None:
        valid = valid & (dist < window)
    bias = rel_bias[t5_bucket(dist)].reshape(Tq, Nk, N_KV, GROUP).transpose(2, 3, 0, 1)
    s = jnp.einsum('bqgrd,bkgd->bgrqk', q, k).astype(jnp.float32) * SCALE + bias.astype(jnp.float32)
    s = jnp.where(valid, s, NEG_INF)
    p = jax.nn.softmax(s, axis=-1) * jnp.any(valid, axis=-1, keepdims=True)
    o = jnp.einsum('bgrqk,bkgd->bqgrd', p.astype(v.dtype), v)
    return o, p


def select_blocks(p_cmp, q_pos, n_sel):
    n_cmp = p_cmp.shape[-1]
    start = jnp.arange(n_cmp) * CMP_STRIDE
    blk = jnp.arange(n_sel)
    covers = ((start[:, None] < (blk[None, :] + 1) * SEL_BLOCK)
              & (start[:, None] + CMP_LEN > blk[None, :] * SEL_BLOCK)).astype(p_cmp.dtype)
    score = jnp.einsum('bgrqn,nj->bqgj', p_cmp, covers)
    cur = (q_pos // SEL_BLOCK)[:, None]
    forced = (blk[None, :] == 0) | (blk[None, :] == cur) | (blk[None, :] == cur - 1)
    eligible = blk[None, :] * SEL_BLOCK <= q_pos[:, None]
    score = jnp.where(forced[None, :, None, :], FORCE_SCORE,
                      jnp.where(eligible[None, :, None, :], score, -1.0))
    _, idx = lax.top_k(score, min(SEL_TOP, n_sel))
    return idx


def select_attend(q, q_pos, blk, idx, rel_bias):
    B, Tq = q.shape[:2]
    k_pos = idx[..., None] * SEL_BLOCK + jnp.arange(SEL_BLOCK, dtype=jnp.int32)
    dist = q_pos[None, :, None, None, None] - k_pos
    bias = rel_bias.reshape(NUM_BUCKETS, N_KV, GROUP)[t5_bucket(dist), jnp.arange(N_KV)[:, None, None]]
    s = jnp.einsum('bqgrd,bqgjsd->bqgjsr', q, blk[..., 0, :]).astype(jnp.float32) * SCALE + bias.astype(jnp.float32)
    s = jnp.where((dist >= 0)[..., None], s, NEG_INF)
    p = jax.nn.softmax(s.reshape(B, Tq, N_KV, -1, GROUP), axis=3).reshape(s.shape)
    return jnp.einsum('bqgjsr,bqgjsd->bqgrd', p.astype(blk.dtype), blk[..., 1, :])


def nsa_prompt(q, kv_c, kv_s, kv_w, cmp_w, rel_bias):
    B, T = q.shape[:2]
    q_pos = jnp.arange(T, dtype=jnp.int32)
    ck, ends = compress(chunk_proj(kv_c, cmp_w[0]), *cmp_w)
    o_c, p_c = dense_attend(q, q_pos, ck[..., 0, :], ck[..., 1, :], ends, rel_bias, None)
    n_sel = T // SEL_BLOCK
    idx = select_blocks(p_c, q_pos, n_sel)
    win_pad = jnp.pad(kv_w, ((0, 0), (WINDOW, 0), (0, 0), (0, 0), (0, 0)))
    b_ix = jnp.arange(B)[:, None, None, None, None]
    g_ix = jnp.arange(N_KV)[None, None, :, None, None]

    def q_block(c):
        s0 = c * Q_BLOCK
        qb = lax.dynamic_slice_in_dim(q, s0, Q_BLOCK, axis=1)
        pb = s0 + jnp.arange(Q_BLOCK, dtype=jnp.int32)
        ib = lax.dynamic_slice_in_dim(idx, s0, Q_BLOCK, axis=1)
        rows = ib[..., None] * SEL_BLOCK + jnp.arange(SEL_BLOCK)
        o_s = select_attend(qb, pb, kv_s[b_ix, rows, g_ix], ib, rel_bias)
        kw = lax.dynamic_slice_in_dim(win_pad, s0, WINDOW + Q_BLOCK, axis=1)
        kp = s0 - WINDOW + jnp.arange(WINDOW + Q_BLOCK, dtype=jnp.int32)
        o_w, _ = dense_attend(qb, pb, kw[..., 0, :], kw[..., 1, :], kp, rel_bias, WINDOW)
        return o_s, o_w

    o_s, o_w = lax.map(q_block, jnp.arange(T // Q_BLOCK, dtype=jnp.int32))
    o_s = jnp.moveaxis(o_s, 0, 1).reshape(q.shape)
    o_w = jnp.moveaxis(o_w, 0, 1).reshape(q.shape)
    return o_c, o_s, o_w, kv_w[:, -min(WINDOW, T):]


def nsa_sample(q, kv_c, kv_s, kv_w, cache_cmp, cache_sel, cache_win, page_table, layer, cmp_w, rel_bias):
    B, T = q.shape[:2]
    n_pages = page_table.shape[1]
    past = n_pages * PAGE_SIZE
    q_pos = past + jnp.arange(T, dtype=jnp.int32)
    past_c = cache_cmp[layer, page_table].reshape(B, past, N_KV, 2, HEAD_DIM)
    proj = jnp.concatenate([chunk_proj(past_c, cmp_w[0]), chunk_proj(kv_c, cmp_w[0])], axis=1)
    ck, ends = compress(proj, *cmp_w)
    o_c, p_c = dense_attend(q, q_pos, ck[..., 0, :], ck[..., 1, :], ends, rel_bias, None)
    n_sel = -(-(past + T) // SEL_BLOCK)
    idx = select_blocks(p_c, q_pos, n_sel)
    n_past_blk = past // SEL_BLOCK
    n_new_blk = n_sel - n_past_blk
    blk_per_page = PAGE_SIZE // SEL_BLOCK
    b_ix = jnp.arange(B)[:, None, None, None, None]
    g_ix = jnp.arange(N_KV)[None, None, :, None, None]
    phys = page_table[b_ix[..., 0], jnp.minimum(idx // blk_per_page, n_pages - 1)]
    rows = (idx % blk_per_page)[..., None] * SEL_BLOCK + jnp.arange(SEL_BLOCK)
    from_past = cache_sel[layer, phys[..., None], rows, g_ix]
    new_pad = jnp.pad(kv_s, ((0, 0), (0, n_new_blk * SEL_BLOCK - T), (0, 0), (0, 0), (0, 0)))
    rows_new = jnp.clip(idx - n_past_blk, 0, n_new_blk - 1)[..., None] * SEL_BLOCK + jnp.arange(SEL_BLOCK)
    from_new = new_pad[b_ix, rows_new, g_ix].astype(from_past.dtype)
    blk = jnp.where((idx < n_past_blk)[..., None, None, None], from_past, from_new)
    o_s = select_attend(q, q_pos, blk, idx, rel_bias)
    wb = cache_win.shape[1]
    kw = jnp.concatenate([cache_win, kv_w.astype(cache_win.dtype)], axis=1)
    kp = jnp.concatenate([past - wb + jnp.arange(wb, dtype=jnp.int32), q_pos])
    o_w, _ = dense_attend(q, q_pos, kw[..., 0, :], kw[..., 1, :], kp, rel_bias, WINDOW)
    return o_c, o_s, o_w, kw[:, -wb:]


def peer_ffn(x, wq, bq, subkeys, u_tab, v_tab):
    n, d = x.shape
    n_chunks = -(-n // PEER_CHUNK)
    xp = jnp.pad(x, ((0, n_chunks * PEER_CHUNK - n), (0, 0))).reshape(n_chunks, PEER_CHUNK, d)
    half = PEER_DKEY // 2

    def chunk_fn(xc):
        q = (xc @ wq + bq).reshape(PEER_CHUNK, PEER_HEADS, 2, half)
        s = jnp.einsum('nhpd,hpkd->nhpk', q, subkeys).astype(jnp.float32)
        sv, si = lax.top_k(s, PEER_TOPK)
        cand = (sv[:, :, 0, :, None] + sv[:, :, 1, None, :]).reshape(PEER_CHUNK, PEER_HEADS, PEER_TOPK * PEER_TOPK)
        best, ci = lax.top_k(cand, PEER_TOPK)
        e = (jnp.take_along_axis(si[:, :, 0], ci // PEER_TOPK, axis=-1) * PEER_KEYS
             + jnp.take_along_axis(si[:, :, 1], ci % PEER_TOPK, axis=-1))
        g = jax.nn.softmax(best, axis=-1)
        z = jnp.einsum('nd,nhkd->nhk', xc, u_tab[e])
        act = (jax.nn.gelu(z.astype(jnp.float32)) * g).astype(xc.dtype)
        return jnp.einsum('nhk,nhkd->nd', act, v_tab[e])

    return lax.map(chunk_fn, xp).reshape(-1, d)[:n]


def layer_forward(x, c, lw, rel_bias, past):
    (ada_w, ada_b, norm1_g, norm2_g, w_in, conv_w, conv_b, lru_wa, lru_ba, lru_wi, lru_bi,
     lru_lambda, cmp_w1, cmp_w2, cmp_pos, out_norm_lru, out_norm_att, w_out,
     peer_wq, peer_bq, peer_subkeys, peer_u, peer_v) = lw
    B, T, _ = x.shape
    sh1, sc1, gt1, sh2, sc2, gt2 = ada_mod(c, ada_w, ada_b)
    h = rmsnorm(x, norm1_g) * (1.0 + sc1) + sh1
    xr, yg, q, kv_c, kv_s, kv_w, gl = split_columns(h @ w_in)
    kv_shape = (B, T, N_KV, 2, HEAD_DIM)
    kv_c, kv_s, kv_w = kv_c.reshape(kv_shape), kv_s.reshape(kv_shape), kv_w.reshape(kv_shape)
    q = q.reshape(B, T, N_KV, GROUP, HEAD_DIM)
    cmp_w = (cmp_w1, cmp_w2, cmp_pos)
    if past is None:
        conv_buf = jnp.zeros((B, CONV_W - 1, D_RNN), x.dtype)
        h0 = jnp.zeros((B, D_RNN), x.dtype)
        o_c, o_s, o_w, new_win = nsa_prompt(q, kv_c, kv_s, kv_w, cmp_w, rel_bias)
    else:
        cache_cmp, cache_sel, cache_win, h0, conv_buf, page_table, layer = past
        o_c, o_s, o_w, new_win = nsa_sample(q, kv_c, kv_s, kv_w, cache_cmp, cache_sel, cache_win,
                                            page_table, layer, cmp_w, rel_bias)
    xc, new_conv = causal_conv(xr, conv_buf, conv_w, conv_b)
    hs, new_h = rg_lru(xc, h0, lru_wa, lru_ba, lru_wi, lru_bi, lru_lambda)
    o_lru = hs * jax.nn.gelu(yg)
    g = jax.nn.sigmoid(gl).reshape(B, T, 3, N_KV, GROUP, 1)
    o_att = (g[:, :, 0] * o_c + g[:, :, 1] * o_s + g[:, :, 2] * o_w).reshape(B, T, D_ATT)
    mixed = jnp.concatenate([rmsnorm(o_lru, out_norm_lru), rmsnorm(o_att, out_norm_att)], axis=-1) @ w_out
    x = x + gt1 * mixed
    h2 = rmsnorm(x, norm2_g) * (1.0 + sc2) + sh2
    ff = peer_ffn(h2.reshape(B * T, D_MODEL), peer_wq, peer_bq, peer_subkeys, peer_u, peer_v)
    x = x + gt2 * ff.reshape(B, T, D_MODEL)
    return x, (kv_c, kv_s, new_win, new_h, new_conv)


def setup_inputs(seed: int = 0) -> dict:
    key = jax.random.key(seed)
    ks = iter(jax.random.split(key, 48))

    def nrm(shape, scale):
        return jax.random.normal(next(ks), shape, jnp.float32) * scale

    def gain(shape):
        return 1.0 + nrm(shape, 0.02)

    L = DEPTH
    n_pages = PAST_LEN // PAGE_SIZE
    n_pool = (5 * DEC_BATCH * n_pages) // 4
    win_buf = min(WINDOW, PAST_LEN)
    a_init = jax.random.uniform(next(ks), (L, D_RNN), jnp.float32, 0.9, 0.999) ** (1.0 / LRU_C)
    page_table = jax.random.permutation(next(ks), n_pool)[:DEC_BATCH * n_pages].reshape(DEC_BATCH, n_pages).astype(jnp.int32)
    return {
        'x_prompt': nrm((BATCH, SEQ, D_MODEL), 1.0),
        'x_sample': nrm((DEC_BATCH, DEC_SEQ, D_MODEL), 1.0),
        'cache_cmp_kv': nrm((L, n_pool, PAGE_SIZE, N_KV, 2, HEAD_DIM), 1.0),
        'cache_sel_kv': nrm((L, n_pool, PAGE_SIZE, N_KV, 2, HEAD_DIM), 1.0),
        'cache_win_kv': nrm((L, DEC_BATCH, win_buf, N_KV, 2, HEAD_DIM), 1.0),
        'state_lru_h': nrm((L, DEC_BATCH, D_RNN), 0.5),
        'state_conv': nrm((L, DEC_BATCH, CONV_W - 1, D_RNN), 1.0),
        'page_table': page_table,
        'c_prompt': nrm((BATCH, D_MODEL), 1.0),
        'c_sample': nrm((DEC_BATCH, D_MODEL), 1.0),
        'ada_w': nrm((L, D_MODEL, 6 * D_MODEL), 0.5 * D_MODEL ** -0.5),
        'ada_b': nrm((L, 6 * D_MODEL), 0.02),
        'norm1_g': gain((L, D_MODEL)),
        'norm2_g': gain((L, D_MODEL)),
        'w_in': nrm((L, D_MODEL, IN_COLS), D_MODEL ** -0.5),
        'conv_w': nrm((L, CONV_W, D_RNN), CONV_W ** -0.5),
        'conv_b': nrm((L, D_RNN), 0.02),
        'lru_wa': nrm((L, LRU_BLOCKS, LRU_BW, LRU_BW), LRU_BW ** -0.5),
        'lru_ba': nrm((L, D_RNN), 0.02),
        'lru_wi': nrm((L, LRU_BLOCKS, LRU_BW, LRU_BW), LRU_BW ** -0.5),
        'lru_bi': nrm((L, D_RNN), 0.02),
        'lru_lambda': jnp.log(a_init) - jnp.log1p(-a_init),
        'cmp_w1': nrm((L, 2, CMP_LEN, HEAD_DIM, CMP_HID), (CMP_LEN * HEAD_DIM) ** -0.5),
        'cmp_w2': nrm((L, 2, CMP_HID, HEAD_DIM), CMP_HID ** -0.5),
        'cmp_pos': nrm((L, 2, CMP_LEN, HEAD_DIM), 0.1),
        'out_norm_lru': gain((L, D_RNN)),
        'out_norm_att': gain((L, D_ATT)),
        'w_out': nrm((L, D_MIX, D_MODEL), D_MIX ** -0.5),
        'peer_wq': nrm((L, D_MODEL, PEER_HEADS * PEER_DKEY), D_MODEL ** -0.5),
        'peer_bq': nrm((L, PEER_HEADS * PEER_DKEY), 0.02),
        'peer_subkeys': nrm((L, PEER_HEADS, 2, PEER_KEYS, PEER_DKEY // 2), (PEER_DKEY // 2) ** -0.5),
        'peer_u': nrm((L, PEER_EXPERTS, D_MODEL), D_MODEL ** -0.5),
        'peer_v': nrm((L, PEER_EXPERTS, D_MODEL), PEER_HEADS ** -0.5),
        'rel_bias': nrm((NUM_BUCKETS, N_HEADS), 0.3),
        'final_g': gain((D_MODEL,)),
    }


def reference(x_prompt, x_sample, cache_cmp_kv, cache_sel_kv, cache_win_kv, state_lru_h, state_conv,
              page_table, c_prompt, c_sample, ada_w, ada_b, norm1_g, norm2_g, w_in, conv_w, conv_b,
              lru_wa, lru_ba, lru_wi, lru_bi, lru_lambda, cmp_w1, cmp_w2, cmp_pos, out_norm_lru,
              out_norm_att, w_out, peer_wq, peer_bq, peer_subkeys, peer_u, peer_v, rel_bias, final_g):
    y_prompt, y_sample = x_prompt, x_sample
    new_p, new_s = [], []
    for l in range(DEPTH):
        lw = (ada_w[l], ada_b[l], norm1_g[l], norm2_g[l], w_in[l], conv_w[l], conv_b[l], lru_wa[l],
              lru_ba[l], lru_wi[l], lru_bi[l], lru_lambda[l], cmp_w1[l], cmp_w2[l], cmp_pos[l],
              out_norm_lru[l], out_norm_att[l], w_out[l], peer_wq[l], peer_bq[l], peer_subkeys[l],
              peer_u[l], peer_v[l])
        y_prompt, sp = layer_forward(y_prompt, c_prompt, lw, rel_bias, None)
        past = (cache_cmp_kv, cache_sel_kv, cache_win_kv[l], state_lru_h[l], state_conv[l], page_table, l)
        y_sample, ss = layer_forward(y_sample, c_sample, lw, rel_bias, past)
        new_p.append(sp)
        new_s.append(ss)
    y_prompt = rmsnorm(y_prompt, final_g)
    y_sample = rmsnorm(y_sample, final_g)

    def stk(states, i):
        return jnp.stack([s[i] for s in states])

    return (y_prompt, y_sample,
            stk(new_p, 0), stk(new_s, 0),
            stk(new_p, 1), stk(new_s, 1),
            stk(new_p, 2), stk(new_s, 2),
            stk(new_p, 3), stk(new_s, 3),
            stk(new_p, 4), stk(new_s, 4))
```

```python
import functools
import math

import numpy as np
import jax
import jax.numpy as jnp
from jax import lax
from jax.experimental import pallas as pl
from jax.experimental.pallas import tpu as pltpu

F32 = jnp.float32
BF16 = jnp.bfloat16
HIGHEST = lax.Precision.HIGHEST

D_MODEL = 1024
D_RNN = 512
LRU_BLOCKS = 8
CONV_W = 4
LRU_C = 8.0
N_HEADS = 8
HEAD_DIM = 64
N_KV = 2
GROUP = 4
D_ATT = 512
KV_COLS = 2 * N_KV * HEAD_DIM
CMP_LEN = 32
CMP_STRIDE = 16
SEL_BLOCK = 64
SEL_TOP = 16
WINDOW = 512
SCALE = HEAD_DIM ** -0.5
NEG_INF = -1e30
REMOVED = -3e38
FORCE_SCORE = 1e6
NUM_BUCKETS = 32
MAX_DISTANCE = 1024
PEER_HEADS = 8
PEER_KEYS = 128
PEER_TOPK = 16
NORM_EPS = 1e-6
PAGE_SIZE = 128

LANE = 128
VMEM_LIMIT = 56 * 1024 * 1024


def _bucket_thresholds():
    d = np.arange(0, 1 << 15, dtype=np.int64)
    max_exact = NUM_BUCKETS // 2
    df = np.maximum(d, 1).astype(np.float32)
    ratio = (np.log(df / np.float32(max_exact)) / np.float32(math.log(MAX_DISTANCE / max_exact))
             * np.float32(NUM_BUCKETS - max_exact))
    large = np.minimum(max_exact + ratio.astype(np.int32), NUM_BUCKETS - 1)
    bucket = np.where(d < max_exact, d, large)
    assert np.all(np.diff(bucket) >= 0)
    return tuple(int(np.argmax(bucket >= k)) for k in range(1, NUM_BUCKETS))


BUCKET_THR = _bucket_thresholds()


def _params(*sem):
    return pltpu.CompilerParams(dimension_semantics=sem or None, vmem_limit_bytes=VMEM_LIMIT)


def _nt(a, b):
    return lax.dot_general(a, b, (((1,), (1,)), ((), ())), preferred_element_type=F32)


def _mm(a, b):
    return jnp.dot(a, b, preferred_element_type=F32)


def _rms(x, g):
    return x * lax.rsqrt(jnp.mean(x * x, axis=-1, keepdims=True) + NORM_EPS) * g


def _bias_lookup(dist, rb_ref, h):
    b = jnp.full(dist.shape, rb_ref[0, h], F32)
    for k in range(1, NUM_BUCKETS):
        b = jnp.where(dist >= BUCKET_THR[k - 1], rb_ref[k, h], b)
    return b


def _ada_kernel(c_ref, w_ref, b_ref, o_ref):
    c = c_ref[...]
    o_ref[...] = jnp.dot(jax.nn.silu(c), w_ref[...], precision=HIGHEST,
                         preferred_element_type=F32) + b_ref[...]


def _ada(c_all, w, b):
    rows, d = c_all.shape
    cols = w.shape[1]
    tn = 512
    return pl.pallas_call(
        _ada_kernel,
        grid=(cols // tn,),
        in_specs=[pl.BlockSpec((rows, d), lambda j: (0, 0)),
                  pl.BlockSpec((d, tn), lambda j: (0, j)),
                  pl.BlockSpec((1, tn), lambda j: (0, j))],
        out_specs=pl.BlockSpec((rows, tn), lambda j: (0, j)),
        out_shape=jax.ShapeDtypeStruct((rows, cols), F32),
        compiler_params=_params("arbitrary"),
    )(c_all, w, b.reshape(1, cols))


def _bias_prompt_kernel(rb_ref, bt_ref, bc_ref, *, nd, t, nc):
    h = pl.program_id(0)
    row = lax.broadcasted_iota(jnp.int32, (LANE, LANE), 0)
    col = lax.broadcasted_iota(jnp.int32, (LANE, LANE), 1)
    for dl in range(nd):
        bt_ref[dl] = _bias_lookup(dl * LANE + row - col, rb_ref, h)
    rq = lax.broadcasted_iota(jnp.int32, (LANE, nc), 0)
    ends = lax.broadcasted_iota(jnp.int32, (LANE, nc), 1) * CMP_STRIDE + (CMP_LEN - 1)

    def chunk(i, carry):
        r0 = pl.multiple_of(i * LANE, LANE)
        bc_ref[pl.ds(r0, LANE), :] = _bias_lookup(r0 + rq - ends, rb_ref, h)
        return carry

    lax.fori_loop(0, t // LANE, chunk, 0)


def _bias_prompt(rel_bias, t, nd):
    nc = t // CMP_STRIDE
    return pl.pallas_call(
        functools.partial(_bias_prompt_kernel, nd=nd, t=t, nc=nc),
        grid=(N_HEADS,),
        in_specs=[pl.BlockSpec(memory_space=pltpu.SMEM)],
        out_specs=[pl.BlockSpec((None, nd, LANE, LANE), lambda h: (h, 0, 0, 0)),
                   pl.BlockSpec((None, t, nc), lambda h: (h, 0, 0))],
        out_shape=[jax.ShapeDtypeStruct((N_HEADS, nd, LANE, LANE), F32),
                   jax.ShapeDtypeStruct((N_HEADS, t, nc), F32)],
        compiler_params=_params("arbitrary"),
    )(rel_bias)


def _bias_sample_kernel(rb_ref, bsc_ref, bsw_ref, bss_ref, *, past, ncs, wb, npb):
    ends = lax.broadcasted_iota(jnp.int32, (1, ncs), 1) * CMP_STRIDE + (CMP_LEN - 1)
    jw = lax.broadcasted_iota(jnp.int32, (1, wb), 1)
    kpos = (lax.broadcasted_iota(jnp.int32, (npb, SEL_BLOCK), 0) * SEL_BLOCK
            + lax.broadcasted_iota(jnp.int32, (npb, SEL_BLOCK), 1))
    for h in range(N_HEADS):
        bsc_ref[h:h + 1, :] = _bias_lookup(past - ends, rb_ref, h)
        bsw_ref[h:h + 1, :] = _bias_lookup(wb - jw, rb_ref, h)
        bss_ref[h] = _bias_lookup(past - kpos, rb_ref, h)


def _bias_sample(rel_bias, past, wb):
    ncs = past // CMP_STRIDE
    npb = past // SEL_BLOCK
    return pl.pallas_call(
        functools.partial(_bias_sample_kernel, past=past, ncs=ncs, wb=wb, npb=npb),
        in_specs=[pl.BlockSpec(memory_space=pltpu.SMEM)],
        out_shape=[jax.ShapeDtypeStruct((N_HEADS, ncs), F32),
                   jax.ShapeDtypeStruct((N_HEADS, wb), F32),
                   jax.ShapeDtypeStruct((N_HEADS, npb, SEL_BLOCK), F32)],
        compiler_params=_params(),
    )(rel_bias)


IN_SPLITS = (D_RNN, D_RNN, D_ATT, KV_COLS, KV_COLS, KV_COLS, LANE)


def _inproj_kernel(x_ref, sh_ref, sc_ref, g_ref, w_ref, *out_refs):
    h = _rms(x_ref[...], g_ref[...]) * (1.0 + sc_ref[...]) + sh_ref[...]
    z = _mm(h.astype(BF16), w_ref[...])
    off = 0
    for o_ref, wdt in zip(out_refs, IN_SPLITS):
        o_ref[...] = z[:, off:off + wdt]
        off += wdt


def _inproj(x, mod, mod_spec, g1, w_in_p, tm):
    n = x.shape[0]
    cols = w_in_p.shape[1]
    return pl.pallas_call(
        _inproj_kernel,
        grid=(n // tm,),
        in_specs=[pl.BlockSpec((tm, D_MODEL), lambda i: (i, 0)),
                  mod_spec(0), mod_spec(1),
                  pl.BlockSpec((1, D_MODEL), lambda i: (0, 0)),
                  pl.BlockSpec((D_MODEL, cols), lambda i: (0, 0))],
        out_specs=[pl.BlockSpec((tm, wdt), lambda i: (i, 0)) for wdt in IN_SPLITS],
        out_shape=[jax.ShapeDtypeStruct((n, wdt), F32) for wdt in IN_SPLITS],
        compiler_params=_params("arbitrary"),
    )(x, mod, mod, g1, w_in_p)


def _lru_gates(xc, wg_ref, bg_ref, lam_ref):
    gates = _mm(xc.astype(BF16), wg_ref[...]) + bg_ref[...]
    r = jax.nn.sigmoid(gates[:, :D_RNN])
    i = jax.nn.sigmoid(gates[:, D_RNN:])
    log_a = -LRU_C * r * jax.nn.softplus(-lam_ref[...])
    a = jnp.exp(log_a)
    u = jnp.sqrt(-jnp.tanh(log_a) * (a * a + 1.0)) * i * xc
    return a, u


def _lru_prompt_kernel(xr_ref, yg_ref, cw_ref, cb_ref, wg_ref, bg_ref, lam_ref,
                       o_ref, hl_ref, cv_ref, xp_s, a_s, u_s, hs_s, h_s, *, nb, tc):
    t = pl.program_id(0)

    @pl.when(t == 0)
    def _():
        xp_s[:, 0:8, :] = jnp.zeros((nb, 8, D_RNN), F32)
        h_s[...] = jnp.zeros((nb, 1, D_RNN), F32)

    @pl.when(t > 0)
    def _():
        xp_s[:, 0:8, :] = xp_s[:, tc:tc + 8, :]

    xp_s[:, 8:, :] = xr_ref[...]
    xc = xp_s[:, 5:5 + tc, :] * cw_ref[0:1, :]
    for k in range(1, CONV_W):
        xc = xc + xp_s[:, 5 + k:5 + k + tc, :] * cw_ref[k:k + 1, :]
    xc = xc + cb_ref[...]
    a, u = _lru_gates(xc.reshape(nb * tc, D_RNN), wg_ref, bg_ref, lam_ref)
    a_s[...] = a.reshape(nb, tc, D_RNN)
    u_s[...] = u.reshape(nb, tc, D_RNN)

    def step(tt, h):
        h = a_s[:, pl.ds(tt, 1), :] * h + u_s[:, pl.ds(tt, 1), :]
        hs_s[:, pl.ds(tt, 1), :] = h
        return h

    h = lax.fori_loop(0, tc, step, h_s[...], unroll=8)
    h_s[...] = h
    o_ref[...] = hs_s[...] * jax.nn.gelu(yg_ref[...])
    hl_ref[...] = h
    cv_ref[...] = xp_s[:, tc + 8 - (CONV_W - 1):tc + 8, :]


def _lru_prompt(xr, yg, cw, cb, wg, bg, lam, tc):
    nb, t, _ = xr.shape
    blk = pl.BlockSpec((nb, tc, D_RNN), lambda i: (0, i, 0))
    full = lambda shape: pl.BlockSpec(shape, lambda i: tuple(0 for _ in shape))
    return pl.pallas_call(
        functools.partial(_lru_prompt_kernel, nb=nb, tc=tc),
        grid=(t // tc,),
        in_specs=[blk, blk, full(cw.shape), full(cb.shape), full(wg.shape), full(bg.shape),
                  full(lam.shape)],
        out_specs=[blk, full((nb, 1, D_RNN)), full((nb, CONV_W - 1, D_RNN))],
        out_shape=[jax.ShapeDtypeStruct((nb, t, D_RNN), F32),
                   jax.ShapeDtypeStruct((nb, 1, D_RNN), F32),
                   jax.ShapeDtypeStruct((nb, CONV_W - 1, D_RNN), F32)],
        scratch_shapes=[pltpu.VMEM((nb, tc + 8, D_RNN), F32), pltpu.VMEM((nb, tc, D_RNN), F32),
                        pltpu.VMEM((nb, tc, D_RNN), F32), pltpu.VMEM((nb, tc, D_RNN), F32),
                        pltpu.VMEM((nb, 1, D_RNN), F32)],
        compiler_params=_params("arbitrary"),
    )(xr, yg, cw, cb, wg, bg, lam)


def _lru_sample_kernel(x_ref, yg_ref, b0_ref, b1_ref, b2_ref, h0_ref, cw_ref, cb_ref, wg_ref,
                       bg_ref, lam_ref, o_ref, hn_ref):
    xc = b0_ref[...] * cw_ref[0:1, :]
    xc = xc + b1_ref[...] * cw_ref[1:2, :]
    xc = xc + b2_ref[...] * cw_ref[2:3, :]
    xc = xc + x_ref[...] * cw_ref[3:4, :]
    xc = xc + cb_ref[...]
    a, u = _lru_gates(xc, wg_ref, bg_ref, lam_ref)
    h = a * h0_ref[...] + u
    hn_ref[...] = h
    o_ref[...] = h * jax.nn.gelu(yg_ref[...])


def _lru_sample(x, yg, buf, h0, cw, cb, wg, bg, lam):
    s = x.shape[0]
    return pl.pallas_call(
        _lru_sample_kernel,
        out_shape=[jax.ShapeDtypeStruct((s, D_RNN), F32), jax.ShapeDtypeStruct((s, D_RNN), F32)],
        compiler_params=_params(),
    )(x, yg, buf[:, 0], buf[:, 1], buf[:, 2], h0, cw, cb, wg, bg, lam)


CHUNK_COLS = CMP_STRIDE * KV_COLS


def _poshid_kernel(pos_ref, w_ref, o_ref):
    parts = [jnp.dot(pos_ref[c], w_ref[c], precision=HIGHEST, preferred_element_type=F32)[0:1]
             for c in range(2)]
    o_ref[...] = jnp.concatenate(parts, axis=1)


def _poshid(cmp_pos, cmp_w1):
    k = CMP_LEN * HEAD_DIM
    pos = jnp.broadcast_to(cmp_pos.reshape(2, 1, k), (2, 8, k))
    return pl.pallas_call(
        _poshid_kernel,
        out_shape=jax.ShapeDtypeStruct((1, 2 * HEAD_DIM), F32),
        compiler_params=_params(),
    )(pos, cmp_w1.reshape(2, k, HEAD_DIM))


def _compress_rows(x, w1_ref, w2_ref, ph_ref):
    n = x.shape[0]
    outs = []
    for g in range(N_KV):
        xg = jnp.concatenate(
            [x[:, s * KV_COLS + g * LANE:s * KV_COLS + (g + 1) * LANE] for s in range(CMP_STRIDE)],
            axis=1)
        p = _mm(xg.astype(BF16), w1_ref[...])
        hid = ph_ref[...] + p[:, :LANE] + pltpu.roll(p[:, LANE:], n - 1, axis=0)
        outs.append(_mm(jax.nn.gelu(hid).astype(BF16), w2_ref[...]))
    return jnp.concatenate(outs, axis=1)


def _compress_prompt_kernel(x_ref, w1_ref, w2_ref, ph_ref, o_ref):
    o_ref[...] = _compress_rows(x_ref[...], w1_ref, w2_ref, ph_ref)


def _compress_prompt(kvc, w1g, w2bd, ph):
    nb, t, _ = kvc.shape
    n = t // CMP_STRIDE
    x = kvc.reshape(nb, n, CHUNK_COLS)
    return pl.pallas_call(
        _compress_prompt_kernel,
        grid=(nb,),
        in_specs=[pl.BlockSpec((None, n, CHUNK_COLS), lambda b: (b, 0, 0)),
                  pl.BlockSpec(w1g.shape, lambda b: (0, 0)),
                  pl.BlockSpec(w2bd.shape, lambda b: (0, 0)),
                  pl.BlockSpec(ph.shape, lambda b: (0, 0))],
        out_specs=pl.BlockSpec((None, n, KV_COLS), lambda b: (b, 0, 0)),
        out_shape=jax.ShapeDtypeStruct((nb, n, KV_COLS), F32),
        compiler_params=_params("arbitrary"),
    )(x, w1g, w2bd, ph)


PAGES_PER_STEP = 8
CHUNKS_PER_PAGE = PAGE_SIZE // CMP_STRIDE


def _compress_sample_kernel(pt_ref, *refs):
    page_refs = refs[:PAGES_PER_STEP]
    w1_ref, w2_ref, ph_ref, o_ref, x_s = refs[PAGES_PER_STEP:]
    p = pl.program_id(1)
    for k in range(PAGES_PER_STEP):
        r0 = pl.multiple_of((p * PAGES_PER_STEP + k) * CHUNKS_PER_PAGE, CHUNKS_PER_PAGE)
        x_s[pl.ds(r0, CHUNKS_PER_PAGE), :] = page_refs[k][...]

    @pl.when(p == pl.num_programs(1) - 1)
    def _():
        o_ref[...] = _compress_rows(x_s[...], w1_ref, w2_ref, ph_ref)


def _compress_sample(cache, page_table, w1g, w2bd, ph):
    s, n_pages = page_table.shape
    n = n_pages * CHUNKS_PER_PAGE
    pages = cache.reshape(cache.shape[0], CHUNKS_PER_PAGE, CHUNK_COLS)

    def page_spec(k):
        return pl.BlockSpec((None, CHUNKS_PER_PAGE, CHUNK_COLS),
                            lambda b, p, pt: (pt[b, p * PAGES_PER_STEP + k], 0, 0))

    const = lambda shape: pl.BlockSpec(shape, lambda b, p, pt: (0, 0))
    return pl.pallas_call(
        _compress_sample_kernel,
        grid_spec=pltpu.PrefetchScalarGridSpec(
            num_scalar_prefetch=1,
            grid=(s, n_pages // PAGES_PER_STEP),
            in_specs=[page_spec(k) for k in range(PAGES_PER_STEP)]
            + [const(w1g.shape), const(w2bd.shape), const(ph.shape)],
            out_specs=pl.BlockSpec((None, n, KV_COLS), lambda b, p, pt: (b, 0, 0)),
            scratch_shapes=[pltpu.VMEM((n, CHUNK_COLS), F32)]),
        out_shape=jax.ShapeDtypeStruct((s, n, KV_COLS), F32),
        compiler_params=_params("arbitrary", "arbitrary"),
    )(page_table, *([pages] * PAGES_PER_STEP), w1g, w2bd, ph)


def _select_top(sc, top, want_idx):
    rows, nlane = sc.shape
    j = lax.broadcasted_iota(jnp.int32, (rows, nlane), 1).astype(F32)
    mask = jnp.zeros((rows, nlane), F32)
    picks = []
    for _ in range(top):
        m = jnp.max(sc, axis=-1, keepdims=True)
        idx = jnp.min(jnp.where(sc == m, j, float(nlane)), axis=-1, keepdims=True)
        pick = j == idx
        mask = jnp.where(pick, 1.0, mask)
        sc = jnp.where(pick, REMOVED, sc)
        picks.append(idx)
    if want_idx:
        return mask, jnp.concatenate(picks, axis=1).astype(jnp.int32)
    return mask


def _selection_scores(score, blk, qpos, nsel):
    cur = qpos // SEL_BLOCK
    forced = (blk == 0) | (blk == cur) | (blk == cur - 1)
    sc = jnp.where(forced, FORCE_SCORE, jnp.where(blk * SEL_BLOCK <= qpos, score, -1.0))
    return jnp.where(blk < nsel, sc, REMOVED)


def _softmax_rows(s, valid):
    s = jnp.where(valid, s, NEG_INF)
    p = jnp.exp(s - jnp.max(s, axis=-1, keepdims=True))
    return p / jnp.sum(p, axis=-1, keepdims=True)


def _cattn_prompt_kernel(q_ref, ck_ref, bc_ref, cov_ref, oc_ref, sel_ref, *, tq, nc, nsel, top):
    qpos = pl.program_id(1) * tq + lax.broadcasted_iota(jnp.int32, (tq, 1), 0)
    n_idx = lax.broadcasted_iota(jnp.int32, (1, nc), 1)
    valid = (qpos >= n_idx * CMP_STRIDE + (CMP_LEN - 1)) & (n_idx < nc - 1)
    anyv = jnp.max(valid.astype(F32), axis=-1, keepdims=True)
    blk = lax.broadcasted_iota(jnp.int32, (1, nsel), 1)
    outs, masks = [], []
    for g in range(N_KV):
        kv = ck_ref[:, g * LANE:(g + 1) * LANE].astype(BF16)
        k, v = kv[:, :HEAD_DIM], kv[:, HEAD_DIM:]
        psum = jnp.zeros((tq, nc), F32)
        for r in range(GROUP):
            h = g * GROUP + r
            qh = q_ref[:, h * HEAD_DIM:(h + 1) * HEAD_DIM].astype(BF16)
            p = _softmax_rows(_nt(qh, k) * SCALE + bc_ref[h], valid) * anyv
            outs.append(_mm(p.astype(BF16), v))
            psum = psum + p
        score = jnp.dot(psum, cov_ref[...], precision=HIGHEST, preferred_element_type=F32)
        masks.append(_select_top(_selection_scores(score, blk, qpos, nsel), top, False))
    oc_ref[...] = jnp.concatenate(outs, axis=1)
    sel_ref[...] = jnp.concatenate(masks, axis=1)


def _covers(n_cmp_rows, n_cmp, n_sel, n_sel_cols):
    start = np.arange(n_cmp_rows)[:, None] * CMP_STRIDE
    blk = np.arange(n_sel_cols)[None, :]
    cov = ((start < (blk + 1) * SEL_BLOCK) & (start + CMP_LEN > blk * SEL_BLOCK)
           & (np.arange(n_cmp_rows)[:, None] < n_cmp) & (blk < n_sel))
    return jnp.asarray(cov.astype(np.float32))


def _cattn_prompt(q, ck, bc, tq):
    nb, t, _ = q.shape
    nc = ck.shape[1]
    nsel = t // SEL_BLOCK
    top = min(SEL_TOP, nsel)
    cov = _covers(nc, nc - 1, nsel, nsel)
    return pl.pallas_call(
        functools.partial(_cattn_prompt_kernel, tq=tq, nc=nc, nsel=nsel, top=top),
        grid=(nb, t // tq),
        in_specs=[pl.BlockSpec((None, tq, D_ATT), lambda b, i: (b, i, 0)),
                  pl.BlockSpec((None, nc, KV_COLS), lambda b, i: (b, 0, 0)),
                  pl.BlockSpec((N_HEADS, tq, nc), lambda b, i: (0, i, 0)),
                  pl.BlockSpec(cov.shape, lambda b, i: (0, 0))],
        out_specs=[pl.BlockSpec((None, tq, D_ATT), lambda b, i: (b, i, 0)),
                   pl.BlockSpec((None, tq, N_KV * nsel), lambda b, i: (b, i, 0))],
        out_shape=[jax.ShapeDtypeStruct((nb, t, D_ATT), F32),
                   jax.ShapeDtypeStruct((nb, t, N_KV * nsel), F32)],
        compiler_params=_params("arbitrary", "arbitrary"),
    )(q, ck, bc, cov)


SAMPLES_PER_STEP = 8


def _cattn_sample_kernel(q_ref, ck_ref, bsc_ref, cov_ref, oc_ref, idx_ref, *, past, nc, nsel, top):
    n_idx = lax.broadcasted_iota(jnp.int32, (1, nc), 1)
    valid = (past >= n_idx * CMP_STRIDE + (CMP_LEN - 1)) & (n_idx < nc - 1)
    anyv = jnp.max(valid.astype(F32), axis=-1, keepdims=True)
    nlane = cov_ref.shape[1]
    blk = lax.broadcasted_iota(jnp.int32, (1, nlane), 1)
    idxs = []
    for g in range(N_KV):
        psums = []
        for i in range(SAMPLES_PER_STEP):
            kv = ck_ref[i, :, g * LANE:(g + 1) * LANE].astype(BF16)
            q4 = q_ref[i, g * GROUP:(g + 1) * GROUP, :].astype(BF16)
            s = _nt(q4, kv[:, :HEAD_DIM]) * SCALE + bsc_ref[g * GROUP:(g + 1) * GROUP, :]
            p = _softmax_rows(s, valid) * anyv
            oc_ref[i, g * GROUP:(g + 1) * GROUP, :] = _mm(p.astype(BF16), kv[:, HEAD_DIM:])
            psums.append(jnp.sum(p, axis=0, keepdims=True))
        score = jnp.dot(jnp.concatenate(psums, axis=0), cov_ref[...], precision=HIGHEST,
                        preferred_element_type=F32)
        _, idx = _select_top(_selection_scores(score, blk, past, nsel), top, True)
        idxs.append(idx)
    idx_ref[...] = jnp.concatenate(idxs, axis=1)


def _cattn_sample(q, ck, bsc, past):
    s = q.shape[0]
    nc = ck.shape[1]
    nsel = -(-(past + 1) // SEL_BLOCK)
    top = min(SEL_TOP, nsel)
    nlane = -(-nsel // LANE) * LANE
    cov = _covers(nc, nc - 1, nsel, nlane)
    g = SAMPLES_PER_STEP
    return pl.pallas_call(
        functools.partial(_cattn_sample_kernel, past=past, nc=nc, nsel=nsel, top=top),
        grid=(s // g,),
        in_specs=[pl.BlockSpec((g, N_HEADS, HEAD_DIM), lambda i: (i, 0, 0)),
                  pl.BlockSpec((g, nc, KV_COLS), lambda i: (i, 0, 0)),
                  pl.BlockSpec(bsc.shape, lambda i: (0, 0)),
                  pl.BlockSpec(cov.shape, lambda i: (0, 0))],
        out_specs=[pl.BlockSpec((g, N_HEADS, HEAD_DIM), lambda i: (i, 0, 0)),
                   pl.BlockSpec((g, N_KV * top), lambda i: (i, 0))],
        out_shape=[jax.ShapeDtypeStruct((s, N_HEADS, HEAD_DIM), F32),
                   jax.ShapeDtypeStruct((s, N_KV * top), jnp.int32)],
        compiler_params=_params("arbitrary"),
    )(q.reshape(s, N_HEADS, HEAD_DIM), ck, bsc, cov)


def _flash_tile(q4, kv, bias4, pen, m_s, l_s, acc_s):
    s = _nt(q4, kv) * SCALE + bias4 + jnp.concatenate([pen] * GROUP, axis=0)
    m_old = m_s[...]
    m_new = jnp.maximum(m_old, jnp.max(s, axis=-1, keepdims=True))
    alpha = jnp.exp(m_old - m_new)
    p = jnp.exp(s - m_new)
    l_s[...] = alpha * l_s[...] + jnp.sum(p, axis=-1, keepdims=True)
    acc_s[...] = alpha * acc_s[...] + _mm(p.astype(BF16), kv)
    m_s[...] = m_new


def _sattn_prompt_kernel(q_ref, sel_ref, kvs_ref, kvw_ref, bt_ref, os_ref, ow_ref,
                         m_s, l_s, acc_s, *, nsel, nd, nwb):
    c = pl.program_id(1)
    rows = 4 * LANE
    qrow = lax.broadcasted_iota(jnp.int32, (LANE, LANE), 0)
    kcol = lax.broadcasted_iota(jnp.int32, (LANE, LANE), 1)
    ej = lax.broadcasted_iota(jnp.int32, (nsel, LANE), 0)
    el = lax.broadcasted_iota(jnp.int32, (nsel, LANE), 1) // SEL_BLOCK
    zpad = jnp.zeros((LANE, LANE - HEAD_DIM), BF16)

    def reset():
        m_s[...] = jnp.full((rows, 1), NEG_INF, F32)
        l_s[...] = jnp.zeros((rows, 1), F32)
        acc_s[...] = jnp.zeros((rows, LANE), F32)

    def result():
        o = acc_s[:, HEAD_DIM:] / l_s[...]
        return jnp.concatenate([o[r * LANE:(r + 1) * LANE] for r in range(GROUP)], axis=1)

    outs_s, outs_w = [], []
    for g in range(N_KV):
        q4 = jnp.concatenate(
            [jnp.concatenate([q_ref[:, (g * GROUP + r) * HEAD_DIM:(g * GROUP + r + 1) * HEAD_DIM]
                              .astype(BF16), zpad], axis=1) for r in range(GROUP)], axis=0)
        selg = sel_ref[:, g * nsel:(g + 1) * nsel].astype(BF16)

        def bias_tiles(kb):
            dl = jnp.minimum(c - kb, nd - 1)
            return bt_ref[g * GROUP:(g + 1) * GROUP, dl].reshape(rows, LANE)

        def sel_step(kb, carry):
            k0 = pl.multiple_of(kb * LANE, LANE)
            kv = kvs_ref[pl.ds(k0, LANE), g * LANE:(g + 1) * LANE].astype(BF16)
            expand = (ej == (LANE // SEL_BLOCK) * kb + el).astype(BF16)
            chosen = _mm(selg, expand) > 0.5
            dist = (c - kb) * LANE + qrow - kcol
            pen = jnp.where(chosen & (dist >= 0), 0.0, NEG_INF)
            _flash_tile(q4, kv, bias_tiles(kb), pen, m_s, l_s, acc_s)
            return carry

        reset()
        lax.fori_loop(0, c + 1, sel_step, 0)
        outs_s.append(result())

        def win_step(kb, carry):
            k0 = pl.multiple_of(kb * LANE, LANE)
            kv = kvw_ref[pl.ds(k0, LANE), g * LANE:(g + 1) * LANE].astype(BF16)
            dist = (c - kb) * LANE + qrow - kcol
            pen = jnp.where((dist >= 0) & (dist < WINDOW), 0.0, NEG_INF)
            _flash_tile(q4, kv, bias_tiles(kb), pen, m_s, l_s, acc_s)
            return carry

        reset()
        lax.fori_loop(jnp.maximum(c - (nwb - 1), 0), c + 1, win_step, 0)
        outs_w.append(result())
    os_ref[...] = jnp.concatenate(outs_s, axis=1)
    ow_ref[...] = jnp.concatenate(outs_w, axis=1)


def _sattn_prompt(q, sel, kvs, kvw, bt):
    nb, t, _ = q.shape
    nsel = t // SEL_BLOCK
    nd = bt.shape[1]
    nwb = WINDOW // LANE + 1
    qblk = pl.BlockSpec((None, LANE, D_ATT), lambda b, i: (b, i, 0))
    kvblk = pl.BlockSpec((None, t, KV_COLS), lambda b, i: (b, 0, 0))
    return pl.pallas_call(
        functools.partial(_sattn_prompt_kernel, nsel=nsel, nd=nd, nwb=nwb),
        grid=(nb, t // LANE),
        in_specs=[qblk, pl.BlockSpec((None, LANE, N_KV * nsel), lambda b, i: (b, i, 0)),
                  kvblk, kvblk, pl.BlockSpec(bt.shape, lambda b, i: (0, 0, 0, 0))],
        out_specs=[qblk, qblk],
        out_shape=[jax.ShapeDtypeStruct((nb, t, D_ATT), F32)] * 2,
        scratch_shapes=[pltpu.VMEM((4 * LANE, 1), F32), pltpu.VMEM((4 * LANE, 1), F32),
                        pltpu.VMEM((4 * LANE, LANE), F32)],
        compiler_params=_params("arbitrary", "arbitrary"),
    )(q, sel, kvs, kvw, bt)


def _sattn_sample_kernel(idx_ref, pt_ref, *refs, top, npb, wb):
    nblk = N_KV * top
    blk_refs = refs[:nblk]
    (q_ref, ksn_ref, kwn_ref, win_ref, bss_ref, bsw_ref, b0_ref, os_ref, ow_ref) = refs[nblk:]
    b = pl.program_id(0)
    zpad = jnp.zeros((GROUP, LANE - HEAD_DIM), BF16)
    jw = lax.broadcasted_iota(jnp.int32, (1, wb), 1)
    for g in range(N_KV):
        hs = slice(g * GROUP, (g + 1) * GROUP)
        q4f = q_ref[hs, :]
        q4 = jnp.concatenate([q4f.astype(BF16), zpad], axis=1)
        b0 = b0_ref[hs, :]

        def new_logit(new_ref):
            kn = new_ref[:, g * LANE:g * LANE + HEAD_DIM]
            return jnp.sum(q4f * kn, axis=-1, keepdims=True) * SCALE + b0

        logits, blocks = [], []
        for j in range(top):
            ib = idx_ref[b, g * top + j]
            kv = blk_refs[g * top + j][...].astype(BF16)
            s = _nt(q4, kv)[:, :SEL_BLOCK] * SCALE + bss_ref[jnp.minimum(ib, npb - 1), hs, :]
            logits.append(jnp.where(ib < npb, s, NEG_INF))
            blocks.append(kv)
        s_new = new_logit(ksn_ref)
        m = s_new
        for s in logits:
            m = jnp.maximum(m, jnp.max(s, axis=-1, keepdims=True))
        p_new = jnp.exp(s_new - m)
        l = p_new
        acc = p_new * ksn_ref[:, g * LANE:(g + 1) * LANE]
        for s, kv in zip(logits, blocks):
            p = jnp.exp(s - m)
            l = l + jnp.sum(p, axis=-1, keepdims=True)
            acc = acc + _mm(p.astype(BF16), kv)
        os_ref[hs, :] = (acc / l)[:, HEAD_DIM:]

        kvw = win_ref[:, g * LANE:(g + 1) * LANE].astype(BF16)
        s = _nt(q4, kvw) * SCALE + bsw_ref[hs, :]
        s = jnp.where(wb - jw < WINDOW, s, NEG_INF)
        s_new = new_logit(kwn_ref)
        m = jnp.maximum(s_new, jnp.max(s, axis=-1, keepdims=True))
        p = jnp.exp(s - m)
        p_new = jnp.exp(s_new - m)
        l = p_new + jnp.sum(p, axis=-1, keepdims=True)
        acc = p_new * kwn_ref[:, g * LANE:(g + 1) * LANE] + _mm(p.astype(BF16), kvw)
        ow_ref[hs, :] = (acc / l)[:, HEAD_DIM:]


def _sattn_sample(q, idx, page_table, cache_sel, ks_new, kw_new, cache_win, bss, bsw, b0, top):
    s, n_pages = page_table.shape
    npb = bss.shape[0]
    wb = cache_win.shape[1]
    halves = PAGE_SIZE // SEL_BLOCK
    blocks = cache_sel.reshape(cache_sel.shape[0] * halves, SEL_BLOCK, KV_COLS)

    def blk_spec(g, j):
        def imap(b, idx_r, pt_r):
            ib = idx_r[b, g * top + j]
            page = pt_r[b, jnp.minimum(ib // halves, n_pages - 1)]
            return (page * halves + ib % halves, 0, g)
        return pl.BlockSpec((None, SEL_BLOCK, LANE), imap)

    per_b = lambda shape: pl.BlockSpec((None,) + shape, lambda b, i_r, p_r: (b, 0, 0))
    const = lambda shape: pl.BlockSpec(shape, lambda b, i_r, p_r: tuple(0 for _ in shape))
    out = pl.pallas_call(
        functools.partial(_sattn_sample_kernel, top=top, npb=npb, wb=wb),
        grid_spec=pltpu.PrefetchScalarGridSpec(
            num_scalar_prefetch=2,
            grid=(s,),
            in_specs=[blk_spec(g, j) for g in range(N_KV) for j in range(top)]
            + [per_b((N_HEADS, HEAD_DIM)), per_b((1, KV_COLS)), per_b((1, KV_COLS)),
               per_b((wb, KV_COLS)), const(bss.shape), const(bsw.shape), const(b0.shape)],
            out_specs=[per_b((N_HEADS, HEAD_DIM)), per_b((N_HEADS, HEAD_DIM))]),
        out_shape=[jax.ShapeDtypeStruct((s, N_HEADS, HEAD_DIM), F32)] * 2,
        compiler_params=_params("arbitrary"),
    )(idx, page_table, *([blocks] * (N_KV * top)), q.reshape(s, N_HEADS, HEAD_DIM),
      ks_new.reshape(s, 1, KV_COLS), kw_new.reshape(s, 1, KV_COLS), cache_win, bss, bsw, b0)
    return out[0].reshape(s, D_ATT), out[1].reshape(s, D_ATT)


def _outproj_kernel(x_ref, ol_ref, oc_ref, os_ref, ow_ref, gl_ref, gt1_ref, sh2_ref, sc2_ref,
                    gexp_ref, nl_ref, na_ref, n2_ref, wo_ref, x1_ref, h2_ref):
    gates = jax.nn.sigmoid(jnp.dot(gl_ref[...], gexp_ref[...], precision=HIGHEST,
                                   preferred_element_type=F32))
    o_att = (gates[:, :D_ATT] * oc_ref[...] + gates[:, D_ATT:2 * D_ATT] * os_ref[...]
             + gates[:, 2 * D_ATT:] * ow_ref[...])
    n_lru = _rms(ol_ref[...], nl_ref[...]).astype(BF16)
    n_att = _rms(o_att, na_ref[...]).astype(BF16)
    mixed = _mm(n_lru, wo_ref[:D_RNN, :]) + _mm(n_att, wo_ref[D_RNN:, :])
    x1 = x_ref[...] + gt1_ref[...] * mixed
    x1_ref[...] = x1
    h2_ref[...] = (_rms(x1, n2_ref[...]) * (1.0 + sc2_ref[...]) + sh2_ref[...]).astype(BF16)


def _gate_expand():
    e = np.zeros((LANE, 3 * D_ATT), np.float32)
    for br in range(3):
        for h in range(N_HEADS):
            e[br * N_HEADS + h, br * D_ATT + h * HEAD_DIM:br * D_ATT + (h + 1) * HEAD_DIM] = 1.0
    return jnp.asarray(e)


def _outproj(x, o_lru, o_c, o_s, o_w, gl, mod, mod_spec, nl, na, n2, wo, tm):
    n = x.shape[0]
    tok = lambda w: pl.BlockSpec((tm, w), lambda i: (i, 0))
    const = lambda shape: pl.BlockSpec(shape, lambda i: (0, 0))
    gexp = _gate_expand()
    return pl.pallas_call(
        _outproj_kernel,
        grid=(n // tm,),
        in_specs=[tok(D_MODEL), tok(D_RNN), tok(D_ATT), tok(D_ATT), tok(D_ATT), tok(LANE),
                  mod_spec(2), mod_spec(3), mod_spec(4),
                  const(gexp.shape), const(nl.shape), const(na.shape), const(n2.shape),
                  const(wo.shape)],
        out_specs=[tok(D_MODEL), tok(D_MODEL)],
        out_shape=[jax.ShapeDtypeStruct((n, D_MODEL), F32),
                   jax.ShapeDtypeStruct((n, D_MODEL), BF16)],
        compiler_params=_params("arbitrary"),
    )(x, o_lru, o_c, o_s, o_w, gl, mod, mod, mod, gexp, nl, na, n2, wo)


def _extract_top(x, top, row):
    nrow = x.shape[0]
    vals = []
    mask = jnp.zeros(x.shape, F32)
    for _ in range(top):
        m = jnp.max(x, axis=0, keepdims=True)
        idx = jnp.min(jnp.where(x == m, row, float(nrow)), axis=0, keepdims=True)
        pick = row == idx
        mask = jnp.where(pick, 1.0, mask)
        x = jnp.where(pick, REMOVED, x)
        vals.append(m)
    return jnp.concatenate(vals, axis=0), mask


def _peer_score_kernel(h2_ref, wq_ref, bq_ref, sk_ref, s0_ref, a0_ref, s1_ref, e1_ref, t_ref,
                       q_s, *, tn):
    q_s[...] = (_nt(wq_ref[...], h2_ref[...]) + bq_ref[...]).astype(BF16)
    row = lax.broadcasted_iota(jnp.int32, (PEER_KEYS, tn), 0).astype(F32)
    crow = lax.broadcasted_iota(jnp.int32, (PEER_TOPK * PEER_TOPK, tn), 0).astype(F32)

    def head(h, carry):
        r0 = pl.multiple_of(h * 2 * PEER_KEYS, 2 * PEER_KEYS)
        s0 = _mm(sk_ref[2 * h], q_s[pl.ds(r0, PEER_KEYS), :])
        s1 = _mm(sk_ref[2 * h + 1], q_s[pl.ds(r0 + PEER_KEYS, PEER_KEYS), :])
        v0, top0 = _extract_top(s0, PEER_TOPK, row)
        v1, top1 = _extract_top(s1, PEER_TOPK, row)
        cand = jnp.concatenate([v1 + v0[a:a + 1] for a in range(PEER_TOPK)], axis=0)
        best, _ = _extract_top(cand, PEER_TOPK, crow)
        z = jnp.sum(jnp.exp(best - best[0:1]), axis=0, keepdims=True)
        s0_ref[h] = s0
        a0_ref[h] = jnp.where(top0 > 0.5, jnp.exp(s0 - v0[0:1]), 0.0) / z
        s1_ref[h] = jnp.where(top1 > 0.5, s1, NEG_INF)
        e1_ref[h] = jnp.exp(s1 - v1[0:1])
        t_ref[h] = best[PEER_TOPK - 1:PEER_TOPK]
        return carry

    lax.fori_loop(0, PEER_HEADS, head, 0)


def _peer_score(h2, wq_t, bq_t, subkeys, tn):
    n = h2.shape[0]
    dq = wq_t.shape[0]
    tok = pl.BlockSpec((PEER_HEADS, PEER_KEYS, tn), lambda i: (0, 0, i))
    shp = jax.ShapeDtypeStruct((PEER_HEADS, PEER_KEYS, n), F32)
    return pl.pallas_call(
        functools.partial(_peer_score_kernel, tn=tn),
        grid=(n // tn,),
        in_specs=[pl.BlockSpec((tn, D_MODEL), lambda i: (i, 0)),
                  pl.BlockSpec(wq_t.shape, lambda i: (0, 0)),
                  pl.BlockSpec(bq_t.shape, lambda i: (0, 0)),
                  pl.BlockSpec(subkeys.shape, lambda i: (0, 0, 0))],
        out_specs=[tok, tok, tok, tok, pl.BlockSpec((PEER_HEADS, 1, tn), lambda i: (0, 0, i))],
        out_shape=[shp, shp, shp, shp, jax.ShapeDtypeStruct((PEER_HEADS, 1, n), F32)],
        scratch_shapes=[pltpu.VMEM((dq, tn), BF16)],
        compiler_params=_params("arbitrary"),
    )(h2, wq_t, bq_t, subkeys)


def _peer_dense_kernel(h2_ref, u_ref, vt_ref, s0_ref, a0_ref, s1_ref, e1_ref, t_ref, x1_ref,
                       gt2_ref, fg_ref, y_ref, acc_s, act_s, *, tn, eb):
    e = pl.program_id(1)

    @pl.when(e == 0)
    def _():
        acc_s[...] = jnp.zeros(acc_s.shape, F32)

    z = _nt(u_ref[...], h2_ref[...])
    for il in range(eb // PEER_KEYS):
        i = e * (eb // PEER_KEYS) + il
        w = jnp.zeros((PEER_KEYS, tn), F32)
        for h in range(PEER_HEADS):
            c = s1_ref[h] + s0_ref[h, pl.ds(i, 1), :]
            w = w + jnp.where(c >= t_ref[h], e1_ref[h], 0.0) * a0_ref[h, pl.ds(i, 1), :]
        zi = z[il * PEER_KEYS:(il + 1) * PEER_KEYS]
        act_s[il * PEER_KEYS:(il + 1) * PEER_KEYS, :] = (jax.nn.gelu(zi) * w).astype(BF16)
    acc_s[...] += _mm(vt_ref[...], act_s[...])

    @pl.when(e == pl.num_programs(1) - 1)
    def _():
        x = x1_ref[...] + gt2_ref[...] * acc_s[...].T
        y_ref[...] = _rms(x, fg_ref[...])


def _peer_dense(h2, u, vt, scores, x1, mod, mod_spec2, fg, tn, eb):
    n = h2.shape[0]
    n_exp = u.shape[0]
    s0, a0, s1, e1, t = scores
    tok3 = pl.BlockSpec((PEER_HEADS, PEER_KEYS, tn), lambda i, e: (0, 0, i))
    return pl.pallas_call(
        functools.partial(_peer_dense_kernel, tn=tn, eb=eb),
        grid=(n // tn, n_exp // eb),
        in_specs=[pl.BlockSpec((tn, D_MODEL), lambda i, e: (i, 0)),
                  pl.BlockSpec((eb, D_MODEL), lambda i, e: (e, 0)),
                  pl.BlockSpec((D_MODEL, eb), lambda i, e: (0, e)),
                  tok3, tok3, tok3, tok3,
                  pl.BlockSpec((PEER_HEADS, 1, tn), lambda i, e: (0, 0, i)),
                  pl.BlockSpec((tn, D_MODEL), lambda i, e: (i, 0)),
                  mod_spec2(5),
                  pl.BlockSpec((1, D_MODEL), lambda i, e: (0, 0))],
        out_specs=pl.BlockSpec((tn, D_MODEL), lambda i, e: (i, 0)),
        out_shape=jax.ShapeDtypeStruct((n, D_MODEL), F32),
        scratch_shapes=[pltpu.VMEM((D_MODEL, tn), F32), pltpu.VMEM((eb, tn), BF16)],
        compiler_params=_params("arbitrary", "arbitrary"),
    )(h2, u, vt, s0, a0, s1, e1, t, x1, mod, fg)


def _block_diag(w):
    nblk, bw, _ = w.shape
    eye = jnp.eye(nblk, dtype=w.dtype)
    return jnp.einsum('nde,nm->ndme', w, eye).reshape(nblk * bw, nblk * bw)


def kernel(x_prompt, x_sample, cache_cmp_kv, cache_sel_kv, cache_win_kv, state_lru_h, state_conv, page_table, c_prompt, c_sample, ada_w, ada_b, norm1_g, norm2_g, w_in, conv_w, conv_b, lru_wa, lru_ba, lru_wi, lru_bi, lru_lambda, cmp_w1, cmp_w2, cmp_pos, out_norm_lru, out_norm_att, w_out, peer_wq, peer_bq, peer_subkeys, peer_u, peer_v, rel_bias, final_g):
    nb, t, d = x_prompt.shape
    ns = x_sample.shape[0]
    n_pages = page_table.shape[1]
    past = n_pages * PAGE_SIZE
    wb = cache_win_kv.shape[2]
    assert x_sample.shape[1] == 1 and ada_w.shape[0] == 1 and d == D_MODEL
    assert t % (2 * LANE) == 0 and ns % LANE == 0 and past % LANE == 0
    assert n_pages % PAGES_PER_STEP == 0 and wb == min(WINDOW, past)
    row = lambda v: v.reshape(1, -1)

    w_in_p = jnp.pad(w_in[0], ((0, 0), (0, sum(IN_SPLITS) - w_in.shape[2]))).astype(BF16)
    wg = jnp.concatenate([_block_diag(lru_wa[0]), _block_diag(lru_wi[0])], axis=1).astype(BF16)
    bg = row(jnp.concatenate([lru_ba[0], lru_bi[0]]))
    w1r = cmp_w1[0].reshape(2, 2, CMP_STRIDE, HEAD_DIM, HEAD_DIM)
    eye2 = jnp.eye(2, dtype=F32)
    w1g = jnp.einsum('cmsdh,ce->scdmeh', w1r, eye2).reshape(CMP_STRIDE * LANE, 2 * LANE).astype(BF16)
    w2bd = jnp.einsum('che,cf->chfe', cmp_w2[0], eye2).reshape(LANE, LANE).astype(BF16)
    wq_t = peer_wq[0].T.astype(BF16)
    bq_t = peer_bq[0].reshape(-1, 1)
    subk = peer_subkeys[0].reshape(2 * PEER_HEADS, PEER_KEYS, -1).astype(BF16)
    u_b = peer_u[0].astype(BF16)
    vt_b = peer_v[0].T.astype(BF16)
    wo_b = w_out[0].astype(BF16)

    n_c = nb + ns
    c_all = jnp.pad(jnp.concatenate([c_prompt, c_sample]), ((0, -n_c % 8), (0, 0)))
    mod = _ada(c_all, ada_w[0], ada_b[0])
    mod_p = mod[:nb].reshape(nb, 1, 6 * D_MODEL)
    mod_s = mod[nb:n_c]

    tm_p, tm_s = 512, ns
    tn_p, tn_s = 512, ns
    tn_sc = 256 if ns % 256 == 0 else LANE

    def mod_spec_p(tm):
        per = t // tm
        return lambda j: pl.BlockSpec((None, 1, D_MODEL), lambda i, *_: (i // per, 0, j))

    def mod_spec_s(tm):
        return lambda j: pl.BlockSpec((tm, D_MODEL), lambda i, *_: (i, j))

    thr_last = BUCKET_THR[-1]
    nd = min(t // LANE, -(-(thr_last + LANE - 1) // LANE) + 1)
    bt, bc = _bias_prompt(rel_bias, t, nd)
    bsc, bsw, bss = _bias_sample(rel_bias, past, wb)
    bss = jnp.transpose(bss, (1, 0, 2))
    b0 = rel_bias[0].reshape(N_HEADS, 1)

    ph = _poshid(cmp_pos[0], cmp_w1[0])
    lru_args = (conv_w[0], row(conv_b[0]), wg, bg, row(lru_lambda[0]))
    norms = (row(out_norm_lru[0]), row(out_norm_att[0]), row(norm2_g[0]), wo_b)

    xp = x_prompt.reshape(nb * t, d)
    xr, yg, q, kvc, kvs, kvw, gl = _inproj(xp, mod_p, mod_spec_p(tm_p), row(norm1_g[0]), w_in_p, tm_p)
    seq = lambda a: a.reshape(nb, t, a.shape[-1])
    o_lru, h_p, conv_p = _lru_prompt(seq(xr), seq(yg), *lru_args, tc=256)
    ck = _compress_prompt(seq(kvc), w1g, w2bd, ph)
    o_c, sel = _cattn_prompt(seq(q), ck, bc, LANE)
    o_s, o_w = _sattn_prompt(seq(q), sel, seq(kvs), seq(kvw), bt)
    flat = lambda a: a.reshape(nb * t, a.shape[-1])
    x1, h2 = _outproj(xp, flat(o_lru), flat(o_c), flat(o_s), flat(o_w), gl, mod_p,
                      mod_spec_p(tm_p), *norms, tm_p)
    scores = _peer_score(h2, wq_t, bq_t, subk, 256)
    y_p = _peer_dense(h2, u_b, vt_b, scores, x1, mod_p, mod_spec_p(tn_p), row(final_g), tn_p, 1024)

    xs = x_sample.reshape(ns, d)
    xr_s, yg_s, q_s, kvc_s, kvs_s, kvw_s, gl_s = _inproj(xs, mod_s, mod_spec_s(tm_s),
                                                          row(norm1_g[0]), w_in_p, tm_s)
    o_lru_s, h_s = _lru_sample(xr_s, yg_s, state_conv[0], state_lru_h[0], *lru_args)
    ck_s = _compress_sample(cache_cmp_kv[0].reshape(-1, PAGE_SIZE, KV_COLS), page_table, w1g, w2bd, ph)
    o_c_s, idx = _cattn_sample(q_s, ck_s, bsc, past)
    top_s = idx.shape[1] // N_KV
    o_s_s, o_w_s = _sattn_sample(q_s, idx, page_table, cache_sel_kv[0].reshape(-1, PAGE_SIZE, KV_COLS),
                                 kvs_s, kvw_s, cache_win_kv[0].reshape(ns, wb, KV_COLS), bss, bsw, b0,
                                 top_s)
    x1_s, h2_s = _outproj(xs, o_lru_s, o_c_s.reshape(ns, D_ATT), o_s_s, o_w_s, gl_s, mod_s,
                          mod_spec_s(tm_s), *norms, tm_s)
    scores_s = _peer_score(h2_s, wq_t, bq_t, subk, tn_sc if ns % tn_sc == 0 else ns)
    y_s = _peer_dense(h2_s, u_b, vt_b, scores_s, x1_s, mod_s, mod_spec_s(tn_s), row(final_g), tn_s, 1024)

    kv6 = lambda a, n, tt: a.reshape(1, n, tt, N_KV, 2, HEAD_DIM)
    win_p = kv6(kvw, nb, t)[:, :, t - min(WINDOW, t):]
    win_s = jnp.concatenate([cache_win_kv[:, :, 1:], kv6(kvw_s, ns, 1)], axis=2)[:, :, -wb:]
    conv_s = jnp.concatenate([state_conv[:, :, 1:], xr_s.reshape(1, ns, 1, D_RNN)], axis=2)
    return (y_p.reshape(nb, t, d), y_s.reshape(ns, 1, d),
            kv6(kvc, nb, t), kv6(kvc_s, ns, 1),
            kv6(kvs, nb, t), kv6(kvs_s, ns, 1),
            win_p, win_s,
            h_p.reshape(1, nb, D_RNN), h_s.reshape(1, ns, D_RNN),
            conv_p.reshape(1, nb, CONV_W - 1, D_RNN), conv_s)
```

```python
import functools
import math

import numpy as np
import jax
import jax.numpy as jnp
from jax import lax
from jax.experimental import pallas as pl
from jax.experimental.pallas import tpu as pltpu

F32 = jnp.float32
BF16 = jnp.bfloat16
HIGHEST = lax.Precision.HIGHEST

D_MODEL = 1024
D_RNN = 512
LRU_BLOCKS = 8
CONV_W = 4
LRU_C = 8.0
N_HEADS = 8
HEAD_DIM = 64
N_KV = 2
GROUP = 4
D_ATT = 512
KV_COLS = 2 * N_KV * HEAD_DIM
CMP_LEN = 32
CMP_STRIDE = 16
SEL_BLOCK = 64
SEL_TOP = 16
WINDOW = 512
SCALE = HEAD_DIM ** -0.5
NEG_INF = -1e30
REMOVED = -3e38
FORCE_SCORE = 1e6
NUM_BUCKETS = 32
MAX_DISTANCE = 1024
PEER_HEADS = 8
PEER_KEYS = 128
PEER_TOPK = 16
NORM_EPS = 1e-6
PAGE_SIZE = 128

LANE = 128
VMEM_LIMIT = 56 * 1024 * 1024


def _bucket_thresholds():
    d = np.arange(0, 1 << 15, dtype=np.int64)
    max_exact = NUM_BUCKETS // 2
    df = np.maximum(d, 1).astype(np.float32)
    ratio = (np.log(df / np.float32(max_exact)) / np.float32(math.log(MAX_DISTANCE / max_exact))
             * np.float32(NUM_BUCKETS - max_exact))
    large = np.minimum(max_exact + ratio.astype(np.int32), NUM_BUCKETS - 1)
    bucket = np.where(d < max_exact, d, large)
    assert np.all(np.diff(bucket) >= 0)
    return tuple(int(np.argmax(bucket >= k)) for k in range(1, NUM_BUCKETS))


BUCKET_THR = _bucket_thresholds()


def _params(*sem):
    return pltpu.CompilerParams(dimension_semantics=sem or None, vmem_limit_bytes=VMEM_LIMIT)


def _nt(a, b):
    return lax.dot_general(a, b, (((1,), (1,)), ((), ())), preferred_element_type=F32)


def _mm(a, b):
    return jnp.dot(a, b, preferred_element_type=F32)


def _rms(x, g):
    return x * lax.rsqrt(jnp.mean(x * x, axis=-1, keepdims=True) + NORM_EPS) * g


def _bias_lookup(dist, rb_ref, h):
    b = jnp.full(dist.shape, rb_ref[0, h], F32)
    for k in range(1, NUM_BUCKETS):
        b = jnp.where(dist >= BUCKET_THR[k - 1], rb_ref[k, h], b)
    return b


def _ada_kernel(c_ref, w_ref, b_ref, o_ref):
    c = c_ref[...]
    o_ref[...] = jnp.dot(jax.nn.silu(c), w_ref[...], precision=HIGHEST,
                         preferred_element_type=F32) + b_ref[...]


def _ada(c_all, w, b):
    rows, d = c_all.shape
    cols = w.shape[1]
    tn = 512
    return pl.pallas_call(
        _ada_kernel,
        grid=(cols // tn,),
        in_specs=[pl.BlockSpec((rows, d), lambda j: (0, 0)),
                  pl.BlockSpec((d, tn), lambda j: (0, j)),
                  pl.BlockSpec((1, tn), lambda j: (0, j))],
        out_specs=pl.BlockSpec((rows, tn), lambda j: (0, j)),
        out_shape=jax.ShapeDtypeStruct((rows, cols), F32),
        compiler_params=_params("arbitrary"),
    )(c_all, w, b.reshape(1, cols))


def _bias_prompt_kernel(rb_ref, bt_ref, bc_ref, *, nd, t, nc):
    h = pl.program_id(0)
    row = lax.broadcasted_iota(jnp.int32, (LANE, LANE), 0)
    col = lax.broadcasted_iota(jnp.int32, (LANE, LANE), 1)
    for dl in range(nd):
        bt_ref[dl] = _bias_lookup(dl * LANE + row - col, rb_ref, h)
    rq = lax.broadcasted_iota(jnp.int32, (LANE, nc), 0)
    ends = lax.broadcasted_iota(jnp.int32, (LANE, nc), 1) * CMP_STRIDE + (CMP_LEN - 1)

    def chunk(i, carry):
        r0 = pl.multiple_of(i * LANE, LANE)
        bc_ref[pl.ds(r0, LANE), :] = _bias_lookup(r0 + rq - ends, rb_ref, h)
        return carry

    lax.fori_loop(0, t // LANE, chunk, 0)


def _bias_prompt(rel_bias, t, nd):
    nc = t // CMP_STRIDE
    return pl.pallas_call(
        functools.partial(_bias_prompt_kernel, nd=nd, t=t, nc=nc),
        grid=(N_HEADS,),
        in_specs=[pl.BlockSpec(memory_space=pltpu.SMEM)],
        out_specs=[pl.BlockSpec((None, nd, LANE, LANE), lambda h: (h, 0, 0, 0)),
                   pl.BlockSpec((None, t, nc), lambda h: (h, 0, 0))],
        out_shape=[jax.ShapeDtypeStruct((N_HEADS, nd, LANE, LANE), F32),
                   jax.ShapeDtypeStruct((N_HEADS, t, nc), F32)],
        compiler_params=_params("arbitrary"),
    )(rel_bias)


def _bias_sample_kernel(rb_ref, bsc_ref, bsw_ref, bss_ref, *, past, ncs, wb, n_pages):
    ends = lax.broadcasted_iota(jnp.int32, (1, ncs), 1) * CMP_STRIDE + (CMP_LEN - 1)
    jw = lax.broadcasted_iota(jnp.int32, (1, wb), 1)
    kpos = (lax.broadcasted_iota(jnp.int32, (n_pages, PAGE_SIZE), 0) * PAGE_SIZE
            + lax.broadcasted_iota(jnp.int32, (n_pages, PAGE_SIZE), 1))
    for h in range(N_HEADS):
        bsc_ref[h:h + 1, :] = _bias_lookup(past - ends, rb_ref, h)
        bsw_ref[h:h + 1, :] = _bias_lookup(wb - jw, rb_ref, h)
        bss_ref[h] = _bias_lookup(past - kpos, rb_ref, h)


def _bias_sample(rel_bias, past, wb):
    ncs = past // CMP_STRIDE
    n_pages = past // PAGE_SIZE
    return pl.pallas_call(
        functools.partial(_bias_sample_kernel, past=past, ncs=ncs, wb=wb, n_pages=n_pages),
        in_specs=[pl.BlockSpec(memory_space=pltpu.SMEM)],
        out_shape=[jax.ShapeDtypeStruct((N_HEADS, ncs), F32),
                   jax.ShapeDtypeStruct((N_HEADS, wb), F32),
                   jax.ShapeDtypeStruct((N_HEADS, n_pages, PAGE_SIZE), F32)],
        compiler_params=_params(),
        name="bias_sample",
    )(rel_bias)


IN_SPLITS = (D_RNN, D_RNN, D_ATT, KV_COLS, KV_COLS, KV_COLS, LANE)


def _inproj_kernel(x_ref, sh_ref, sc_ref, g_ref, w_ref, *out_refs):
    h = _rms(x_ref[...], g_ref[...]) * (1.0 + sc_ref[...]) + sh_ref[...]
    z = _mm(h.astype(BF16), w_ref[...])
    off = 0
    for o_ref, wdt in zip(out_refs, IN_SPLITS):
        o_ref[...] = z[:, off:off + wdt]
        off += wdt


def _inproj(x, mod, mod_spec, g1, w_in_p, tm):
    n = x.shape[0]
    cols = w_in_p.shape[1]
    return pl.pallas_call(
        _inproj_kernel,
        grid=(n // tm,),
        in_specs=[pl.BlockSpec((tm, D_MODEL), lambda i: (i, 0)),
                  mod_spec(0), mod_spec(1),
                  pl.BlockSpec((1, D_MODEL), lambda i: (0, 0)),
                  pl.BlockSpec((D_MODEL, cols), lambda i: (0, 0))],
        out_specs=[pl.BlockSpec((tm, wdt), lambda i: (i, 0)) for wdt in IN_SPLITS],
        out_shape=[jax.ShapeDtypeStruct((n, wdt), F32) for wdt in IN_SPLITS],
        compiler_params=_params("arbitrary"),
    )(x, mod, mod, g1, w_in_p)


def _lru_gates(xc, wg_ref, bg_ref, lam_ref):
    gates = _mm(xc.astype(BF16), wg_ref[...]) + bg_ref[...]
    r = jax.nn.sigmoid(gates[:, :D_RNN])
    i = jax.nn.sigmoid(gates[:, D_RNN:])
    log_a = -LRU_C * r * jax.nn.softplus(-lam_ref[...])
    a = jnp.exp(log_a)
    u = jnp.sqrt(-jnp.tanh(log_a) * (a * a + 1.0)) * i * xc
    return a, u


def _lru_prompt_kernel(xr_ref, yg_ref, cw_ref, cb_ref, wg_ref, bg_ref, lam_ref,
                       o_ref, hl_ref, cv_ref, xp_s, a_s, u_s, hs_s, h_s, *, nb, tc):
    t = pl.program_id(0)

    @pl.when(t == 0)
    def _():
        xp_s[:, 0:8, :] = jnp.zeros((nb, 8, D_RNN), F32)
        h_s[...] = jnp.zeros((nb, 1, D_RNN), F32)

    @pl.when(t > 0)
    def _():
        xp_s[:, 0:8, :] = xp_s[:, tc:tc + 8, :]

    xp_s[:, 8:, :] = xr_ref[...]
    xc = xp_s[:, 5:5 + tc, :] * cw_ref[0:1, :]
    for k in range(1, CONV_W):
        xc = xc + xp_s[:, 5 + k:5 + k + tc, :] * cw_ref[k:k + 1, :]
    xc = xc + cb_ref[...]
    a, u = _lru_gates(xc.reshape(nb * tc, D_RNN), wg_ref, bg_ref, lam_ref)
    a_s[...] = a.reshape(nb, tc, D_RNN)
    u_s[...] = u.reshape(nb, tc, D_RNN)

    def step(tt, h):
        h = a_s[:, pl.ds(tt, 1), :] * h + u_s[:, pl.ds(tt, 1), :]
        hs_s[:, pl.ds(tt, 1), :] = h
        return h

    h = lax.fori_loop(0, tc, step, h_s[...], unroll=8)
    h_s[...] = h
    o_ref[...] = hs_s[...] * jax.nn.gelu(yg_ref[...])
    hl_ref[...] = h
    cv_ref[...] = xp_s[:, tc + 8 - (CONV_W - 1):tc + 8, :]


def _lru_prompt(xr, yg, cw, cb, wg, bg, lam, tc):
    nb, t, _ = xr.shape
    blk = pl.BlockSpec((nb, tc, D_RNN), lambda i: (0, i, 0))
    full = lambda shape: pl.BlockSpec(shape, lambda i: tuple(0 for _ in shape))
    return pl.pallas_call(
        functools.partial(_lru_prompt_kernel, nb=nb, tc=tc),
        grid=(t // tc,),
        in_specs=[blk, blk, full(cw.shape), full(cb.shape), full(wg.shape), full(bg.shape),
                  full(lam.shape)],
        out_specs=[blk, full((nb, 1, D_RNN)), full((nb, CONV_W - 1, D_RNN))],
        out_shape=[jax.ShapeDtypeStruct((nb, t, D_RNN), F32),
                   jax.ShapeDtypeStruct((nb, 1, D_RNN), F32),
                   jax.ShapeDtypeStruct((nb, CONV_W - 1, D_RNN), F32)],
        scratch_shapes=[pltpu.VMEM((nb, tc + 8, D_RNN), F32), pltpu.VMEM((nb, tc, D_RNN), F32),
                        pltpu.VMEM((nb, tc, D_RNN), F32), pltpu.VMEM((nb, tc, D_RNN), F32),
                        pltpu.VMEM((nb, 1, D_RNN), F32)],
        compiler_params=_params("arbitrary"),
    )(xr, yg, cw, cb, wg, bg, lam)


def _lru_sample_kernel(x_ref, yg_ref, b0_ref, b1_ref, b2_ref, h0_ref, cw_ref, cb_ref, wg_ref,
                       bg_ref, lam_ref, o_ref, hn_ref):
    xc = b0_ref[...] * cw_ref[0:1, :]
    xc = xc + b1_ref[...] * cw_ref[1:2, :]
    xc = xc + b2_ref[...] * cw_ref[2:3, :]
    xc = xc + x_ref[...] * cw_ref[3:4, :]
    xc = xc + cb_ref[...]
    a, u = _lru_gates(xc, wg_ref, bg_ref, lam_ref)
    h = a * h0_ref[...] + u
    hn_ref[...] = h
    o_ref[...] = h * jax.nn.gelu(yg_ref[...])


def _lru_sample(x, yg, buf, h0, cw, cb, wg, bg, lam):
    s = x.shape[0]
    return pl.pallas_call(
        _lru_sample_kernel,
        out_shape=[jax.ShapeDtypeStruct((s, D_RNN), F32), jax.ShapeDtypeStruct((s, D_RNN), F32)],
        compiler_params=_params(),
    )(x, yg, buf[:, 0], buf[:, 1], buf[:, 2], h0, cw, cb, wg, bg, lam)


CHUNK_COLS = CMP_STRIDE * KV_COLS


def _poshid_kernel(pos_ref, w_ref, o_ref):
    parts = [jnp.dot(pos_ref[c], w_ref[c], precision=HIGHEST, preferred_element_type=F32)[0:1]
             for c in range(2)]
    o_ref[...] = jnp.concatenate(parts, axis=1)


def _poshid(cmp_pos, cmp_w1):
    k = CMP_LEN * HEAD_DIM
    pos = jnp.broadcast_to(cmp_pos.reshape(2, 1, k), (2, 8, k))
    return pl.pallas_call(
        _poshid_kernel,
        out_shape=jax.ShapeDtypeStruct((1, 2 * HEAD_DIM), F32),
        compiler_params=_params(),
    )(pos, cmp_w1.reshape(2, k, HEAD_DIM))


def _compress_rows(x_s, n, w1_ref, w2_ref, ph_ref):
    outs = []
    for g in range(N_KV):
        xg = jnp.concatenate(
            [x_s[g, pl.ds(s, n, stride=CMP_STRIDE), :].astype(BF16) for s in range(CMP_STRIDE)],
            axis=1)
        p = _mm(xg, w1_ref[...])
        hid = ph_ref[...] + p[:, :LANE] + pltpu.roll(p[:, LANE:], n - 1, axis=0)
        outs.append(_mm(jax.nn.gelu(hid).astype(BF16), w2_ref[...]))
    return jnp.concatenate(outs, axis=1)


def _compress_prompt_kernel(x_ref, w1_ref, w2_ref, ph_ref, o_ref, x_s, *, n):
    for g in range(N_KV):
        x_s[g] = x_ref[:, g * LANE:(g + 1) * LANE]
    o_ref[...] = _compress_rows(x_s, n, w1_ref, w2_ref, ph_ref)


def _compress_prompt(kvc, w1g, w2bd, ph):
    nb, t, _ = kvc.shape
    n = t // CMP_STRIDE
    return pl.pallas_call(
        functools.partial(_compress_prompt_kernel, n=n),
        grid=(nb,),
        in_specs=[pl.BlockSpec((None, t, KV_COLS), lambda b: (b, 0, 0)),
                  pl.BlockSpec(w1g.shape, lambda b: (0, 0)),
                  pl.BlockSpec(w2bd.shape, lambda b: (0, 0)),
                  pl.BlockSpec(ph.shape, lambda b: (0, 0))],
        out_specs=pl.BlockSpec((None, n, KV_COLS), lambda b: (b, 0, 0)),
        out_shape=jax.ShapeDtypeStruct((nb, n, KV_COLS), F32),
        scratch_shapes=[pltpu.VMEM((N_KV, t, LANE), F32)],
        compiler_params=_params("arbitrary"),
        name="compress_prompt",
    )(kvc, w1g, w2bd, ph)


PAGES_PER_STEP = 8


def _compress_sample_kernel(pt_ref, *refs, n):
    page_refs = refs[:PAGES_PER_STEP]
    w1_ref, w2_ref, ph_ref, o_ref, x_s = refs[PAGES_PER_STEP:]
    p = pl.program_id(1)
    for k in range(PAGES_PER_STEP):
        r0 = pl.multiple_of((p * PAGES_PER_STEP + k) * PAGE_SIZE, PAGE_SIZE)
        for g in range(N_KV):
            x_s[g, pl.ds(r0, PAGE_SIZE), :] = page_refs[k][g].T

    @pl.when(p == pl.num_programs(1) - 1)
    def _():
        o_ref[...] = _compress_rows(x_s, n, w1_ref, w2_ref, ph_ref)


def _compress_sample(cache_t, page_table, w1g, w2bd, ph):
    s, n_pages = page_table.shape
    n = n_pages * PAGE_SIZE // CMP_STRIDE

    def page_spec(k):
        return pl.BlockSpec((None, N_KV, LANE, PAGE_SIZE),
                            lambda b, p, pt: (pt[b, p * PAGES_PER_STEP + k], 0, 0, 0))

    const = lambda shape: pl.BlockSpec(shape, lambda b, p, pt: (0, 0))
    return pl.pallas_call(
        functools.partial(_compress_sample_kernel, n=n),
        grid_spec=pltpu.PrefetchScalarGridSpec(
            num_scalar_prefetch=1,
            grid=(s, n_pages // PAGES_PER_STEP),
            in_specs=[page_spec(k) for k in range(PAGES_PER_STEP)]
            + [const(w1g.shape), const(w2bd.shape), const(ph.shape)],
            out_specs=pl.BlockSpec((None, n, KV_COLS), lambda b, p, pt: (b, 0, 0)),
            scratch_shapes=[pltpu.VMEM((N_KV, n_pages * PAGE_SIZE, LANE), F32)]),
        out_shape=jax.ShapeDtypeStruct((s, n, KV_COLS), F32),
        compiler_params=_params("arbitrary", "arbitrary"),
        name="compress_sample",
    )(page_table, *([cache_t] * PAGES_PER_STEP), w1g, w2bd, ph)


def _select_top(sc, top, want_idx):
    rows, nlane = sc.shape
    j = lax.broadcasted_iota(jnp.int32, (rows, nlane), 1).astype(F32)
    mask = jnp.zeros((rows, nlane), F32)
    picks = []
    for _ in range(top):
        m = jnp.max(sc, axis=-1, keepdims=True)
        idx = jnp.min(jnp.where(sc == m, j, float(nlane)), axis=-1, keepdims=True)
        pick = j == idx
        mask = jnp.where(pick, 1.0, mask)
        sc = jnp.where(pick, REMOVED, sc)
        picks.append(idx)
    if want_idx:
        return mask, jnp.concatenate(picks, axis=1).astype(jnp.int32)
    return mask


def _selection_scores(score, blk, qpos, nsel):
    cur = qpos // SEL_BLOCK
    forced = (blk == 0) | (blk == cur) | (blk == cur - 1)
    sc = jnp.where(forced, FORCE_SCORE, jnp.where(blk * SEL_BLOCK <= qpos, score, -1.0))
    return jnp.where(blk < nsel, sc, REMOVED)


def _softmax_rows(s, valid):
    s = jnp.where(valid, s, NEG_INF)
    p = jnp.exp(s - jnp.max(s, axis=-1, keepdims=True))
    return p / jnp.sum(p, axis=-1, keepdims=True)


def _cattn_prompt_kernel(q_ref, ck_ref, bc_ref, cov_ref, oc_ref, sel_ref, *, tq, nc, nsel, top):
    qpos = pl.program_id(1) * tq + lax.broadcasted_iota(jnp.int32, (tq, 1), 0)
    n_idx = lax.broadcasted_iota(jnp.int32, (1, nc), 1)
    valid = (qpos >= n_idx * CMP_STRIDE + (CMP_LEN - 1)) & (n_idx < nc - 1)
    anyv = jnp.max(valid.astype(F32), axis=-1, keepdims=True)
    blk = lax.broadcasted_iota(jnp.int32, (1, nsel), 1)
    outs, masks = [], []
    for g in range(N_KV):
        kv = ck_ref[:, g * LANE:(g + 1) * LANE].astype(BF16)
        k, v = kv[:, :HEAD_DIM], kv[:, HEAD_DIM:]
        psum = jnp.zeros((tq, nc), F32)
        for r in range(GROUP):
            h = g * GROUP + r
            qh = q_ref[:, h * HEAD_DIM:(h + 1) * HEAD_DIM].astype(BF16)
            p = _softmax_rows(_nt(qh, k) * SCALE + bc_ref[h], valid) * anyv
            outs.append(_mm(p.astype(BF16), v))
            psum = psum + p
        score = jnp.dot(psum, cov_ref[...], precision=HIGHEST, preferred_element_type=F32)
        masks.append(_select_top(_selection_scores(score, blk, qpos, nsel), top, False))
    oc_ref[...] = jnp.concatenate(outs, axis=1)
    sel_ref[...] = jnp.concatenate(masks, axis=1)


def _covers(n_cmp_rows, n_cmp, n_sel, n_sel_cols):
    start = np.arange(n_cmp_rows)[:, None] * CMP_STRIDE
    blk = np.arange(n_sel_cols)[None, :]
    cov = ((start < (blk + 1) * SEL_BLOCK) & (start + CMP_LEN > blk * SEL_BLOCK)
           & (np.arange(n_cmp_rows)[:, None] < n_cmp) & (blk < n_sel))
    return jnp.asarray(cov.astype(np.float32))


def _cattn_prompt(q, ck, bc, tq):
    nb, t, _ = q.shape
    nc = ck.shape[1]
    nsel = t // SEL_BLOCK
    top = min(SEL_TOP, nsel)
    cov = _covers(nc, nc - 1, nsel, nsel)
    return pl.pallas_call(
        functools.partial(_cattn_prompt_kernel, tq=tq, nc=nc, nsel=nsel, top=top),
        grid=(nb, t // tq),
        in_specs=[pl.BlockSpec((None, tq, D_ATT), lambda b, i: (b, i, 0)),
                  pl.BlockSpec((None, nc, KV_COLS), lambda b, i: (b, 0, 0)),
                  pl.BlockSpec((N_HEADS, tq, nc), lambda b, i: (0, i, 0)),
                  pl.BlockSpec(cov.shape, lambda b, i: (0, 0))],
        out_specs=[pl.BlockSpec((None, tq, D_ATT), lambda b, i: (b, i, 0)),
                   pl.BlockSpec((None, tq, N_KV * nsel), lambda b, i: (b, i, 0))],
        out_shape=[jax.ShapeDtypeStruct((nb, t, D_ATT), F32),
                   jax.ShapeDtypeStruct((nb, t, N_KV * nsel), F32)],
        compiler_params=_params("arbitrary", "arbitrary"),
    )(q, ck, bc, cov)


SAMPLES_PER_STEP = 8


def _cattn_sample_kernel(q_ref, ck_ref, bsc_ref, cov_ref, oc_ref, idx_ref, *, past, nc, nsel, top):
    n_idx = lax.broadcasted_iota(jnp.int32, (1, nc), 1)
    valid = (past >= n_idx * CMP_STRIDE + (CMP_LEN - 1)) & (n_idx < nc - 1)
    anyv = jnp.max(valid.astype(F32), axis=-1, keepdims=True)
    nlane = cov_ref.shape[1]
    blk = lax.broadcasted_iota(jnp.int32, (1, nlane), 1)
    idxs = []
    for g in range(N_KV):
        psums = []
        for i in range(SAMPLES_PER_STEP):
            kv = ck_ref[i, :, g * LANE:(g + 1) * LANE].astype(BF16)
            q4 = q_ref[i, g * GROUP:(g + 1) * GROUP, :].astype(BF16)
            s = _nt(q4, kv[:, :HEAD_DIM]) * SCALE + bsc_ref[g * GROUP:(g + 1) * GROUP, :]
            p = _softmax_rows(s, valid) * anyv
            oc_ref[i, g * GROUP:(g + 1) * GROUP, :] = _mm(p.astype(BF16), kv[:, HEAD_DIM:])
            psums.append(jnp.sum(p, axis=0, keepdims=True))
        score = jnp.dot(jnp.concatenate(psums, axis=0), cov_ref[...], precision=HIGHEST,
                        preferred_element_type=F32)
        _, idx = _select_top(_selection_scores(score, blk, past, nsel), top, True)
        idxs.append(idx)
    idx_ref[...] = jnp.concatenate(idxs, axis=1)


def _cattn_sample(q, ck, bsc, past):
    s = q.shape[0]
    nc = ck.shape[1]
    nsel = -(-(past + 1) // SEL_BLOCK)
    top = min(SEL_TOP, nsel)
    nlane = -(-nsel // LANE) * LANE
    cov = _covers(nc, nc - 1, nsel, nlane)
    g = SAMPLES_PER_STEP
    return pl.pallas_call(
        functools.partial(_cattn_sample_kernel, past=past, nc=nc, nsel=nsel, top=top),
        grid=(s // g,),
        in_specs=[pl.BlockSpec((g, N_HEADS, HEAD_DIM), lambda i: (i, 0, 0)),
                  pl.BlockSpec((g, nc, KV_COLS), lambda i: (i, 0, 0)),
                  pl.BlockSpec(bsc.shape, lambda i: (0, 0)),
                  pl.BlockSpec(cov.shape, lambda i: (0, 0))],
        out_specs=[pl.BlockSpec((g, N_HEADS, HEAD_DIM), lambda i: (i, 0, 0)),
                   pl.BlockSpec((g, N_KV * top), lambda i: (i, 0))],
        out_shape=[jax.ShapeDtypeStruct((s, N_HEADS, HEAD_DIM), F32),
                   jax.ShapeDtypeStruct((s, N_KV * top), jnp.int32)],
        compiler_params=_params("arbitrary"),
    )(q.reshape(s, N_HEADS, HEAD_DIM), ck, bsc, cov)


def _flash_pair(q4, kvs, biases, pens, m_ref, acc_ref):
    lane = lax.broadcasted_iota(jnp.int32, (LANE, LANE), 1)
    ss = [_nt(q4, kv) * SCALE + bias + jnp.concatenate([pen] * GROUP, axis=0)
          for kv, bias, pen in zip(kvs, biases, pens)]
    m_old = m_ref[...]
    m_new = jnp.maximum(m_old, jnp.max(jnp.maximum(ss[0], ss[1]), axis=-1, keepdims=True))
    acc = jnp.exp(m_old - m_new) * acc_ref[...]
    for s, kv in zip(ss, kvs):
        acc = acc + _mm(jnp.exp(s - m_new).astype(BF16), jnp.where(lane < HEAD_DIM, 1.0, kv))
    acc_ref[...] = acc
    m_ref[...] = m_new


def _sattn_prompt_kernel(q_ref, sel_ref, kvs_ref, kvw_ref, bt_ref, os_ref, ow_ref,
                         m_s, acc_s, *, nsel, nd, npw):
    c = pl.program_id(1)
    rows = GROUP * LANE
    qrow = lax.broadcasted_iota(jnp.int32, (LANE, LANE), 0)
    kcol = lax.broadcasted_iota(jnp.int32, (LANE, LANE), 1)
    ej = lax.broadcasted_iota(jnp.int32, (nsel, LANE), 0)
    el = lax.broadcasted_iota(jnp.int32, (nsel, LANE), 1) // SEL_BLOCK
    zpad = jnp.zeros((LANE, LANE - HEAD_DIM), BF16)
    q4s = [jnp.concatenate(
        [jnp.concatenate([q_ref[:, (g * GROUP + r) * HEAD_DIM:(g * GROUP + r + 1) * HEAD_DIM]
                          .astype(BF16), zpad], axis=1) for r in range(GROUP)], axis=0)
        for g in range(N_KV)]
    selgs = [sel_ref[:, g * nsel:(g + 1) * nsel].astype(BF16) for g in range(N_KV)]

    def reset():
        m_s[...] = jnp.full(m_s.shape, NEG_INF, F32)
        acc_s[...] = jnp.zeros(acc_s.shape, F32)

    def result():
        outs = []
        for g in range(N_KV):
            acc = acc_s[g]
            o = acc[:, HEAD_DIM:] / acc[:, 0:1]
            outs += [o[r * LANE:(r + 1) * LANE] for r in range(GROUP)]
        return jnp.concatenate(outs, axis=1)

    def bias_tiles(g, kb):
        dl = jnp.clip(c - kb, 0, nd - 1)
        return bt_ref[g * GROUP:(g + 1) * GROUP, dl].reshape(rows, LANE)

    def kv_tile(ref, g, kb):
        k0 = pl.multiple_of(jnp.maximum(kb, 0) * LANE, LANE)
        return ref[pl.ds(k0, LANE), g * LANE:(g + 1) * LANE].astype(BF16)

    def update(ref, g, kbs, pens):
        _flash_pair(q4s[g], [kv_tile(ref, g, kb) for kb in kbs],
                    [bias_tiles(g, kb) for kb in kbs], pens, m_s.at[g], acc_s.at[g])

    def sel_pair(kp, carry):
        kbs = (2 * kp, 2 * kp + 1)
        for g in range(N_KV):
            pens = []
            for kb in kbs:
                expand = (ej == (LANE // SEL_BLOCK) * kb + el).astype(BF16)
                chosen = _mm(selgs[g], expand) > 0.5
                dist = (c - kb) * LANE + qrow - kcol
                pens.append(jnp.where(chosen & (dist >= 0), 0.0, NEG_INF))
            update(kvs_ref, g, kbs, pens)
        return carry

    reset()
    lax.fori_loop(0, (c + 2) // 2, sel_pair, 0)
    os_ref[...] = result()

    reset()
    for kp in range(npw):
        kbs = (c - 2 * (npw - kp) + 1, c - 2 * (npw - kp) + 2)
        pens = []
        for kb in kbs:
            dist = jnp.where(kb >= 0, (c - kb) * LANE + qrow - kcol, -1)
            pens.append(jnp.where((dist >= 0) & (dist < WINDOW), 0.0, NEG_INF))
        for g in range(N_KV):
            update(kvw_ref, g, kbs, pens)
    ow_ref[...] = result()


def _sattn_prompt(q, sel, kvs, kvw, bt):
    nb, t, _ = q.shape
    nsel = t // SEL_BLOCK
    nd = bt.shape[1]
    npw = (WINDOW // LANE + 2) // 2
    qblk = pl.BlockSpec((None, LANE, D_ATT), lambda b, i: (b, i, 0))
    kvblk = pl.BlockSpec((None, t, KV_COLS), lambda b, i: (b, 0, 0))
    return pl.pallas_call(
        functools.partial(_sattn_prompt_kernel, nsel=nsel, nd=nd, npw=npw),
        grid=(nb, t // LANE),
        in_specs=[qblk, pl.BlockSpec((None, LANE, N_KV * nsel), lambda b, i: (b, i, 0)),
                  kvblk, kvblk, pl.BlockSpec(bt.shape, lambda b, i: (0, 0, 0, 0))],
        out_specs=[qblk, qblk],
        out_shape=[jax.ShapeDtypeStruct((nb, t, D_ATT), F32)] * 2,
        scratch_shapes=[pltpu.VMEM((N_KV, GROUP * LANE, 1), F32),
                        pltpu.VMEM((N_KV, GROUP * LANE, LANE), F32)],
        compiler_params=_params("arbitrary", "arbitrary"),
        name="sattn_prompt",
    )(q, sel, kvs, kvw, bt)


def _sattn_sample_kernel(idx_ref, pt_ref, *refs, top, npb, n_pages, wb):
    nblk = N_KV * top
    blk_refs = refs[:nblk]
    (q_ref, ksn_ref, kwn_ref, win_ref, bss_ref, bsw_ref, b0_ref, os_ref, ow_ref) = refs[nblk:]
    b = pl.program_id(0)
    halves = PAGE_SIZE // SEL_BLOCK
    half = lax.broadcasted_iota(jnp.int32, (1, PAGE_SIZE), 1) // SEL_BLOCK
    jw = lax.broadcasted_iota(jnp.int32, (1, wb), 1)
    for g in range(N_KV):
        hs = slice(g * GROUP, (g + 1) * GROUP)
        q4f = q_ref[hs, :]
        q4 = q4f.astype(BF16)
        b0 = b0_ref[hs, :]

        def new_logit(new_ref):
            kn = new_ref[:, g * LANE:g * LANE + HEAD_DIM]
            return jnp.sum(q4f * kn, axis=-1, keepdims=True) * SCALE + b0

        def new_value(new_ref):
            return new_ref[:, g * LANE + HEAD_DIM:(g + 1) * LANE]

        logits, vts = [], []
        for j in range(top):
            ib = idx_ref[b, g * top + j]
            page = jnp.minimum(ib // halves, n_pages - 1)
            s = _mm(q4, blk_refs[g * top + j][0].astype(BF16)) * SCALE + bss_ref[page, hs, :]
            logits.append(jnp.where(half == jnp.where(ib < npb, ib % halves, -1), s, NEG_INF))
            vts.append(blk_refs[g * top + j][1].astype(BF16))
        s_new = new_logit(ksn_ref)
        m = s_new
        for s in logits:
            m = jnp.maximum(m, jnp.max(s, axis=-1, keepdims=True))
        p_new = jnp.exp(s_new - m)
        l = p_new
        acc = p_new * new_value(ksn_ref)
        for s, vt in zip(logits, vts):
            p = jnp.exp(s - m)
            l = l + jnp.sum(p, axis=-1, keepdims=True)
            acc = acc + _nt(p.astype(BF16), vt)
        os_ref[hs, :] = acc / l

        s = _mm(q4, win_ref[g, 0].astype(BF16)) * SCALE + bsw_ref[hs, :]
        s = jnp.where(wb - jw < WINDOW, s, NEG_INF)
        s_new = new_logit(kwn_ref)
        m = jnp.maximum(s_new, jnp.max(s, axis=-1, keepdims=True))
        p = jnp.exp(s - m)
        p_new = jnp.exp(s_new - m)
        l = p_new + jnp.sum(p, axis=-1, keepdims=True)
        acc = p_new * new_value(kwn_ref) + _nt(p.astype(BF16), win_ref[g, 1].astype(BF16))
        ow_ref[hs, :] = acc / l


def _sattn_sample(q, idx, page_table, sel_t, ks_new, kw_new, win_t, bss, bsw, b0, top, npb):
    s, n_pages = page_table.shape
    wb = win_t.shape[-1]
    halves = PAGE_SIZE // SEL_BLOCK

    def blk_spec(g, j):
        def imap(b, idx_r, pt_r):
            ib = idx_r[b, g * top + j]
            return (pt_r[b, jnp.minimum(ib // halves, n_pages - 1)], g, 0, 0, 0)
        return pl.BlockSpec((None, None, 2, HEAD_DIM, PAGE_SIZE), imap)

    per_b = lambda shape: pl.BlockSpec((None,) + shape,
                                       lambda b, i_r, p_r: (b,) + tuple(0 for _ in shape))
    const = lambda shape: pl.BlockSpec(shape, lambda b, i_r, p_r: tuple(0 for _ in shape))
    out = pl.pallas_call(
        functools.partial(_sattn_sample_kernel, top=top, npb=npb, n_pages=n_pages, wb=wb),
        grid_spec=pltpu.PrefetchScalarGridSpec(
            num_scalar_prefetch=2,
            grid=(s,),
            in_specs=[blk_spec(g, j) for g in range(N_KV) for j in range(top)]
            + [per_b((N_HEADS, HEAD_DIM)), per_b((1, KV_COLS)), per_b((1, KV_COLS)),
               per_b(win_t.shape[1:]), const(bss.shape), const(bsw.shape), const(b0.shape)],
            out_specs=[per_b((N_HEADS, HEAD_DIM)), per_b((N_HEADS, HEAD_DIM))]),
        out_shape=[jax.ShapeDtypeStruct((s, N_HEADS, HEAD_DIM), F32)] * 2,
        compiler_params=_params("arbitrary"),
        name="sattn_sample",
    )(idx, page_table, *([sel_t] * (N_KV * top)), q.reshape(s, N_HEADS, HEAD_DIM),
      ks_new.reshape(s, 1, KV_COLS), kw_new.reshape(s, 1, KV_COLS), win_t, bss, bsw, b0)
    return out[0].reshape(s, D_ATT), out[1].reshape(s, D_ATT)


def _outproj_kernel(x_ref, ol_ref, oc_ref, os_ref, ow_ref, gl_ref, gt1_ref, sh2_ref, sc2_ref,
                    gexp_ref, nl_ref, na_ref, n2_ref, wo_ref, x1_ref, h2_ref):
    gates = jax.nn.sigmoid(jnp.dot(gl_ref[...], gexp_ref[...], precision=HIGHEST,
                                   preferred_element_type=F32))
    o_att = (gates[:, :D_ATT] * oc_ref[...] + gates[:, D_ATT:2 * D_ATT] * os_ref[...]
             + gates[:, 2 * D_ATT:] * ow_ref[...])
    n_lru = _rms(ol_ref[...], nl_ref[...]).astype(BF16)
    n_att = _rms(o_att, na_ref[...]).astype(BF16)
    mixed = _mm(n_lru, wo_ref[:D_RNN, :]) + _mm(n_att, wo_ref[D_RNN:, :])
    x1 = x_ref[...] + gt1_ref[...] * mixed
    x1_ref[...] = x1
    h2_ref[...] = (_rms(x1, n2_ref[...]) * (1.0 + sc2_ref[...]) + sh2_ref[...]).astype(BF16)


def _gate_expand():
    e = np.zeros((LANE, 3 * D_ATT), np.float32)
    for br in range(3):
        for h in range(N_HEADS):
            e[br * N_HEADS + h, br * D_ATT + h * HEAD_DIM:br * D_ATT + (h + 1) * HEAD_DIM] = 1.0
    return jnp.asarray(e)


def _outproj(x, o_lru, o_c, o_s, o_w, gl, mod, mod_spec, nl, na, n2, wo, tm):
    n = x.shape[0]
    tok = lambda w: pl.BlockSpec((tm, w), lambda i: (i, 0))
    const = lambda shape: pl.BlockSpec(shape, lambda i: (0, 0))
    gexp = _gate_expand()
    return pl.pallas_call(
        _outproj_kernel,
        grid=(n // tm,),
        in_specs=[tok(D_MODEL), tok(D_RNN), tok(D_ATT), tok(D_ATT), tok(D_ATT), tok(LANE),
                  mod_spec(2), mod_spec(3), mod_spec(4),
                  const(gexp.shape), const(nl.shape), const(na.shape), const(n2.shape),
                  const(wo.shape)],
        out_specs=[tok(D_MODEL), tok(D_MODEL)],
        out_shape=[jax.ShapeDtypeStruct((n, D_MODEL), F32),
                   jax.ShapeDtypeStruct((n, D_MODEL), BF16)],
        compiler_params=_params("arbitrary"),
    )(x, o_lru, o_c, o_s, o_w, gl, mod, mod, mod, gexp, nl, na, n2, wo)


def _extract_top(x, top, row):
    nrow = x.shape[0]
    vals = []
    mask = jnp.zeros(x.shape, F32)
    for _ in range(top):
        m = jnp.max(x, axis=0, keepdims=True)
        idx = jnp.min(jnp.where(x == m, row, float(nrow)), axis=0, keepdims=True)
        pick = row == idx
        mask = jnp.where(pick, 1.0, mask)
        x = jnp.where(pick, REMOVED, x)
        vals.append(m)
    return jnp.concatenate(vals, axis=0), mask


PAIR_COUNTS = tuple(PEER_TOPK // (a + 1) for a in range(PEER_TOPK))
N_PAIRS = sum(PAIR_COUNTS)
PAIR_ROWS = -(-N_PAIRS // 8) * 8


def _peer_score_kernel(h2_ref, wq_ref, bq_ref, sk_ref, thr_ref, a0_ref, s1_ref, e1_ref,
                       q_s, cand_s, *, tn):
    q_s[...] = (_nt(wq_ref[...], h2_ref[...]) + bq_ref[...]).astype(BF16)
    row = lax.broadcasted_iota(jnp.int32, (PEER_KEYS, tn), 0).astype(F32)
    crow = lax.broadcasted_iota(jnp.int32, (PAIR_ROWS, tn), 0).astype(F32)
    cand_s[N_PAIRS:, :] = jnp.full((PAIR_ROWS - N_PAIRS, tn), REMOVED, F32)

    def head(h, carry):
        r0 = pl.multiple_of(h * 2 * PEER_KEYS, 2 * PEER_KEYS)
        s0 = _mm(sk_ref[2 * h], q_s[pl.ds(r0, PEER_KEYS), :])
        s1 = _mm(sk_ref[2 * h + 1], q_s[pl.ds(r0 + PEER_KEYS, PEER_KEYS), :])
        v0, top0 = _extract_top(s0, PEER_TOPK, row)
        v1, top1 = _extract_top(s1, PEER_TOPK, row)
        off = 0
        for a, cnt in enumerate(PAIR_COUNTS):
            cand_s[off:off + cnt, :] = v1[0:cnt] + v0[a:a + 1]
            off += cnt
        best, _ = _extract_top(cand_s[...], PEER_TOPK + 1, crow)
        z = jnp.sum(jnp.exp(best[:PEER_TOPK] - best[0:1]), axis=0, keepdims=True)
        t_mid = 0.5 * (best[PEER_TOPK - 1:PEER_TOPK] + best[PEER_TOPK:PEER_TOPK + 1])
        thr_ref[h] = t_mid - s0
        a0_ref[h] = jnp.where(top0 > 0.5, jnp.exp(s0 - v0[0:1]), 0.0) / z
        s1_ref[h] = jnp.where(top1 > 0.5, s1, NEG_INF)
        e1_ref[h] = jnp.exp(s1 - v1[0:1])
        return carry

    lax.fori_loop(0, PEER_HEADS, head, 0)


def _peer_score(h2, wq_t, bq_t, subkeys, tn):
    n = h2.shape[0]
    dq = wq_t.shape[0]
    tok = pl.BlockSpec((PEER_HEADS, PEER_KEYS, tn), lambda i: (0, 0, i))
    shp = jax.ShapeDtypeStruct((PEER_HEADS, PEER_KEYS, n), F32)
    return pl.pallas_call(
        functools.partial(_peer_score_kernel, tn=tn),
        grid=(n // tn,),
        in_specs=[pl.BlockSpec((tn, D_MODEL), lambda i: (i, 0)),
                  pl.BlockSpec(wq_t.shape, lambda i: (0, 0)),
                  pl.BlockSpec(bq_t.shape, lambda i: (0, 0)),
                  pl.BlockSpec(subkeys.shape, lambda i: (0, 0, 0))],
        out_specs=[tok, tok, tok, tok],
        out_shape=[shp, shp, shp, shp],
        scratch_shapes=[pltpu.VMEM((dq, tn), BF16), pltpu.VMEM((PAIR_ROWS, tn), F32)],
        compiler_params=_params("arbitrary"),
        name="peer_score",
    )(h2, wq_t, bq_t, subkeys)


def _peer_dense_kernel(h2_ref, u_ref, vt_ref, thr_ref, a0_ref, s1_ref, e1_ref, x1_ref,
                       gt2_ref, fg_ref, y_ref, acc_s, act_s, *, tn, eb, lc):
    e = pl.program_id(1)

    @pl.when(e == 0)
    def _():
        acc_s[...] = jnp.zeros(acc_s.shape, F32)

    z = _nt(u_ref[...], h2_ref[...])
    for il in range(eb // PEER_KEYS):
        i = e * (eb // PEER_KEYS) + il
        rs = slice(il * PEER_KEYS, (il + 1) * PEER_KEYS)
        for c0 in range(0, tn, lc):
            ls = slice(c0, c0 + lc)
            w = jnp.zeros((PEER_KEYS, lc), F32)
            for h in range(PEER_HEADS):
                keep = s1_ref[h, :, ls] >= thr_ref[h, pl.ds(i, 1), ls]
                w = w + jnp.where(keep, e1_ref[h, :, ls], 0.0) * a0_ref[h, pl.ds(i, 1), ls]
            act_s[rs, ls] = (jax.nn.gelu(z[rs, ls]) * w).astype(BF16)
    acc_s[...] += _mm(vt_ref[...], act_s[...])

    @pl.when(e == pl.num_programs(1) - 1)
    def _():
        x = x1_ref[...] + gt2_ref[...] * acc_s[...].T
        y_ref[...] = _rms(x, fg_ref[...])


def _peer_dense(h2, u, vt, scores, x1, mod, mod_spec2, fg, tn, eb):
    n = h2.shape[0]
    n_exp = u.shape[0]
    thr, a0, s1, e1 = scores
    tok3 = pl.BlockSpec((PEER_HEADS, PEER_KEYS, tn), lambda i, e: (0, 0, i))
    return pl.pallas_call(
        functools.partial(_peer_dense_kernel, tn=tn, eb=eb, lc=min(tn, 2 * LANE)),
        grid=(n // tn, n_exp // eb),
        in_specs=[pl.BlockSpec((tn, D_MODEL), lambda i, e: (i, 0)),
                  pl.BlockSpec((eb, D_MODEL), lambda i, e: (e, 0)),
                  pl.BlockSpec((D_MODEL, eb), lambda i, e: (0, e)),
                  tok3, tok3, tok3, tok3,
                  pl.BlockSpec((tn, D_MODEL), lambda i, e: (i, 0)),
                  mod_spec2(5),
                  pl.BlockSpec((1, D_MODEL), lambda i, e: (0, 0))],
        out_specs=pl.BlockSpec((tn, D_MODEL), lambda i, e: (i, 0)),
        out_shape=jax.ShapeDtypeStruct((n, D_MODEL), F32),
        scratch_shapes=[pltpu.VMEM((D_MODEL, tn), F32), pltpu.VMEM((eb, tn), BF16)],
        compiler_params=_params("arbitrary", "arbitrary"),
        name="peer_dense",
    )(h2, u, vt, thr, a0, s1, e1, x1, mod, fg)


def _block_diag(w):
    nblk, bw, _ = w.shape
    eye = jnp.eye(nblk, dtype=w.dtype)
    return jnp.einsum('nde,nm->ndme', w, eye).reshape(nblk * bw, nblk * bw)


def kernel(x_prompt, x_sample, cache_cmp_kv, cache_sel_kv, cache_win_kv, state_lru_h, state_conv, page_table, c_prompt, c_sample, ada_w, ada_b, norm1_g, norm2_g, w_in, conv_w, conv_b, lru_wa, lru_ba, lru_wi, lru_bi, lru_lambda, cmp_w1, cmp_w2, cmp_pos, out_norm_lru, out_norm_att, w_out, peer_wq, peer_bq, peer_subkeys, peer_u, peer_v, rel_bias, final_g):
    nb, t, d = x_prompt.shape
    ns = x_sample.shape[0]
    n_pages = page_table.shape[1]
    past = n_pages * PAGE_SIZE
    wb = cache_win_kv.shape[2]
    assert x_sample.shape[1] == 1 and ada_w.shape[0] == 1 and d == D_MODEL
    assert t % (2 * LANE) == 0 and ns % LANE == 0 and past % LANE == 0
    assert n_pages % PAGES_PER_STEP == 0 and wb == min(WINDOW, past)
    row = lambda v: v.reshape(1, -1)

    w_in_p = jnp.pad(w_in[0], ((0, 0), (0, sum(IN_SPLITS) - w_in.shape[2]))).astype(BF16)
    wg = jnp.concatenate([_block_diag(lru_wa[0]), _block_diag(lru_wi[0])], axis=1).astype(BF16)
    bg = row(jnp.concatenate([lru_ba[0], lru_bi[0]]))
    w1r = cmp_w1[0].reshape(2, 2, CMP_STRIDE, HEAD_DIM, HEAD_DIM)
    eye2 = jnp.eye(2, dtype=F32)
    w1g = jnp.einsum('cmsdh,ce->scdmeh', w1r, eye2).reshape(CMP_STRIDE * LANE, 2 * LANE).astype(BF16)
    w2bd = jnp.einsum('che,cf->chfe', cmp_w2[0], eye2).reshape(LANE, LANE).astype(BF16)
    wq_t = peer_wq[0].T.astype(BF16)
    bq_t = peer_bq[0].reshape(-1, 1)
    subk = peer_subkeys[0].reshape(2 * PEER_HEADS, PEER_KEYS, -1).astype(BF16)
    u_b = peer_u[0].astype(BF16)
    vt_b = peer_v[0].T.astype(BF16)
    wo_b = w_out[0].astype(BF16)

    n_c = nb + ns
    c_all = jnp.pad(jnp.concatenate([c_prompt, c_sample]), ((0, -n_c % 8), (0, 0)))
    mod = _ada(c_all, ada_w[0], ada_b[0])
    mod_p = mod[:nb].reshape(nb, 1, 6 * D_MODEL)
    mod_s = mod[nb:n_c]

    tm_p, tm_s = 512, ns
    tn_p, tn_s = 512, ns
    tn_sc = 256 if ns % 256 == 0 else LANE

    def mod_spec_p(tm):
        per = t // tm
        return lambda j: pl.BlockSpec((None, 1, D_MODEL), lambda i, *_: (i // per, 0, j))

    def mod_spec_s(tm):
        return lambda j: pl.BlockSpec((tm, D_MODEL), lambda i, *_: (i, j))

    thr_last = BUCKET_THR[-1]
    nd = min(t // LANE, -(-(thr_last + LANE - 1) // LANE) + 1)
    bt, bc = _bias_prompt(rel_bias, t, nd)
    bsc, bsw, bss = _bias_sample(rel_bias, past, wb)
    bss = jnp.transpose(bss, (1, 0, 2))
    b0 = rel_bias[0].reshape(N_HEADS, 1)

    ph = _poshid(cmp_pos[0], cmp_w1[0])
    lru_args = (conv_w[0], row(conv_b[0]), wg, bg, row(lru_lambda[0]))
    norms = (row(out_norm_lru[0]), row(out_norm_att[0]), row(norm2_g[0]), wo_b)

    xp = x_prompt.reshape(nb * t, d)
    xr, yg, q, kvc, kvs, kvw, gl = _inproj(xp, mod_p, mod_spec_p(tm_p), row(norm1_g[0]), w_in_p, tm_p)
    seq = lambda a: a.reshape(nb, t, a.shape[-1])
    o_lru, h_p, conv_p = _lru_prompt(seq(xr), seq(yg), *lru_args, tc=256)
    ck = _compress_prompt(seq(kvc), w1g, w2bd, ph)
    o_c, sel = _cattn_prompt(seq(q), ck, bc, LANE)
    o_s, o_w = _sattn_prompt(seq(q), sel, seq(kvs), seq(kvw), bt)
    flat = lambda a: a.reshape(nb * t, a.shape[-1])
    x1, h2 = _outproj(xp, flat(o_lru), flat(o_c), flat(o_s), flat(o_w), gl, mod_p,
                      mod_spec_p(tm_p), *norms, tm_p)
    scores = _peer_score(h2, wq_t, bq_t, subk, 256)
    y_p = _peer_dense(h2, u_b, vt_b, scores, x1, mod_p, mod_spec_p(tn_p), row(final_g), tn_p, 1024)

    xs = x_sample.reshape(ns, d)
    xr_s, yg_s, q_s, kvc_s, kvs_s, kvw_s, gl_s = _inproj(xs, mod_s, mod_spec_s(tm_s),
                                                          row(norm1_g[0]), w_in_p, tm_s)
    o_lru_s, h_s = _lru_sample(xr_s, yg_s, state_conv[0], state_lru_h[0], *lru_args)
    rows_minor = lambda a: jnp.transpose(a, (0, 2, 3, 4, 1))
    cmp_t = rows_minor(cache_cmp_kv[0]).reshape(-1, N_KV, LANE, PAGE_SIZE)
    ck_s = _compress_sample(cmp_t, page_table, w1g, w2bd, ph)
    o_c_s, idx = _cattn_sample(q_s, ck_s, bsc, past)
    top_s = idx.shape[1] // N_KV
    o_s_s, o_w_s = _sattn_sample(q_s, idx, page_table, rows_minor(cache_sel_kv[0]), kvs_s, kvw_s,
                                 rows_minor(cache_win_kv[0]), bss, bsw, b0, top_s,
                                 past // SEL_BLOCK)
    x1_s, h2_s = _outproj(xs, o_lru_s, o_c_s.reshape(ns, D_ATT), o_s_s, o_w_s, gl_s, mod_s,
                          mod_spec_s(tm_s), *norms, tm_s)
    scores_s = _peer_score(h2_s, wq_t, bq_t, subk, tn_sc if ns % tn_sc == 0 else ns)
    y_s = _peer_dense(h2_s, u_b, vt_b, scores_s, x1_s, mod_s, mod_spec_s(tn_s), row(final_g), tn_s, 1024)

    kv6 = lambda a, n, tt: a.reshape(1, n, tt, N_KV, 2, HEAD_DIM)
    win_p = kv6(kvw, nb, t)[:, :, t - min(WINDOW, t):]
    win_s = jnp.concatenate([cache_win_kv[:, :, 1:], kv6(kvw_s, ns, 1)], axis=2)[:, :, -wb:]
    conv_s = jnp.concatenate([state_conv[:, :, 1:], xr_s.reshape(1, ns, 1, D_RNN)], axis=2)
    return (y_p.reshape(nb, t, d), y_s.reshape(ns, 1, d),
            kv6(kvc, nb, t), kv6(kvc_s, ns, 1),
            kv6(kvs, nb, t), kv6(kvs_s, ns, 1),
            win_p, win_s,
            h_p.reshape(1, nb, D_RNN), h_s.reshape(1, ns, D_RNN),
            conv_p.reshape(1, nb, CONV_W - 1, D_RNN), conv_s)
```

```python
import functools
import math

import numpy as np
import jax
import jax.numpy as jnp
from jax import lax
from jax.experimental import pallas as pl
from jax.experimental.pallas import tpu as pltpu

F32 = jnp.float32
BF16 = jnp.bfloat16
HIGHEST = lax.Precision.HIGHEST

D_MODEL = 1024
D_RNN = 512
LRU_BLOCKS = 8
CONV_W = 4
LRU_C = 8.0
N_HEADS = 8
HEAD_DIM = 64
N_KV = 2
GROUP = 4
D_ATT = 512
KV_COLS = 2 * N_KV * HEAD_DIM
CMP_LEN = 32
CMP_STRIDE = 16
SEL_BLOCK = 64
SEL_TOP = 16
WINDOW = 512
SCALE = HEAD_DIM ** -0.5
NEG_INF = -1e30
REMOVED = -3e38
FORCE_SCORE = 1e6
NUM_BUCKETS = 32
MAX_DISTANCE = 1024
PEER_HEADS = 8
PEER_KEYS = 128
PEER_TOPK = 16
NORM_EPS = 1e-6
PAGE_SIZE = 128

LANE = 128
VMEM_LIMIT = 56 * 1024 * 1024


def _bucket_thresholds():
    d = np.arange(0, 1 << 15, dtype=np.int64)
    max_exact = NUM_BUCKETS // 2
    df = np.maximum(d, 1).astype(np.float32)
    ratio = (np.log(df / np.float32(max_exact)) / np.float32(math.log(MAX_DISTANCE / max_exact))
             * np.float32(NUM_BUCKETS - max_exact))
    large = np.minimum(max_exact + ratio.astype(np.int32), NUM_BUCKETS - 1)
    bucket = np.where(d < max_exact, d, large)
    assert np.all(np.diff(bucket) >= 0)
    return tuple(int(np.argmax(bucket >= k)) for k in range(1, NUM_BUCKETS))


BUCKET_THR = _bucket_thresholds()


def _params(*sem):
    return pltpu.CompilerParams(dimension_semantics=sem or None, vmem_limit_bytes=VMEM_LIMIT)


def _nt(a, b):
    return lax.dot_general(a, b, (((1,), (1,)), ((), ())), preferred_element_type=F32)


def _mm(a, b):
    return jnp.dot(a, b, preferred_element_type=F32)


def _rms(x, g):
    return x * lax.rsqrt(jnp.mean(x * x, axis=-1, keepdims=True) + NORM_EPS) * g


def _bias_lookup(dist, rb_ref, h):
    b = jnp.full(dist.shape, rb_ref[0, h], F32)
    for k in range(1, NUM_BUCKETS):
        b = jnp.where(dist >= BUCKET_THR[k - 1], rb_ref[k, h], b)
    return b


def _ada_kernel(c_ref, w_ref, b_ref, o_ref):
    c = c_ref[...]
    o_ref[...] = jnp.dot(jax.nn.silu(c), w_ref[...], precision=HIGHEST,
                         preferred_element_type=F32) + b_ref[...]


def _ada(c_all, w, b):
    rows, d = c_all.shape
    cols = w.shape[1]
    tn = 512
    return pl.pallas_call(
        _ada_kernel,
        grid=(cols // tn,),
        in_specs=[pl.BlockSpec((rows, d), lambda j: (0, 0)),
                  pl.BlockSpec((d, tn), lambda j: (0, j)),
                  pl.BlockSpec((1, tn), lambda j: (0, j))],
        out_specs=pl.BlockSpec((rows, tn), lambda j: (0, j)),
        out_shape=jax.ShapeDtypeStruct((rows, cols), F32),
        compiler_params=_params("arbitrary"),
    )(c_all, w, b.reshape(1, cols))


def _bias_prompt_kernel(rb_ref, bt_ref, bc_ref, *, nd, t, nc):
    h = pl.program_id(0)
    row = lax.broadcasted_iota(jnp.int32, (LANE, LANE), 0)
    col = lax.broadcasted_iota(jnp.int32, (LANE, LANE), 1)
    for dl in range(nd):
        bt_ref[dl] = _bias_lookup(dl * LANE + row - col, rb_ref, h)
    rq = lax.broadcasted_iota(jnp.int32, (LANE, nc), 0)
    ends = lax.broadcasted_iota(jnp.int32, (LANE, nc), 1) * CMP_STRIDE + (CMP_LEN - 1)

    def chunk(i, carry):
        r0 = pl.multiple_of(i * LANE, LANE)
        bc_ref[pl.ds(r0, LANE), :] = _bias_lookup(r0 + rq - ends, rb_ref, h)
        return carry

    lax.fori_loop(0, t // LANE, chunk, 0)


def _bias_prompt(rel_bias, t, nd):
    nc = t // CMP_STRIDE
    return pl.pallas_call(
        functools.partial(_bias_prompt_kernel, nd=nd, t=t, nc=nc),
        grid=(N_HEADS,),
        in_specs=[pl.BlockSpec(memory_space=pltpu.SMEM)],
        out_specs=[pl.BlockSpec((None, nd, LANE, LANE), lambda h: (h, 0, 0, 0)),
                   pl.BlockSpec((None, t, nc), lambda h: (h, 0, 0))],
        out_shape=[jax.ShapeDtypeStruct((N_HEADS, nd, LANE, LANE), F32),
                   jax.ShapeDtypeStruct((N_HEADS, t, nc), F32)],
        compiler_params=_params("arbitrary"),
    )(rel_bias)


def _bias_sample_kernel(rb_ref, bsc_ref, bsw_ref, bss_ref, *, past, ncs, wb, n_pages):
    ends = lax.broadcasted_iota(jnp.int32, (1, ncs), 1) * CMP_STRIDE + (CMP_LEN - 1)
    jw = lax.broadcasted_iota(jnp.int32, (1, wb), 1)
    kpos = (lax.broadcasted_iota(jnp.int32, (n_pages, PAGE_SIZE), 0) * PAGE_SIZE
            + lax.broadcasted_iota(jnp.int32, (n_pages, PAGE_SIZE), 1))
    for h in range(N_HEADS):
        bsc_ref[h:h + 1, :] = _bias_lookup(past - ends, rb_ref, h)
        bsw_ref[h:h + 1, :] = _bias_lookup(wb - jw, rb_ref, h)
        bss_ref[h] = _bias_lookup(past - kpos, rb_ref, h)


def _bias_sample(rel_bias, past, wb):
    ncs = past // CMP_STRIDE
    n_pages = past // PAGE_SIZE
    return pl.pallas_call(
        functools.partial(_bias_sample_kernel, past=past, ncs=ncs, wb=wb, n_pages=n_pages),
        in_specs=[pl.BlockSpec(memory_space=pltpu.SMEM)],
        out_shape=[jax.ShapeDtypeStruct((N_HEADS, ncs), F32),
                   jax.ShapeDtypeStruct((N_HEADS, wb), F32),
                   jax.ShapeDtypeStruct((N_HEADS, n_pages, PAGE_SIZE), F32)],
        compiler_params=_params(),
        name="bias_sample",
    )(rel_bias)


IN_SPLITS = (D_RNN, D_RNN, D_ATT, KV_COLS, KV_COLS, KV_COLS, LANE)


def _inproj_kernel(x_ref, sh_ref, sc_ref, g_ref, w_ref, *out_refs):
    h = _rms(x_ref[...], g_ref[...]) * (1.0 + sc_ref[...]) + sh_ref[...]
    z = _mm(h.astype(BF16), w_ref[...])
    off = 0
    for o_ref, wdt in zip(out_refs, IN_SPLITS):
        o_ref[...] = z[:, off:off + wdt]
        off += wdt


def _inproj(x, mod, mod_spec, g1, w_in_p, tm):
    n = x.shape[0]
    cols = w_in_p.shape[1]
    return pl.pallas_call(
        _inproj_kernel,
        grid=(n // tm,),
        in_specs=[pl.BlockSpec((tm, D_MODEL), lambda i: (i, 0)),
                  mod_spec(0), mod_spec(1),
                  pl.BlockSpec((1, D_MODEL), lambda i: (0, 0)),
                  pl.BlockSpec((D_MODEL, cols), lambda i: (0, 0))],
        out_specs=[pl.BlockSpec((tm, wdt), lambda i: (i, 0)) for wdt in IN_SPLITS],
        out_shape=[jax.ShapeDtypeStruct((n, wdt), F32) for wdt in IN_SPLITS],
        compiler_params=_params("arbitrary"),
    )(x, mod, mod, g1, w_in_p)


def _lru_gates(xc, wg_ref, bg_ref, lam_ref):
    gates = _mm(xc.astype(BF16), wg_ref[...]) + bg_ref[...]
    r = jax.nn.sigmoid(gates[:, :D_RNN])
    i = jax.nn.sigmoid(gates[:, D_RNN:])
    log_a = -LRU_C * r * jax.nn.softplus(-lam_ref[...])
    a = jnp.exp(log_a)
    u = jnp.sqrt(-jnp.tanh(log_a) * (a * a + 1.0)) * i * xc
    return a, u


def _lru_prompt_kernel(xr_ref, yg_ref, cw_ref, cb_ref, wg_ref, bg_ref, lam_ref,
                       o_ref, hl_ref, cv_ref, xp_s, a_s, u_s, hs_s, h_s, *, nb, tc):
    t = pl.program_id(0)

    @pl.when(t == 0)
    def _():
        xp_s[:, 0:8, :] = jnp.zeros((nb, 8, D_RNN), F32)
        h_s[...] = jnp.zeros((nb, 1, D_RNN), F32)

    @pl.when(t > 0)
    def _():
        xp_s[:, 0:8, :] = xp_s[:, tc:tc + 8, :]

    xp_s[:, 8:, :] = xr_ref[...]
    xc = xp_s[:, 5:5 + tc, :] * cw_ref[0:1, :]
    for k in range(1, CONV_W):
        xc = xc + xp_s[:, 5 + k:5 + k + tc, :] * cw_ref[k:k + 1, :]
    xc = xc + cb_ref[...]
    a, u = _lru_gates(xc.reshape(nb * tc, D_RNN), wg_ref, bg_ref, lam_ref)
    a_s[...] = a.reshape(nb, tc, D_RNN)
    u_s[...] = u.reshape(nb, tc, D_RNN)

    def step(tt, h):
        h = a_s[:, pl.ds(tt, 1), :] * h + u_s[:, pl.ds(tt, 1), :]
        hs_s[:, pl.ds(tt, 1), :] = h
        return h

    h = lax.fori_loop(0, tc, step, h_s[...], unroll=8)
    h_s[...] = h
    o_ref[...] = hs_s[...] * jax.nn.gelu(yg_ref[...])
    hl_ref[...] = h
    cv_ref[...] = xp_s[:, tc + 8 - (CONV_W - 1):tc + 8, :]


def _lru_prompt(xr, yg, cw, cb, wg, bg, lam, tc):
    nb, t, _ = xr.shape
    blk = pl.BlockSpec((nb, tc, D_RNN), lambda i: (0, i, 0))
    full = lambda shape: pl.BlockSpec(shape, lambda i: tuple(0 for _ in shape))
    return pl.pallas_call(
        functools.partial(_lru_prompt_kernel, nb=nb, tc=tc),
        grid=(t // tc,),
        in_specs=[blk, blk, full(cw.shape), full(cb.shape), full(wg.shape), full(bg.shape),
                  full(lam.shape)],
        out_specs=[blk, full((nb, 1, D_RNN)), full((nb, CONV_W - 1, D_RNN))],
        out_shape=[jax.ShapeDtypeStruct((nb, t, D_RNN), F32),
                   jax.ShapeDtypeStruct((nb, 1, D_RNN), F32),
                   jax.ShapeDtypeStruct((nb, CONV_W - 1, D_RNN), F32)],
        scratch_shapes=[pltpu.VMEM((nb, tc + 8, D_RNN), F32), pltpu.VMEM((nb, tc, D_RNN), F32),
                        pltpu.VMEM((nb, tc, D_RNN), F32), pltpu.VMEM((nb, tc, D_RNN), F32),
                        pltpu.VMEM((nb, 1, D_RNN), F32)],
        compiler_params=_params("arbitrary"),
    )(xr, yg, cw, cb, wg, bg, lam)


def _lru_sample_kernel(x_ref, yg_ref, b0_ref, b1_ref, b2_ref, h0_ref, cw_ref, cb_ref, wg_ref,
                       bg_ref, lam_ref, o_ref, hn_ref):
    xc = b0_ref[...] * cw_ref[0:1, :]
    xc = xc + b1_ref[...] * cw_ref[1:2, :]
    xc = xc + b2_ref[...] * cw_ref[2:3, :]
    xc = xc + x_ref[...] * cw_ref[3:4, :]
    xc = xc + cb_ref[...]
    a, u = _lru_gates(xc, wg_ref, bg_ref, lam_ref)
    h = a * h0_ref[...] + u
    hn_ref[...] = h
    o_ref[...] = h * jax.nn.gelu(yg_ref[...])


def _lru_sample(x, yg, buf, h0, cw, cb, wg, bg, lam):
    s = x.shape[0]
    return pl.pallas_call(
        _lru_sample_kernel,
        out_shape=[jax.ShapeDtypeStruct((s, D_RNN), F32), jax.ShapeDtypeStruct((s, D_RNN), F32)],
        compiler_params=_params(),
    )(x, yg, buf[:, 0], buf[:, 1], buf[:, 2], h0, cw, cb, wg, bg, lam)


CHUNK_COLS = CMP_STRIDE * KV_COLS


def _poshid_kernel(pos_ref, w_ref, o_ref):
    parts = [jnp.dot(pos_ref[c], w_ref[c], precision=HIGHEST, preferred_element_type=F32)[0:1]
             for c in range(2)]
    o_ref[...] = jnp.concatenate(parts, axis=1)


def _poshid(cmp_pos, cmp_w1):
    k = CMP_LEN * HEAD_DIM
    pos = jnp.broadcast_to(cmp_pos.reshape(2, 1, k), (2, 8, k))
    return pl.pallas_call(
        _poshid_kernel,
        out_shape=jax.ShapeDtypeStruct((1, 2 * HEAD_DIM), F32),
        compiler_params=_params(),
    )(pos, cmp_w1.reshape(2, k, HEAD_DIM))


def _compress_rows(x_s, n, w1_ref, w2_ref, ph_ref):
    outs = []
    for g in range(N_KV):
        xg = jnp.concatenate(
            [x_s[g, pl.ds(s, n, stride=CMP_STRIDE), :].astype(BF16) for s in range(CMP_STRIDE)],
            axis=1)
        p = _mm(xg, w1_ref[...])
        hid = ph_ref[...] + p[:, :LANE] + pltpu.roll(p[:, LANE:], n - 1, axis=0)
        outs.append(_mm(jax.nn.gelu(hid).astype(BF16), w2_ref[...]))
    return jnp.concatenate(outs, axis=1)


def _compress_prompt_kernel(x_ref, w1_ref, w2_ref, ph_ref, o_ref, x_s, *, n):
    for g in range(N_KV):
        x_s[g] = x_ref[:, g * LANE:(g + 1) * LANE]
    o_ref[...] = _compress_rows(x_s, n, w1_ref, w2_ref, ph_ref)


def _compress_prompt(kvc, w1g, w2bd, ph):
    nb, t, _ = kvc.shape
    n = t // CMP_STRIDE
    return pl.pallas_call(
        functools.partial(_compress_prompt_kernel, n=n),
        grid=(nb,),
        in_specs=[pl.BlockSpec((None, t, KV_COLS), lambda b: (b, 0, 0)),
                  pl.BlockSpec(w1g.shape, lambda b: (0, 0)),
                  pl.BlockSpec(w2bd.shape, lambda b: (0, 0)),
                  pl.BlockSpec(ph.shape, lambda b: (0, 0))],
        out_specs=pl.BlockSpec((None, n, KV_COLS), lambda b: (b, 0, 0)),
        out_shape=jax.ShapeDtypeStruct((nb, n, KV_COLS), F32),
        scratch_shapes=[pltpu.VMEM((N_KV, t, LANE), F32)],
        compiler_params=_params("arbitrary"),
        name="compress_prompt",
    )(kvc, w1g, w2bd, ph)


MAX_PAGES_PER_STEP = 32


def _compress_sample_kernel(pt_ref, *refs, n, pps):
    page_refs = refs[:pps]
    w1_ref, w2_ref, ph_ref, o_ref, x_s = refs[pps:]
    p = pl.program_id(1)
    for k in range(pps):
        r0 = pl.multiple_of((p * pps + k) * PAGE_SIZE, PAGE_SIZE)
        for g in range(N_KV):
            x_s[g, pl.ds(r0, PAGE_SIZE), :] = page_refs[k][g].T

    @pl.when(p == pl.num_programs(1) - 1)
    def _():
        o_ref[...] = _compress_rows(x_s, n, w1_ref, w2_ref, ph_ref)


def _compress_sample(cache_t, page_table, w1g, w2bd, ph):
    s, n_pages = page_table.shape
    n = n_pages * PAGE_SIZE // CMP_STRIDE
    pps = math.gcd(n_pages, MAX_PAGES_PER_STEP)

    def page_spec(k):
        return pl.BlockSpec((None, N_KV, LANE, PAGE_SIZE),
                            lambda b, p, pt: (pt[b, p * pps + k], 0, 0, 0))

    const = lambda shape: pl.BlockSpec(shape, lambda b, p, pt: (0, 0))
    return pl.pallas_call(
        functools.partial(_compress_sample_kernel, n=n, pps=pps),
        grid_spec=pltpu.PrefetchScalarGridSpec(
            num_scalar_prefetch=1,
            grid=(s, n_pages // pps),
            in_specs=[page_spec(k) for k in range(pps)]
            + [const(w1g.shape), const(w2bd.shape), const(ph.shape)],
            out_specs=pl.BlockSpec((None, n, KV_COLS), lambda b, p, pt: (b, 0, 0)),
            scratch_shapes=[pltpu.VMEM((N_KV, n_pages * PAGE_SIZE, LANE), F32)]),
        out_shape=jax.ShapeDtypeStruct((s, n, KV_COLS), F32),
        compiler_params=_params("arbitrary", "arbitrary"),
        name="compress_sample",
    )(page_table, *([cache_t] * pps), w1g, w2bd, ph)


def _select_top(sc, top, want_idx):
    rows, nlane = sc.shape
    j = lax.broadcasted_iota(jnp.int32, (rows, nlane), 1).astype(F32)
    mask = jnp.zeros((rows, nlane), F32)
    picks = []
    for _ in range(top):
        m = jnp.max(sc, axis=-1, keepdims=True)
        idx = jnp.min(jnp.where(sc == m, j, float(nlane)), axis=-1, keepdims=True)
        pick = j == idx
        mask = jnp.where(pick, 1.0, mask)
        sc = jnp.where(pick, REMOVED, sc)
        picks.append(idx)
    if want_idx:
        return mask, jnp.concatenate(picks, axis=1).astype(jnp.int32)
    return mask


def _selection_scores(score, blk, qpos, nsel):
    cur = qpos // SEL_BLOCK
    forced = (blk == 0) | (blk == cur) | (blk == cur - 1)
    sc = jnp.where(forced, FORCE_SCORE, jnp.where(blk * SEL_BLOCK <= qpos, score, -1.0))
    return jnp.where(blk < nsel, sc, REMOVED)


def _softmax_rows(s, valid):
    s = jnp.where(valid, s, NEG_INF)
    p = jnp.exp(s - jnp.max(s, axis=-1, keepdims=True))
    return p / jnp.sum(p, axis=-1, keepdims=True)


def _cattn_prompt_kernel(q_ref, ck_ref, bc_ref, cov_ref, oc_ref, sel_ref, *, tq, nc, nsel, top):
    qpos = pl.program_id(1) * tq + lax.broadcasted_iota(jnp.int32, (tq, 1), 0)
    n_idx = lax.broadcasted_iota(jnp.int32, (1, nc), 1)
    valid = (qpos >= n_idx * CMP_STRIDE + (CMP_LEN - 1)) & (n_idx < nc - 1)
    anyv = jnp.max(valid.astype(F32), axis=-1, keepdims=True)
    blk = lax.broadcasted_iota(jnp.int32, (1, nsel), 1)
    outs, masks = [], []
    for g in range(N_KV):
        kv = ck_ref[:, g * LANE:(g + 1) * LANE].astype(BF16)
        k, v = kv[:, :HEAD_DIM], kv[:, HEAD_DIM:]
        psum = jnp.zeros((tq, nc), F32)
        for r in range(GROUP):
            h = g * GROUP + r
            qh = q_ref[:, h * HEAD_DIM:(h + 1) * HEAD_DIM].astype(BF16)
            p = _softmax_rows(_nt(qh, k) * SCALE + bc_ref[h], valid) * anyv
            outs.append(_mm(p.astype(BF16), v))
            psum = psum + p
        score = jnp.dot(psum, cov_ref[...], precision=HIGHEST, preferred_element_type=F32)
        masks.append(_select_top(_selection_scores(score, blk, qpos, nsel), top, False))
    oc_ref[...] = jnp.concatenate(outs, axis=1)
    sel_ref[...] = jnp.concatenate(masks, axis=1)


def _covers(n_cmp_rows, n_cmp, n_sel, n_sel_cols):
    start = np.arange(n_cmp_rows)[:, None] * CMP_STRIDE
    blk = np.arange(n_sel_cols)[None, :]
    cov = ((start < (blk + 1) * SEL_BLOCK) & (start + CMP_LEN > blk * SEL_BLOCK)
           & (np.arange(n_cmp_rows)[:, None] < n_cmp) & (blk < n_sel))
    return jnp.asarray(cov.astype(np.float32))


def _cattn_prompt(q, ck, bc, tq):
    nb, t, _ = q.shape
    nc = ck.shape[1]
    nsel = t // SEL_BLOCK
    top = min(SEL_TOP, nsel)
    cov = _covers(nc, nc - 1, nsel, nsel)
    return pl.pallas_call(
        functools.partial(_cattn_prompt_kernel, tq=tq, nc=nc, nsel=nsel, top=top),
        grid=(nb, t // tq),
        in_specs=[pl.BlockSpec((None, tq, D_ATT), lambda b, i: (b, i, 0)),
                  pl.BlockSpec((None, nc, KV_COLS), lambda b, i: (b, 0, 0)),
                  pl.BlockSpec((N_HEADS, tq, nc), lambda b, i: (0, i, 0)),
                  pl.BlockSpec(cov.shape, lambda b, i: (0, 0))],
        out_specs=[pl.BlockSpec((None, tq, D_ATT), lambda b, i: (b, i, 0)),
                   pl.BlockSpec((None, tq, N_KV * nsel), lambda b, i: (b, i, 0))],
        out_shape=[jax.ShapeDtypeStruct((nb, t, D_ATT), F32),
                   jax.ShapeDtypeStruct((nb, t, N_KV * nsel), F32)],
        compiler_params=_params("arbitrary", "arbitrary"),
    )(q, ck, bc, cov)


SAMPLES_PER_STEP = 8


def _cattn_sample_kernel(q_ref, ck_ref, bsc_ref, cov_ref, oc_ref, idx_ref, *, past, nc, nsel, top):
    n_idx = lax.broadcasted_iota(jnp.int32, (1, nc), 1)
    valid = (past >= n_idx * CMP_STRIDE + (CMP_LEN - 1)) & (n_idx < nc - 1)
    anyv = jnp.max(valid.astype(F32), axis=-1, keepdims=True)
    nlane = cov_ref.shape[1]
    blk = lax.broadcasted_iota(jnp.int32, (1, nlane), 1)
    idxs = []
    for g in range(N_KV):
        psums = []
        for i in range(SAMPLES_PER_STEP):
            kv = ck_ref[i, :, g * LANE:(g + 1) * LANE].astype(BF16)
            q4 = q_ref[i, g * GROUP:(g + 1) * GROUP, :].astype(BF16)
            s = _nt(q4, kv[:, :HEAD_DIM]) * SCALE + bsc_ref[g * GROUP:(g + 1) * GROUP, :]
            p = _softmax_rows(s, valid) * anyv
            oc_ref[i, g * GROUP:(g + 1) * GROUP, :] = _mm(p.astype(BF16), kv[:, HEAD_DIM:])
            psums.append(jnp.sum(p, axis=0, keepdims=True))
        score = jnp.dot(jnp.concatenate(psums, axis=0), cov_ref[...], precision=HIGHEST,
                        preferred_element_type=F32)
        _, idx = _select_top(_selection_scores(score, blk, past, nsel), top, True)
        idxs.append(idx)
    idx_ref[...] = jnp.concatenate(idxs, axis=1)


def _cattn_sample(q, ck, bsc, past):
    s = q.shape[0]
    nc = ck.shape[1]
    nsel = -(-(past + 1) // SEL_BLOCK)
    top = min(SEL_TOP, nsel)
    nlane = -(-nsel // LANE) * LANE
    cov = _covers(nc, nc - 1, nsel, nlane)
    g = SAMPLES_PER_STEP
    return pl.pallas_call(
        functools.partial(_cattn_sample_kernel, past=past, nc=nc, nsel=nsel, top=top),
        grid=(s // g,),
        in_specs=[pl.BlockSpec((g, N_HEADS, HEAD_DIM), lambda i: (i, 0, 0)),
                  pl.BlockSpec((g, nc, KV_COLS), lambda i: (i, 0, 0)),
                  pl.BlockSpec(bsc.shape, lambda i: (0, 0)),
                  pl.BlockSpec(cov.shape, lambda i: (0, 0))],
        out_specs=[pl.BlockSpec((g, N_HEADS, HEAD_DIM), lambda i: (i, 0, 0)),
                   pl.BlockSpec((g, N_KV * top), lambda i: (i, 0))],
        out_shape=[jax.ShapeDtypeStruct((s, N_HEADS, HEAD_DIM), F32),
                   jax.ShapeDtypeStruct((s, N_KV * top), jnp.int32)],
        compiler_params=_params("arbitrary"),
    )(q.reshape(s, N_HEADS, HEAD_DIM), ck, bsc, cov)


SEL_TILES = 4


def _flash_tiles(q4, kvs, biases, pens, m_ref, acc_ref):
    lane = lax.broadcasted_iota(jnp.int32, (LANE, LANE), 1)
    ss = [_nt(q4, kv) * SCALE + bias + jnp.concatenate([pen] * GROUP, axis=0)
          for kv, bias, pen in zip(kvs, biases, pens)]
    s_max = functools.reduce(jnp.maximum, ss)
    m_old = m_ref[...]
    m_new = jnp.maximum(m_old, jnp.max(s_max, axis=-1, keepdims=True))
    acc = jnp.exp(m_old - m_new) * acc_ref[...]
    for s, kv in zip(ss, kvs):
        acc = acc + _mm(jnp.exp(s - m_new).astype(BF16), jnp.where(lane < HEAD_DIM, 1.0, kv))
    acc_ref[...] = acc
    m_ref[...] = m_new


def _sattn_prompt_kernel(q_ref, sel_ref, kvs_ref, kvw_ref, bt_ref, os_ref, ow_ref,
                         m_s, acc_s, *, nsel, nd, npw):
    c = pl.program_id(1)
    rows = GROUP * LANE
    qrow = lax.broadcasted_iota(jnp.int32, (LANE, LANE), 0)
    kcol = lax.broadcasted_iota(jnp.int32, (LANE, LANE), 1)
    ej = lax.broadcasted_iota(jnp.int32, (nsel, LANE), 0)
    el = lax.broadcasted_iota(jnp.int32, (nsel, LANE), 1) // SEL_BLOCK
    zpad = jnp.zeros((LANE, LANE - HEAD_DIM), BF16)
    q4s = [jnp.concatenate(
        [jnp.concatenate([q_ref[:, (g * GROUP + r) * HEAD_DIM:(g * GROUP + r + 1) * HEAD_DIM]
                          .astype(BF16), zpad], axis=1) for r in range(GROUP)], axis=0)
        for g in range(N_KV)]
    selgs = [sel_ref[:, g * nsel:(g + 1) * nsel].astype(BF16) for g in range(N_KV)]

    def reset():
        m_s[...] = jnp.full(m_s.shape, NEG_INF, F32)
        acc_s[...] = jnp.zeros(acc_s.shape, F32)

    def result():
        outs = []
        for g in range(N_KV):
            acc = acc_s[g]
            o = acc[:, HEAD_DIM:] / acc[:, 0:1]
            outs += [o[r * LANE:(r + 1) * LANE] for r in range(GROUP)]
        return jnp.concatenate(outs, axis=1)

    def bias_tiles(g, kb):
        dl = jnp.clip(c - kb, 0, nd - 1)
        return bt_ref[g * GROUP:(g + 1) * GROUP, dl].reshape(rows, LANE)

    def kv_tile(ref, g, kb):
        k0 = pl.multiple_of(jnp.maximum(kb, 0) * LANE, LANE)
        return ref[pl.ds(k0, LANE), g * LANE:(g + 1) * LANE].astype(BF16)

    def update(ref, g, kbs, pens):
        _flash_tiles(q4s[g], [kv_tile(ref, g, kb) for kb in kbs],
                     [bias_tiles(g, kb) for kb in kbs], pens, m_s.at[g], acc_s.at[g])

    def sel_pair(kp, carry):
        kbs = tuple(SEL_TILES * kp + k for k in range(SEL_TILES))
        for g in range(N_KV):
            pens = []
            for kb in kbs:
                expand = (ej == (LANE // SEL_BLOCK) * kb + el).astype(BF16)
                chosen = _mm(selgs[g], expand) > 0.5
                dist = (c - kb) * LANE + qrow - kcol
                pens.append(jnp.where(chosen & (dist >= 0), 0.0, NEG_INF))
            update(kvs_ref, g, kbs, pens)
        return carry

    reset()
    lax.fori_loop(0, (c + SEL_TILES) // SEL_TILES, sel_pair, 0)
    os_ref[...] = result()

    reset()
    for kp in range(npw):
        kbs = (c - 2 * (npw - kp) + 1, c - 2 * (npw - kp) + 2)
        pens = []
        for kb in kbs:
            dist = jnp.where(kb >= 0, (c - kb) * LANE + qrow - kcol, -1)
            pens.append(jnp.where((dist >= 0) & (dist < WINDOW), 0.0, NEG_INF))
        for g in range(N_KV):
            update(kvw_ref, g, kbs, pens)
    ow_ref[...] = result()


def _sattn_prompt(q, sel, kvs, kvw, bt):
    nb, t, _ = q.shape
    nsel = t // SEL_BLOCK
    nd = bt.shape[1]
    npw = (WINDOW // LANE + 2) // 2
    qblk = pl.BlockSpec((None, LANE, D_ATT), lambda b, i: (b, i, 0))
    kvblk = pl.BlockSpec((None, t, KV_COLS), lambda b, i: (b, 0, 0))
    return pl.pallas_call(
        functools.partial(_sattn_prompt_kernel, nsel=nsel, nd=nd, npw=npw),
        grid=(nb, t // LANE),
        in_specs=[qblk, pl.BlockSpec((None, LANE, N_KV * nsel), lambda b, i: (b, i, 0)),
                  kvblk, kvblk, pl.BlockSpec(bt.shape, lambda b, i: (0, 0, 0, 0))],
        out_specs=[qblk, qblk],
        out_shape=[jax.ShapeDtypeStruct((nb, t, D_ATT), F32)] * 2,
        scratch_shapes=[pltpu.VMEM((N_KV, GROUP * LANE, LANE), F32),
                        pltpu.VMEM((N_KV, GROUP * LANE, LANE), F32)],
        compiler_params=_params("arbitrary", "arbitrary"),
        name="sattn_prompt",
    )(q, sel, kvs, kvw, bt)


def _sattn_sample_kernel(idx_ref, pt_ref, *refs, top, npb, n_pages, wb):
    nblk = N_KV * top
    blk_refs = refs[:nblk]
    (q_ref, ksn_ref, kwn_ref, win_ref, bss_ref, bsw_ref, b0_ref, os_ref, ow_ref) = refs[nblk:]
    b = pl.program_id(0)
    halves = PAGE_SIZE // SEL_BLOCK
    half = lax.broadcasted_iota(jnp.int32, (1, PAGE_SIZE), 1) // SEL_BLOCK
    jw = lax.broadcasted_iota(jnp.int32, (1, wb), 1)
    for g in range(N_KV):
        hs = slice(g * GROUP, (g + 1) * GROUP)
        q4f = q_ref[hs, :]
        q4 = q4f.astype(BF16)
        b0 = b0_ref[hs, :]

        def new_logit(new_ref):
            kn = new_ref[:, g * LANE:g * LANE + HEAD_DIM]
            return jnp.sum(q4f * kn, axis=-1, keepdims=True) * SCALE + b0

        def new_value(new_ref):
            return new_ref[:, g * LANE + HEAD_DIM:(g + 1) * LANE]

        logits, vts = [], []
        for j in range(top):
            ib = idx_ref[b, g * top + j]
            page = jnp.minimum(ib // halves, n_pages - 1)
            s = _mm(q4, blk_refs[g * top + j][0].astype(BF16)) * SCALE + bss_ref[page, hs, :]
            logits.append(jnp.where(half == jnp.where(ib < npb, ib % halves, -1), s, NEG_INF))
            vts.append(blk_refs[g * top + j][1].astype(BF16))
        s_new = new_logit(ksn_ref)
        m = s_new
        for s in logits:
            m = jnp.maximum(m, jnp.max(s, axis=-1, keepdims=True))
        p_new = jnp.exp(s_new - m)
        l = p_new
        acc = p_new * new_value(ksn_ref)
        for s, vt in zip(logits, vts):
            p = jnp.exp(s - m)
            l = l + jnp.sum(p, axis=-1, keepdims=True)
            acc = acc + _nt(p.astype(BF16), vt)
        os_ref[hs, :] = acc / l

        s = _mm(q4, win_ref[g, 0].astype(BF16)) * SCALE + bsw_ref[hs, :]
        s = jnp.where(wb - jw < WINDOW, s, NEG_INF)
        s_new = new_logit(kwn_ref)
        m = jnp.maximum(s_new, jnp.max(s, axis=-1, keepdims=True))
        p = jnp.exp(s - m)
        p_new = jnp.exp(s_new - m)
        l = p_new + jnp.sum(p, axis=-1, keepdims=True)
        acc = p_new * new_value(kwn_ref) + _nt(p.astype(BF16), win_ref[g, 1].astype(BF16))
        ow_ref[hs, :] = acc / l


def _sattn_sample(q, idx, page_table, sel_t, ks_new, kw_new, win_t, bss, bsw, b0, top, npb):
    s, n_pages = page_table.shape
    wb = win_t.shape[-1]
    halves = PAGE_SIZE // SEL_BLOCK

    def blk_spec(g, j):
        def imap(b, idx_r, pt_r):
            ib = idx_r[b, g * top + j]
            return (pt_r[b, jnp.minimum(ib // halves, n_pages - 1)], g, 0, 0, 0)
        return pl.BlockSpec((None, None, 2, HEAD_DIM, PAGE_SIZE), imap)

    per_b = lambda shape: pl.BlockSpec((None,) + shape,
                                       lambda b, i_r, p_r: (b,) + tuple(0 for _ in shape))
    const = lambda shape: pl.BlockSpec(shape, lambda b, i_r, p_r: tuple(0 for _ in shape))
    out = pl.pallas_call(
        functools.partial(_sattn_sample_kernel, top=top, npb=npb, n_pages=n_pages, wb=wb),
        grid_spec=pltpu.PrefetchScalarGridSpec(
            num_scalar_prefetch=2,
            grid=(s,),
            in_specs=[blk_spec(g, j) for g in range(N_KV) for j in range(top)]
            + [per_b((N_HEADS, HEAD_DIM)), per_b((1, KV_COLS)), per_b((1, KV_COLS)),
               per_b(win_t.shape[1:]), const(bss.shape), const(bsw.shape), const(b0.shape)],
            out_specs=[per_b((N_HEADS, HEAD_DIM)), per_b((N_HEADS, HEAD_DIM))]),
        out_shape=[jax.ShapeDtypeStruct((s, N_HEADS, HEAD_DIM), F32)] * 2,
        compiler_params=_params("arbitrary"),
        name="sattn_sample",
    )(idx, page_table, *([sel_t] * (N_KV * top)), q.reshape(s, N_HEADS, HEAD_DIM),
      ks_new.reshape(s, 1, KV_COLS), kw_new.reshape(s, 1, KV_COLS), win_t, bss, bsw, b0)
    return out[0].reshape(s, D_ATT), out[1].reshape(s, D_ATT)


def _outproj_kernel(x_ref, ol_ref, oc_ref, os_ref, ow_ref, gl_ref, gt1_ref, sh2_ref, sc2_ref,
                    gexp_ref, nl_ref, na_ref, n2_ref, wo_ref, x1_ref, h2_ref):
    gates = jax.nn.sigmoid(jnp.dot(gl_ref[...], gexp_ref[...], precision=HIGHEST,
                                   preferred_element_type=F32))
    o_att = (gates[:, :D_ATT] * oc_ref[...] + gates[:, D_ATT:2 * D_ATT] * os_ref[...]
             + gates[:, 2 * D_ATT:] * ow_ref[...])
    n_lru = _rms(ol_ref[...], nl_ref[...]).astype(BF16)
    n_att = _rms(o_att, na_ref[...]).astype(BF16)
    mixed = _mm(n_lru, wo_ref[:D_RNN, :]) + _mm(n_att, wo_ref[D_RNN:, :])
    x1 = x_ref[...] + gt1_ref[...] * mixed
    x1_ref[...] = x1
    h2_ref[...] = (_rms(x1, n2_ref[...]) * (1.0 + sc2_ref[...]) + sh2_ref[...]).astype(BF16)


def _gate_expand():
    e = np.zeros((LANE, 3 * D_ATT), np.float32)
    for br in range(3):
        for h in range(N_HEADS):
            e[br * N_HEADS + h, br * D_ATT + h * HEAD_DIM:br * D_ATT + (h + 1) * HEAD_DIM] = 1.0
    return jnp.asarray(e)


def _outproj(x, o_lru, o_c, o_s, o_w, gl, mod, mod_spec, nl, na, n2, wo, tm):
    n = x.shape[0]
    tok = lambda w: pl.BlockSpec((tm, w), lambda i: (i, 0))
    const = lambda shape: pl.BlockSpec(shape, lambda i: (0, 0))
    gexp = _gate_expand()
    return pl.pallas_call(
        _outproj_kernel,
        grid=(n // tm,),
        in_specs=[tok(D_MODEL), tok(D_RNN), tok(D_ATT), tok(D_ATT), tok(D_ATT), tok(LANE),
                  mod_spec(2), mod_spec(3), mod_spec(4),
                  const(gexp.shape), const(nl.shape), const(na.shape), const(n2.shape),
                  const(wo.shape)],
        out_specs=[tok(D_MODEL), tok(D_MODEL)],
        out_shape=[jax.ShapeDtypeStruct((n, D_MODEL), F32),
                   jax.ShapeDtypeStruct((n, D_MODEL), BF16)],
        compiler_params=_params("arbitrary"),
    )(x, o_lru, o_c, o_s, o_w, gl, mod, mod, mod, gexp, nl, na, n2, wo)


def _extract_top(x, top, row):
    nrow = x.shape[0]
    vals = []
    rank = jnp.full(x.shape, float(top), F32)
    for k in range(top):
        m = jnp.max(x, axis=0, keepdims=True)
        idx = jnp.min(jnp.where(x == m, row, float(nrow)), axis=0, keepdims=True)
        pick = row == idx
        rank = jnp.where(pick, float(k), rank)
        x = jnp.where(pick, REMOVED, x)
        vals.append(m)
    return jnp.concatenate(vals, axis=0), rank


PAIR_COUNTS = tuple(PEER_TOPK // (a + 1) for a in range(PEER_TOPK))
N_PAIRS = sum(PAIR_COUNTS)
PAIR_ROWS = -(-N_PAIRS // 8) * 8


def _peer_score_kernel(h2_ref, wq_ref, bq_ref, sk_ref, bc_ref, a0_ref, r1_ref, e1_ref,
                       q_s, cand_s, *, tn):
    q_s[...] = (_nt(wq_ref[...], h2_ref[...]) + bq_ref[...]).astype(BF16)
    row = lax.broadcasted_iota(jnp.int32, (PEER_KEYS, tn), 0).astype(F32)
    crow = lax.broadcasted_iota(jnp.int32, (PAIR_ROWS, tn), 0).astype(F32)

    def head(h, carry):
        r0 = pl.multiple_of(h * 2 * PEER_KEYS, 2 * PEER_KEYS)
        s0 = _mm(sk_ref[2 * h], q_s[pl.ds(r0, PEER_KEYS), :])
        s1 = _mm(sk_ref[2 * h + 1], q_s[pl.ds(r0 + PEER_KEYS, PEER_KEYS), :])
        v0, rank0 = _extract_top(s0, PEER_TOPK, row)
        v1, rank1 = _extract_top(s1, PEER_TOPK, row)
        off = 0
        for a, cnt in enumerate(PAIR_COUNTS):
            cand_s[off:off + cnt, :] = v1[0:cnt] + v0[a:a + 1]
            off += cnt
        cand_s[N_PAIRS:, :] = jnp.full((PAIR_ROWS - N_PAIRS, tn), REMOVED, F32)
        best, rank_c = _extract_top(cand_s[...], PEER_TOPK, crow)
        z = jnp.sum(jnp.exp(best - best[0:1]), axis=0, keepdims=True)
        cand_s[...] = jnp.where(rank_c < PEER_TOPK, 1.0, 0.0)
        bc = jnp.zeros((PEER_KEYS, tn), F32)
        off = 0
        for a, cnt in enumerate(PAIR_COUNTS):
            kept_a = jnp.sum(cand_s[off:off + cnt, :], axis=0, keepdims=True)
            bc = jnp.where(rank0 == float(a), kept_a, bc)
            off += cnt
        bc_ref[h] = bc
        a0_ref[h] = jnp.where(rank0 < PEER_TOPK, jnp.exp(s0 - v0[0:1]), 0.0) * (0.5 / z)
        r1_ref[h] = rank1.astype(BF16)
        e1_ref[h] = jnp.exp(s1 - v1[0:1]).astype(BF16)
        return carry

    lax.fori_loop(0, PEER_HEADS, head, 0)


def _peer_score(h2, wq_t, bq_t, subkeys, tn):
    n = h2.shape[0]
    dq = wq_t.shape[0]
    tok = pl.BlockSpec((PEER_HEADS, PEER_KEYS, tn), lambda i: (0, 0, i))
    shp = jax.ShapeDtypeStruct((PEER_HEADS, PEER_KEYS, n), F32)
    shp16 = jax.ShapeDtypeStruct((PEER_HEADS, PEER_KEYS, n), BF16)
    return pl.pallas_call(
        functools.partial(_peer_score_kernel, tn=tn),
        grid=(n // tn,),
        in_specs=[pl.BlockSpec((tn, D_MODEL), lambda i: (i, 0)),
                  pl.BlockSpec(wq_t.shape, lambda i: (0, 0)),
                  pl.BlockSpec(bq_t.shape, lambda i: (0, 0)),
                  pl.BlockSpec(subkeys.shape, lambda i: (0, 0, 0))],
        out_specs=[tok, tok, tok, tok],
        out_shape=[shp, shp, shp16, shp16],
        scratch_shapes=[pltpu.VMEM((dq, tn), BF16), pltpu.VMEM((PAIR_ROWS, tn), F32)],
        compiler_params=_params("arbitrary"),
        name="peer_score",
    )(h2, wq_t, bq_t, subkeys)


GELU_C1 = math.sqrt(2.0 / math.pi)
GELU_C2 = 0.044715 * GELU_C1


def _peer_dense_kernel(h2_ref, u_ref, vt_ref, bc_ref, a0_ref, r1_ref, e1_ref, x1_ref,
                       gt2_ref, fg_ref, y_ref, acc_s, act_s, *, tn, eb, lc):
    e = pl.program_id(1)

    @pl.when(e == 0)
    def _():
        acc_s[...] = jnp.zeros(acc_s.shape, F32)

    z = _nt(u_ref[...], h2_ref[...])
    for il in range(eb // PEER_KEYS):
        i = e * (eb // PEER_KEYS) + il
        rs = slice(il * PEER_KEYS, (il + 1) * PEER_KEYS)
        for c0 in range(0, tn, lc):
            ls = slice(c0, c0 + lc)
            w = jnp.zeros((PEER_KEYS, lc), BF16)
            for h in range(PEER_HEADS):
                keep = r1_ref[h, :, ls] < bc_ref[h, pl.ds(i, 1), ls].astype(BF16)
                w = w + (jnp.where(keep, e1_ref[h, :, ls], 0.0)
                         * a0_ref[h, pl.ds(i, 1), ls].astype(BF16))
            zi = z[rs, ls]
            t = jnp.tanh(zi * (GELU_C1 + GELU_C2 * (zi * zi)))
            act_s[rs, ls] = ((zi + zi * t) * w.astype(F32)).astype(BF16)
    acc_s[...] += _mm(vt_ref[...], act_s[...])

    @pl.when(e == pl.num_programs(1) - 1)
    def _():
        x = x1_ref[...] + gt2_ref[...] * acc_s[...].T
        y_ref[...] = _rms(x, fg_ref[...])


def _peer_dense(h2, u, vt, scores, x1, mod, mod_spec2, fg, tn, eb):
    n = h2.shape[0]
    n_exp = u.shape[0]
    bc, a0, r1, e1 = scores
    tok3 = pl.BlockSpec((PEER_HEADS, PEER_KEYS, tn), lambda i, e: (0, 0, i))
    return pl.pallas_call(
        functools.partial(_peer_dense_kernel, tn=tn, eb=eb, lc=min(tn, 2 * LANE)),
        grid=(n // tn, n_exp // eb),
        in_specs=[pl.BlockSpec((tn, D_MODEL), lambda i, e: (i, 0)),
                  pl.BlockSpec((eb, D_MODEL), lambda i, e: (e, 0)),
                  pl.BlockSpec((D_MODEL, eb), lambda i, e: (0, e)),
                  tok3, tok3, tok3, tok3,
                  pl.BlockSpec((tn, D_MODEL), lambda i, e: (i, 0)),
                  mod_spec2(5),
                  pl.BlockSpec((1, D_MODEL), lambda i, e: (0, 0))],
        out_specs=pl.BlockSpec((tn, D_MODEL), lambda i, e: (i, 0)),
        out_shape=jax.ShapeDtypeStruct((n, D_MODEL), F32),
        scratch_shapes=[pltpu.VMEM((D_MODEL, tn), F32), pltpu.VMEM((eb, tn), BF16)],
        compiler_params=_params("arbitrary", "arbitrary"),
        name="peer_dense",
    )(h2, u, vt, bc, a0, r1, e1, x1, mod, fg)


def _block_diag(w):
    nblk, bw, _ = w.shape
    eye = jnp.eye(nblk, dtype=w.dtype)
    return jnp.einsum('nde,nm->ndme', w, eye).reshape(nblk * bw, nblk * bw)


def kernel(x_prompt, x_sample, cache_cmp_kv, cache_sel_kv, cache_win_kv, state_lru_h, state_conv, page_table, c_prompt, c_sample, ada_w, ada_b, norm1_g, norm2_g, w_in, conv_w, conv_b, lru_wa, lru_ba, lru_wi, lru_bi, lru_lambda, cmp_w1, cmp_w2, cmp_pos, out_norm_lru, out_norm_att, w_out, peer_wq, peer_bq, peer_subkeys, peer_u, peer_v, rel_bias, final_g):
    nb, t, d = x_prompt.shape
    ns = x_sample.shape[0]
    n_pages = page_table.shape[1]
    past = n_pages * PAGE_SIZE
    wb = cache_win_kv.shape[2]
    assert x_sample.shape[1] == 1 and ada_w.shape[0] == 1 and d == D_MODEL
    assert t % (SEL_TILES * LANE) == 0 and ns % LANE == 0 and past % LANE == 0
    assert wb == min(WINDOW, past)
    row = lambda v: v.reshape(1, -1)

    w_in_p = jnp.pad(w_in[0], ((0, 0), (0, sum(IN_SPLITS) - w_in.shape[2]))).astype(BF16)
    wg = jnp.concatenate([_block_diag(lru_wa[0]), _block_diag(lru_wi[0])], axis=1).astype(BF16)
    bg = row(jnp.concatenate([lru_ba[0], lru_bi[0]]))
    w1r = cmp_w1[0].reshape(2, 2, CMP_STRIDE, HEAD_DIM, HEAD_DIM)
    eye2 = jnp.eye(2, dtype=F32)
    w1g = jnp.einsum('cmsdh,ce->scdmeh', w1r, eye2).reshape(CMP_STRIDE * LANE, 2 * LANE).astype(BF16)
    w2bd = jnp.einsum('che,cf->chfe', cmp_w2[0], eye2).reshape(LANE, LANE).astype(BF16)
    wq_t = peer_wq[0].T.astype(BF16)
    bq_t = peer_bq[0].reshape(-1, 1)
    subk = peer_subkeys[0].reshape(2 * PEER_HEADS, PEER_KEYS, -1).astype(BF16)
    u_b = peer_u[0].astype(BF16)
    vt_b = peer_v[0].T.astype(BF16)
    wo_b = w_out[0].astype(BF16)

    n_c = nb + ns
    c_all = jnp.pad(jnp.concatenate([c_prompt, c_sample]), ((0, -n_c % 8), (0, 0)))
    mod = _ada(c_all, ada_w[0], ada_b[0])
    mod_p = mod[:nb].reshape(nb, 1, 6 * D_MODEL)
    mod_s = mod[nb:n_c]

    tm_p, tm_s = 512, ns
    tn_p, tn_s = 512, ns
    tn_sc = 256 if ns % 256 == 0 else LANE

    def mod_spec_p(tm):
        per = t // tm
        return lambda j: pl.BlockSpec((None, 1, D_MODEL), lambda i, *_: (i // per, 0, j))

    def mod_spec_s(tm):
        return lambda j: pl.BlockSpec((tm, D_MODEL), lambda i, *_: (i, j))

    thr_last = BUCKET_THR[-1]
    nd = min(t // LANE, -(-(thr_last + LANE - 1) // LANE) + 1)
    bt, bc = _bias_prompt(rel_bias, t, nd)
    bsc, bsw, bss = _bias_sample(rel_bias, past, wb)
    bss = jnp.transpose(bss, (1, 0, 2))
    b0 = rel_bias[0].reshape(N_HEADS, 1)

    ph = _poshid(cmp_pos[0], cmp_w1[0])
    lru_args = (conv_w[0], row(conv_b[0]), wg, bg, row(lru_lambda[0]))
    norms = (row(out_norm_lru[0]), row(out_norm_att[0]), row(norm2_g[0]), wo_b)

    xp = x_prompt.reshape(nb * t, d)
    xr, yg, q, kvc, kvs, kvw, gl = _inproj(xp, mod_p, mod_spec_p(tm_p), row(norm1_g[0]), w_in_p, tm_p)
    seq = lambda a: a.reshape(nb, t, a.shape[-1])
    o_lru, h_p, conv_p = _lru_prompt(seq(xr), seq(yg), *lru_args, tc=256)
    ck = _compress_prompt(seq(kvc), w1g, w2bd, ph)
    o_c, sel = _cattn_prompt(seq(q), ck, bc, LANE)
    o_s, o_w = _sattn_prompt(seq(q), sel, seq(kvs), seq(kvw), bt)
    flat = lambda a: a.reshape(nb * t, a.shape[-1])
    x1, h2 = _outproj(xp, flat(o_lru), flat(o_c), flat(o_s), flat(o_w), gl, mod_p,
                      mod_spec_p(tm_p), *norms, tm_p)
    scores = _peer_score(h2, wq_t, bq_t, subk, 256)
    y_p = _peer_dense(h2, u_b, vt_b, scores, x1, mod_p, mod_spec_p(tn_p), row(final_g), tn_p, 1024)

    xs = x_sample.reshape(ns, d)
    xr_s, yg_s, q_s, kvc_s, kvs_s, kvw_s, gl_s = _inproj(xs, mod_s, mod_spec_s(tm_s),
                                                          row(norm1_g[0]), w_in_p, tm_s)
    o_lru_s, h_s = _lru_sample(xr_s, yg_s, state_conv[0], state_lru_h[0], *lru_args)
    rows_minor = lambda a: jnp.transpose(a, (0, 2, 3, 4, 1))
    cmp_t = rows_minor(cache_cmp_kv[0]).reshape(-1, N_KV, LANE, PAGE_SIZE)
    ck_s = _compress_sample(cmp_t, page_table, w1g, w2bd, ph)
    o_c_s, idx = _cattn_sample(q_s, ck_s, bsc, past)
    top_s = idx.shape[1] // N_KV
    o_s_s, o_w_s = _sattn_sample(q_s, idx, page_table, rows_minor(cache_sel_kv[0]), kvs_s, kvw_s,
                                 rows_minor(cache_win_kv[0]), bss, bsw, b0, top_s,
                                 past // SEL_BLOCK)
    x1_s, h2_s = _outproj(xs, o_lru_s, o_c_s.reshape(ns, D_ATT), o_s_s, o_w_s, gl_s, mod_s,
                          mod_spec_s(tm_s), *norms, tm_s)
    scores_s = _peer_score(h2_s, wq_t, bq_t, subk, tn_sc if ns % tn_sc == 0 else ns)
    y_s = _peer_dense(h2_s, u_b, vt_b, scores_s, x1_s, mod_s, mod_spec_s(tn_s), row(final_g), tn_s, 1024)

    kv6 = lambda a, n, tt: a.reshape(1, n, tt, N_KV, 2, HEAD_DIM)
    win_p = kv6(kvw, nb, t)[:, :, t - min(WINDOW, t):]
    win_s = jnp.concatenate([cache_win_kv[:, :, 1:], kv6(kvw_s, ns, 1)], axis=2)[:, :, -wb:]
    conv_s = jnp.concatenate([state_conv[:, :, 1:], xr_s.reshape(1, ns, 1, D_RNN)], axis=2)
    return (y_p.reshape(nb, t, d), y_s.reshape(ns, 1, d),
            kv6(kvc, nb, t), kv6(kvc_s, ns, 1),
            kv6(kvs, nb, t), kv6(kvs_s, ns, 1),
            win_p, win_s,
            h_p.reshape(1, nb, D_RNN), h_s.reshape(1, ns, D_RNN),
            conv_p.reshape(1, nb, CONV_W - 1, D_RNN), conv_s)
```

```python
import functools
import math

import numpy as np
import jax
import jax.numpy as jnp
from jax import lax
from jax.experimental import pallas as pl
from jax.experimental.pallas import tpu as pltpu

F32 = jnp.float32
BF16 = jnp.bfloat16
HIGHEST = lax.Precision.HIGHEST

D_MODEL = 1024
D_RNN = 512
LRU_BLOCKS = 8
CONV_W = 4
LRU_C = 8.0
N_HEADS = 8
HEAD_DIM = 64
N_KV = 2
GROUP = 4
D_ATT = 512
KV_COLS = 2 * N_KV * HEAD_DIM
CMP_LEN = 32
CMP_STRIDE = 16
SEL_BLOCK = 64
SEL_TOP = 16
WINDOW = 512
SCALE = HEAD_DIM ** -0.5
NEG_INF = -1e30
REMOVED = -3e38
FORCE_SCORE = 1e6
NUM_BUCKETS = 32
MAX_DISTANCE = 1024
PEER_HEADS = 8
PEER_KEYS = 128
PEER_TOPK = 16
NORM_EPS = 1e-6
PAGE_SIZE = 128

LANE = 128
VMEM_LIMIT = 56 * 1024 * 1024


def _bucket_thresholds():
    d = np.arange(0, 1 << 15, dtype=np.int64)
    max_exact = NUM_BUCKETS // 2
    df = np.maximum(d, 1).astype(np.float32)
    ratio = (np.log(df / np.float32(max_exact)) / np.float32(math.log(MAX_DISTANCE / max_exact))
             * np.float32(NUM_BUCKETS - max_exact))
    large = np.minimum(max_exact + ratio.astype(np.int32), NUM_BUCKETS - 1)
    bucket = np.where(d < max_exact, d, large)
    assert np.all(np.diff(bucket) >= 0)
    return tuple(int(np.argmax(bucket >= k)) for k in range(1, NUM_BUCKETS))


BUCKET_THR = _bucket_thresholds()


def _params(*sem):
    return pltpu.CompilerParams(dimension_semantics=sem or None, vmem_limit_bytes=VMEM_LIMIT)


def _nt(a, b):
    return lax.dot_general(a, b, (((1,), (1,)), ((), ())), preferred_element_type=F32)


def _mm(a, b):
    return jnp.dot(a, b, preferred_element_type=F32)


def _rms(x, g):
    return x * lax.rsqrt(jnp.mean(x * x, axis=-1, keepdims=True) + NORM_EPS) * g


def _bias_lookup(dist, rb_ref, h):
    b = jnp.full(dist.shape, rb_ref[0, h], F32)
    for k in range(1, NUM_BUCKETS):
        b = jnp.where(dist >= BUCKET_THR[k - 1], rb_ref[k, h], b)
    return b


def _ada_kernel(c_ref, w_ref, b_ref, o_ref):
    c = c_ref[...]
    o_ref[...] = jnp.dot(jax.nn.silu(c), w_ref[...], precision=HIGHEST,
                         preferred_element_type=F32) + b_ref[...]


def _ada(c_all, w, b):
    rows, d = c_all.shape
    cols = w.shape[1]
    tn = 512
    return pl.pallas_call(
        _ada_kernel,
        grid=(cols // tn,),
        in_specs=[pl.BlockSpec((rows, d), lambda j: (0, 0)),
                  pl.BlockSpec((d, tn), lambda j: (0, j)),
                  pl.BlockSpec((1, tn), lambda j: (0, j))],
        out_specs=pl.BlockSpec((rows, tn), lambda j: (0, j)),
        out_shape=jax.ShapeDtypeStruct((rows, cols), F32),
        compiler_params=_params("arbitrary"),
    )(c_all, w, b.reshape(1, cols))


def _bias_prompt_kernel(rb_ref, bt_ref, bc_ref, *, nd, t, nc):
    h = pl.program_id(0)
    row = lax.broadcasted_iota(jnp.int32, (LANE, LANE), 0)
    col = lax.broadcasted_iota(jnp.int32, (LANE, LANE), 1)
    for dl in range(nd):
        bt_ref[dl] = _bias_lookup(dl * LANE + col - row, rb_ref, h)
    rq = lax.broadcasted_iota(jnp.int32, (LANE, nc), 0)
    ends = lax.broadcasted_iota(jnp.int32, (LANE, nc), 1) * CMP_STRIDE + (CMP_LEN - 1)

    def chunk(i, carry):
        r0 = pl.multiple_of(i * LANE, LANE)
        bc_ref[pl.ds(r0, LANE), :] = _bias_lookup(r0 + rq - ends, rb_ref, h)
        return carry

    lax.fori_loop(0, t // LANE, chunk, 0)


def _bias_prompt(rel_bias, t, nd):
    nc = t // CMP_STRIDE
    return pl.pallas_call(
        functools.partial(_bias_prompt_kernel, nd=nd, t=t, nc=nc),
        grid=(N_HEADS,),
        in_specs=[pl.BlockSpec(memory_space=pltpu.SMEM)],
        out_specs=[pl.BlockSpec((None, nd, LANE, LANE), lambda h: (h, 0, 0, 0)),
                   pl.BlockSpec((None, t, nc), lambda h: (h, 0, 0))],
        out_shape=[jax.ShapeDtypeStruct((N_HEADS, nd, LANE, LANE), F32),
                   jax.ShapeDtypeStruct((N_HEADS, t, nc), F32)],
        compiler_params=_params("arbitrary"),
    )(rel_bias)


def _bias_sample_kernel(rb_ref, bsc_ref, bsw_ref, bss_ref, *, past, ncs, wb, n_pages):
    ends = lax.broadcasted_iota(jnp.int32, (1, ncs), 1) * CMP_STRIDE + (CMP_LEN - 1)
    jw = lax.broadcasted_iota(jnp.int32, (1, wb), 1)
    kpos = (lax.broadcasted_iota(jnp.int32, (n_pages, PAGE_SIZE), 0) * PAGE_SIZE
            + lax.broadcasted_iota(jnp.int32, (n_pages, PAGE_SIZE), 1))
    for h in range(N_HEADS):
        bsc_ref[h:h + 1, :] = _bias_lookup(past - ends, rb_ref, h)
        bsw_ref[h:h + 1, :] = _bias_lookup(wb - jw, rb_ref, h)
        bss_ref[h] = _bias_lookup(past - kpos, rb_ref, h)


def _bias_sample(rel_bias, past, wb):
    ncs = past // CMP_STRIDE
    n_pages = past // PAGE_SIZE
    return pl.pallas_call(
        functools.partial(_bias_sample_kernel, past=past, ncs=ncs, wb=wb, n_pages=n_pages),
        in_specs=[pl.BlockSpec(memory_space=pltpu.SMEM)],
        out_shape=[jax.ShapeDtypeStruct((N_HEADS, ncs), F32),
                   jax.ShapeDtypeStruct((N_HEADS, wb), F32),
                   jax.ShapeDtypeStruct((N_HEADS, n_pages, PAGE_SIZE), F32)],
        compiler_params=_params(),
        name="bias_sample",
    )(rel_bias)


IN_SPLITS = (D_RNN, D_RNN, D_ATT, KV_COLS, KV_COLS, KV_COLS, LANE)


def _inproj_kernel(x_ref, sh_ref, sc_ref, g_ref, w_ref, *out_refs):
    h = _rms(x_ref[...], g_ref[...]) * (1.0 + sc_ref[...]) + sh_ref[...]
    z = _mm(h.astype(BF16), w_ref[...])
    off = 0
    for o_ref, wdt in zip(out_refs, IN_SPLITS):
        o_ref[...] = z[:, off:off + wdt]
        off += wdt


def _inproj(x, mod, mod_spec, g1, w_in_p, tm):
    n = x.shape[0]
    cols = w_in_p.shape[1]
    return pl.pallas_call(
        _inproj_kernel,
        grid=(n // tm,),
        in_specs=[pl.BlockSpec((tm, D_MODEL), lambda i: (i, 0)),
                  mod_spec(0), mod_spec(1),
                  pl.BlockSpec((1, D_MODEL), lambda i: (0, 0)),
                  pl.BlockSpec((D_MODEL, cols), lambda i: (0, 0))],
        out_specs=[pl.BlockSpec((tm, wdt), lambda i: (i, 0)) for wdt in IN_SPLITS],
        out_shape=[jax.ShapeDtypeStruct((n, wdt), F32) for wdt in IN_SPLITS],
        compiler_params=_params("arbitrary"),
    )(x, mod, mod, g1, w_in_p)


def _lru_gates(xc, wg_ref, bg_ref, lam_ref):
    gates = _mm(xc.astype(BF16), wg_ref[...]) + bg_ref[...]
    r = jax.nn.sigmoid(gates[:, :D_RNN])
    i = jax.nn.sigmoid(gates[:, D_RNN:])
    log_a = -LRU_C * r * jax.nn.softplus(-lam_ref[...])
    a = jnp.exp(log_a)
    u = jnp.sqrt(-jnp.tanh(log_a) * (a * a + 1.0)) * i * xc
    return a, u


def _lru_prompt_kernel(xr_ref, yg_ref, cw_ref, cb_ref, wg_ref, bg_ref, lam_ref,
                       o_ref, hl_ref, cv_ref, xp_s, a_s, u_s, hs_s, h_s, *, nb, tc):
    t = pl.program_id(0)

    @pl.when(t == 0)
    def _():
        xp_s[:, 0:8, :] = jnp.zeros((nb, 8, D_RNN), F32)
        h_s[...] = jnp.zeros((nb, 1, D_RNN), F32)

    @pl.when(t > 0)
    def _():
        xp_s[:, 0:8, :] = xp_s[:, tc:tc + 8, :]

    xp_s[:, 8:, :] = xr_ref[...]
    xc = xp_s[:, 5:5 + tc, :] * cw_ref[0:1, :]
    for k in range(1, CONV_W):
        xc = xc + xp_s[:, 5 + k:5 + k + tc, :] * cw_ref[k:k + 1, :]
    xc = xc + cb_ref[...]
    a, u = _lru_gates(xc.reshape(nb * tc, D_RNN), wg_ref, bg_ref, lam_ref)
    a_s[...] = a.reshape(nb, tc, D_RNN)
    u_s[...] = u.reshape(nb, tc, D_RNN)

    def step(tt, h):
        h = a_s[:, pl.ds(tt, 1), :] * h + u_s[:, pl.ds(tt, 1), :]
        hs_s[:, pl.ds(tt, 1), :] = h
        return h

    h = lax.fori_loop(0, tc, step, h_s[...], unroll=8)
    h_s[...] = h
    o_ref[...] = hs_s[...] * jax.nn.gelu(yg_ref[...])
    hl_ref[...] = h
    cv_ref[...] = xp_s[:, tc + 8 - (CONV_W - 1):tc + 8, :]


def _lru_prompt(xr, yg, cw, cb, wg, bg, lam, tc):
    nb, t, _ = xr.shape
    blk = pl.BlockSpec((nb, tc, D_RNN), lambda i: (0, i, 0))
    full = lambda shape: pl.BlockSpec(shape, lambda i: tuple(0 for _ in shape))
    return pl.pallas_call(
        functools.partial(_lru_prompt_kernel, nb=nb, tc=tc),
        grid=(t // tc,),
        in_specs=[blk, blk, full(cw.shape), full(cb.shape), full(wg.shape), full(bg.shape),
                  full(lam.shape)],
        out_specs=[blk, full((nb, 1, D_RNN)), full((nb, CONV_W - 1, D_RNN))],
        out_shape=[jax.ShapeDtypeStruct((nb, t, D_RNN), F32),
                   jax.ShapeDtypeStruct((nb, 1, D_RNN), F32),
                   jax.ShapeDtypeStruct((nb, CONV_W - 1, D_RNN), F32)],
        scratch_shapes=[pltpu.VMEM((nb, tc + 8, D_RNN), F32), pltpu.VMEM((nb, tc, D_RNN), F32),
                        pltpu.VMEM((nb, tc, D_RNN), F32), pltpu.VMEM((nb, tc, D_RNN), F32),
                        pltpu.VMEM((nb, 1, D_RNN), F32)],
        compiler_params=_params("arbitrary"),
    )(xr, yg, cw, cb, wg, bg, lam)


def _lru_sample_kernel(x_ref, yg_ref, b0_ref, b1_ref, b2_ref, h0_ref, cw_ref, cb_ref, wg_ref,
                       bg_ref, lam_ref, o_ref, hn_ref):
    xc = b0_ref[...] * cw_ref[0:1, :]
    xc = xc + b1_ref[...] * cw_ref[1:2, :]
    xc = xc + b2_ref[...] * cw_ref[2:3, :]
    xc = xc + x_ref[...] * cw_ref[3:4, :]
    xc = xc + cb_ref[...]
    a, u = _lru_gates(xc, wg_ref, bg_ref, lam_ref)
    h = a * h0_ref[...] + u
    hn_ref[...] = h
    o_ref[...] = h * jax.nn.gelu(yg_ref[...])


def _lru_sample(x, yg, buf, h0, cw, cb, wg, bg, lam):
    s = x.shape[0]
    return pl.pallas_call(
        _lru_sample_kernel,
        out_shape=[jax.ShapeDtypeStruct((s, D_RNN), F32), jax.ShapeDtypeStruct((s, D_RNN), F32)],
        compiler_params=_params(),
    )(x, yg, buf[:, 0], buf[:, 1], buf[:, 2], h0, cw, cb, wg, bg, lam)


CHUNK_COLS = CMP_STRIDE * KV_COLS


def _poshid_kernel(pos_ref, w_ref, o_ref):
    parts = [jnp.dot(pos_ref[c], w_ref[c], precision=HIGHEST, preferred_element_type=F32)[0:1]
             for c in range(2)]
    o_ref[...] = jnp.concatenate(parts, axis=1)


def _poshid(cmp_pos, cmp_w1):
    k = CMP_LEN * HEAD_DIM
    pos = jnp.broadcast_to(cmp_pos.reshape(2, 1, k), (2, 8, k))
    return pl.pallas_call(
        _poshid_kernel,
        out_shape=jax.ShapeDtypeStruct((1, 2 * HEAD_DIM), F32),
        compiler_params=_params(),
    )(pos, cmp_w1.reshape(2, k, HEAD_DIM))


def _compress_rows(x_s, n, w1_ref, w2_ref, ph_ref):
    outs = []
    for g in range(N_KV):
        xg = jnp.concatenate(
            [x_s[g, pl.ds(s, n, stride=CMP_STRIDE), :].astype(BF16) for s in range(CMP_STRIDE)],
            axis=1)
        p = _mm(xg, w1_ref[...])
        hid = ph_ref[...] + p[:, :LANE] + pltpu.roll(p[:, LANE:], n - 1, axis=0)
        outs.append(_mm(jax.nn.gelu(hid).astype(BF16), w2_ref[...]))
    return jnp.concatenate(outs, axis=1)


def _compress_prompt_kernel(x_ref, w1_ref, w2_ref, ph_ref, o_ref, x_s, *, n):
    for g in range(N_KV):
        x_s[g] = x_ref[:, g * LANE:(g + 1) * LANE]
    o_ref[...] = _compress_rows(x_s, n, w1_ref, w2_ref, ph_ref)


def _compress_prompt(kvc, w1g, w2bd, ph):
    nb, t, _ = kvc.shape
    n = t // CMP_STRIDE
    return pl.pallas_call(
        functools.partial(_compress_prompt_kernel, n=n),
        grid=(nb,),
        in_specs=[pl.BlockSpec((None, t, KV_COLS), lambda b: (b, 0, 0)),
                  pl.BlockSpec(w1g.shape, lambda b: (0, 0)),
                  pl.BlockSpec(w2bd.shape, lambda b: (0, 0)),
                  pl.BlockSpec(ph.shape, lambda b: (0, 0))],
        out_specs=pl.BlockSpec((None, n, KV_COLS), lambda b: (b, 0, 0)),
        out_shape=jax.ShapeDtypeStruct((nb, n, KV_COLS), F32),
        scratch_shapes=[pltpu.VMEM((N_KV, t, LANE), F32)],
        compiler_params=_params("arbitrary"),
        name="compress_prompt",
    )(kvc, w1g, w2bd, ph)


MAX_PAGES_PER_STEP = 32


def _compress_sample_kernel(pt_ref, *refs, n, pps):
    page_refs = refs[:pps]
    w1_ref, w2_ref, ph_ref, o_ref, x_s = refs[pps:]
    p = pl.program_id(1)
    for k in range(pps):
        r0 = pl.multiple_of((p * pps + k) * PAGE_SIZE, PAGE_SIZE)
        for g in range(N_KV):
            x_s[g, pl.ds(r0, PAGE_SIZE), :] = page_refs[k][g].T

    @pl.when(p == pl.num_programs(1) - 1)
    def _():
        o_ref[...] = _compress_rows(x_s, n, w1_ref, w2_ref, ph_ref)


def _compress_sample(cache_t, page_table, w1g, w2bd, ph):
    s, n_pages = page_table.shape
    n = n_pages * PAGE_SIZE // CMP_STRIDE
    pps = math.gcd(n_pages, MAX_PAGES_PER_STEP)

    def page_spec(k):
        return pl.BlockSpec((None, N_KV, LANE, PAGE_SIZE),
                            lambda b, p, pt: (pt[b, p * pps + k], 0, 0, 0))

    const = lambda shape: pl.BlockSpec(shape, lambda b, p, pt: (0, 0))
    return pl.pallas_call(
        functools.partial(_compress_sample_kernel, n=n, pps=pps),
        grid_spec=pltpu.PrefetchScalarGridSpec(
            num_scalar_prefetch=1,
            grid=(s, n_pages // pps),
            in_specs=[page_spec(k) for k in range(pps)]
            + [const(w1g.shape), const(w2bd.shape), const(ph.shape)],
            out_specs=pl.BlockSpec((None, n, KV_COLS), lambda b, p, pt: (b, 0, 0)),
            scratch_shapes=[pltpu.VMEM((N_KV, n_pages * PAGE_SIZE, LANE), F32)]),
        out_shape=jax.ShapeDtypeStruct((s, n, KV_COLS), F32),
        compiler_params=_params("arbitrary", "arbitrary"),
        name="compress_sample",
    )(page_table, *([cache_t] * pps), w1g, w2bd, ph)


def _select_top(sc, top, want_idx):
    rows, nlane = sc.shape
    j = lax.broadcasted_iota(jnp.int32, (rows, nlane), 1).astype(F32)
    mask = jnp.zeros((rows, nlane), F32)
    picks = []
    for _ in range(top):
        m = jnp.max(sc, axis=-1, keepdims=True)
        idx = jnp.min(jnp.where(sc == m, j, float(nlane)), axis=-1, keepdims=True)
        pick = j == idx
        mask = jnp.where(pick, 1.0, mask)
        sc = jnp.where(pick, REMOVED, sc)
        picks.append(idx)
    if want_idx:
        return mask, jnp.concatenate(picks, axis=1).astype(jnp.int32)
    return mask


def _selection_scores(score, blk, qpos, nsel):
    cur = qpos // SEL_BLOCK
    forced = (blk == 0) | (blk == cur) | (blk == cur - 1)
    sc = jnp.where(forced, FORCE_SCORE, jnp.where(blk * SEL_BLOCK <= qpos, score, -1.0))
    return jnp.where(blk < nsel, sc, REMOVED)


def _softmax_rows(s, valid):
    s = jnp.where(valid, s, NEG_INF)
    p = jnp.exp(s - jnp.max(s, axis=-1, keepdims=True))
    return p / jnp.sum(p, axis=-1, keepdims=True)


def _cattn_prompt_kernel(q_ref, ck_ref, bc_ref, covt_ref, oc_ref, sel_ref, *, tq, nc, nsel, top):
    qpos = pl.program_id(1) * tq + lax.broadcasted_iota(jnp.int32, (tq, 1), 0)
    n_idx = lax.broadcasted_iota(jnp.int32, (1, nc), 1)
    valid = (qpos >= n_idx * CMP_STRIDE + (CMP_LEN - 1)) & (n_idx < nc - 1)
    anyv = jnp.max(valid.astype(F32), axis=-1, keepdims=True)
    blk_t = lax.broadcasted_iota(jnp.int32, (nsel, 1), 0)
    qpos_t = pl.program_id(1) * tq + lax.broadcasted_iota(jnp.int32, (1, tq), 1)
    row_t = lax.broadcasted_iota(jnp.int32, (nsel, tq), 0).astype(F32)
    outs, masks = [], []
    for g in range(N_KV):
        kv = ck_ref[:, g * LANE:(g + 1) * LANE].astype(BF16)
        k, v = kv[:, :HEAD_DIM], kv[:, HEAD_DIM:]
        psum = jnp.zeros((tq, nc), F32)
        for r in range(GROUP):
            h = g * GROUP + r
            qh = q_ref[:, h * HEAD_DIM:(h + 1) * HEAD_DIM].astype(BF16)
            p = _softmax_rows(_nt(qh, k) * SCALE + bc_ref[h], valid) * anyv
            outs.append(_mm(p.astype(BF16), v))
            psum = psum + p
        score_t = lax.dot_general(covt_ref[...], psum, (((1,), (1,)), ((), ())),
                                  precision=HIGHEST, preferred_element_type=F32)
        _, rank = _extract_top(_selection_scores(score_t, blk_t, qpos_t, nsel), top, row_t)
        masks.append(jnp.where(rank < top, 1.0, 0.0).T)
    oc_ref[...] = jnp.concatenate(outs, axis=1)
    sel_ref[...] = jnp.concatenate(masks, axis=1)


def _covers(n_cmp_rows, n_cmp, n_sel, n_sel_cols):
    start = np.arange(n_cmp_rows)[:, None] * CMP_STRIDE
    blk = np.arange(n_sel_cols)[None, :]
    cov = ((start < (blk + 1) * SEL_BLOCK) & (start + CMP_LEN > blk * SEL_BLOCK)
           & (np.arange(n_cmp_rows)[:, None] < n_cmp) & (blk < n_sel))
    return jnp.asarray(cov.astype(np.float32))


def _cattn_prompt(q, ck, bc, tq):
    nb, t, _ = q.shape
    nc = ck.shape[1]
    nsel = t // SEL_BLOCK
    top = min(SEL_TOP, nsel)
    cov = _covers(nc, nc - 1, nsel, nsel).T
    return pl.pallas_call(
        functools.partial(_cattn_prompt_kernel, tq=tq, nc=nc, nsel=nsel, top=top),
        grid=(nb, t // tq),
        in_specs=[pl.BlockSpec((None, tq, D_ATT), lambda b, i: (b, i, 0)),
                  pl.BlockSpec((None, nc, KV_COLS), lambda b, i: (b, 0, 0)),
                  pl.BlockSpec((N_HEADS, tq, nc), lambda b, i: (0, i, 0)),
                  pl.BlockSpec(cov.shape, lambda b, i: (0, 0))],
        out_specs=[pl.BlockSpec((None, tq, D_ATT), lambda b, i: (b, i, 0)),
                   pl.BlockSpec((None, tq, N_KV * nsel), lambda b, i: (b, i, 0))],
        out_shape=[jax.ShapeDtypeStruct((nb, t, D_ATT), F32),
                   jax.ShapeDtypeStruct((nb, t, N_KV * nsel), F32)],
        compiler_params=_params("arbitrary", "arbitrary"),
    )(q, ck, bc, cov)


SAMPLES_PER_STEP = 8


def _cattn_sample_kernel(q_ref, ck_ref, bsc_ref, cov_ref, oc_ref, idx_ref, *, past, nc, nsel, top):
    n_idx = lax.broadcasted_iota(jnp.int32, (1, nc), 1)
    valid = (past >= n_idx * CMP_STRIDE + (CMP_LEN - 1)) & (n_idx < nc - 1)
    anyv = jnp.max(valid.astype(F32), axis=-1, keepdims=True)
    nlane = cov_ref.shape[1]
    blk = lax.broadcasted_iota(jnp.int32, (1, nlane), 1)
    idxs = []
    for g in range(N_KV):
        psums = []
        for i in range(SAMPLES_PER_STEP):
            kv = ck_ref[i, :, g * LANE:(g + 1) * LANE].astype(BF16)
            q4 = q_ref[i, g * GROUP:(g + 1) * GROUP, :].astype(BF16)
            s = _nt(q4, kv[:, :HEAD_DIM]) * SCALE + bsc_ref[g * GROUP:(g + 1) * GROUP, :]
            p = _softmax_rows(s, valid) * anyv
            oc_ref[i, g * GROUP:(g + 1) * GROUP, :] = _mm(p.astype(BF16), kv[:, HEAD_DIM:])
            psums.append(jnp.sum(p, axis=0, keepdims=True))
        score = jnp.dot(jnp.concatenate(psums, axis=0), cov_ref[...], precision=HIGHEST,
                        preferred_element_type=F32)
        _, idx = _select_top(_selection_scores(score, blk, past, nsel), top, True)
        idxs.append(idx)
    idx_ref[...] = jnp.concatenate(idxs, axis=1)


def _cattn_sample(q, ck, bsc, past):
    s = q.shape[0]
    nc = ck.shape[1]
    nsel = -(-(past + 1) // SEL_BLOCK)
    top = min(SEL_TOP, nsel)
    nlane = -(-nsel // LANE) * LANE
    cov = _covers(nc, nc - 1, nsel, nlane)
    g = SAMPLES_PER_STEP
    return pl.pallas_call(
        functools.partial(_cattn_sample_kernel, past=past, nc=nc, nsel=nsel, top=top),
        grid=(s // g,),
        in_specs=[pl.BlockSpec((g, N_HEADS, HEAD_DIM), lambda i: (i, 0, 0)),
                  pl.BlockSpec((g, nc, KV_COLS), lambda i: (i, 0, 0)),
                  pl.BlockSpec(bsc.shape, lambda i: (0, 0)),
                  pl.BlockSpec(cov.shape, lambda i: (0, 0))],
        out_specs=[pl.BlockSpec((g, N_HEADS, HEAD_DIM), lambda i: (i, 0, 0)),
                   pl.BlockSpec((g, N_KV * top), lambda i: (i, 0))],
        out_shape=[jax.ShapeDtypeStruct((s, N_HEADS, HEAD_DIM), F32),
                   jax.ShapeDtypeStruct((s, N_KV * top), jnp.int32)],
        compiler_params=_params("arbitrary"),
    )(q.reshape(s, N_HEADS, HEAD_DIM), ck, bsc, cov)


SEL_TILES = 4


def _flash_tiles(qt, kvs, biases, pens, m_ref, acc_ref):
    kv = jnp.concatenate(kvs, axis=0)
    pen = jnp.concatenate(pens, axis=0)
    s = (_mm(kv.astype(BF16), qt) + jnp.concatenate(biases, axis=0)
         + jnp.concatenate([pen] * GROUP, axis=1))
    m_old = m_ref[...]
    m_new = jnp.maximum(m_old, jnp.max(s, axis=0, keepdims=True))
    row = lax.broadcasted_iota(jnp.int32, (LANE, kv.shape[0]), 0)
    ones_vt = jnp.where(row < HEAD_DIM, 1.0, kv.T).astype(BF16)
    acc_ref[...] = (jnp.exp(m_old - m_new) * acc_ref[...]
                    + _mm(ones_vt, jnp.exp(s - m_new).astype(BF16)))
    m_ref[...] = m_new


def _sattn_prompt_kernel(q_ref, sel_ref, kvs_ref, kvw_ref, bt_ref, os_ref, ow_ref,
                         m_s, acc_s, *, nsel, nd, npw):
    c = pl.program_id(1)
    krow = lax.broadcasted_iota(jnp.int32, (LANE, LANE), 0)
    qcol = lax.broadcasted_iota(jnp.int32, (LANE, LANE), 1)
    ej = lax.broadcasted_iota(jnp.int32, (LANE, nsel), 1)
    ek = lax.broadcasted_iota(jnp.int32, (LANE, nsel), 0) // SEL_BLOCK
    q_t = (q_ref[...] * SCALE).T
    zpad = jnp.zeros((LANE - HEAD_DIM, GROUP * LANE), F32)
    qts = [jnp.concatenate(
        [jnp.concatenate([q_t[(g * GROUP + r) * HEAD_DIM:(g * GROUP + r + 1) * HEAD_DIM]
                          for r in range(GROUP)], axis=1), zpad], axis=0).astype(BF16)
        for g in range(N_KV)]
    selgs = [sel_ref[:, g * nsel:(g + 1) * nsel].astype(BF16) for g in range(N_KV)]

    def reset():
        m_s[...] = jnp.full(m_s.shape, NEG_INF, F32)
        acc_s[...] = jnp.zeros(acc_s.shape, F32)

    def result():
        outs = []
        for g in range(N_KV):
            acc = acc_s[g]
            o = (acc / acc[0:1, :]).T[:, HEAD_DIM:]
            outs += [o[r * LANE:(r + 1) * LANE] for r in range(GROUP)]
        return jnp.concatenate(outs, axis=1)

    def bias_tiles(g, kb):
        dl = jnp.clip(c - kb, 0, nd - 1)
        return jnp.concatenate([bt_ref[g * GROUP + r, dl] for r in range(GROUP)], axis=1)

    def kv_tile(ref, g, kb):
        k0 = pl.multiple_of(jnp.maximum(kb, 0) * LANE, LANE)
        return ref[pl.ds(k0, LANE), g * LANE:(g + 1) * LANE]

    def update(ref, g, kbs, pens):
        _flash_tiles(qts[g], [kv_tile(ref, g, kb) for kb in kbs],
                     [bias_tiles(g, kb) for kb in kbs], pens, m_s.at[g], acc_s.at[g])

    def sel_step(kp, carry):
        kbs = tuple(SEL_TILES * kp + k for k in range(SEL_TILES))
        for g in range(N_KV):
            pens = []
            for kb in kbs:
                expand = (ej == (LANE // SEL_BLOCK) * kb + ek).astype(BF16)
                chosen = _nt(expand, selgs[g]) > 0.5
                dist = (c - kb) * LANE + qcol - krow
                pens.append(jnp.where(chosen & (dist >= 0), 0.0, NEG_INF))
            update(kvs_ref, g, kbs, pens)
        return carry

    reset()
    lax.fori_loop(0, (c + SEL_TILES) // SEL_TILES, sel_step, 0)
    os_ref[...] = result()

    reset()
    for kp in range(npw):
        kbs = (c - 2 * (npw - kp) + 1, c - 2 * (npw - kp) + 2)
        pens = []
        for kb in kbs:
            dist = jnp.where(kb >= 0, (c - kb) * LANE + qcol - krow, -1)
            pens.append(jnp.where((dist >= 0) & (dist < WINDOW), 0.0, NEG_INF))
        for g in range(N_KV):
            update(kvw_ref, g, kbs, pens)
    ow_ref[...] = result()


def _sattn_prompt(q, sel, kvs, kvw, bt):
    nb, t, _ = q.shape
    nsel = t // SEL_BLOCK
    nd = bt.shape[1]
    npw = (WINDOW // LANE + 2) // 2
    qblk = pl.BlockSpec((None, LANE, D_ATT), lambda b, i: (b, i, 0))
    kvblk = pl.BlockSpec((None, t, KV_COLS), lambda b, i: (b, 0, 0))
    return pl.pallas_call(
        functools.partial(_sattn_prompt_kernel, nsel=nsel, nd=nd, npw=npw),
        grid=(nb, t // LANE),
        in_specs=[qblk, pl.BlockSpec((None, LANE, N_KV * nsel), lambda b, i: (b, i, 0)),
                  kvblk, kvblk, pl.BlockSpec(bt.shape, lambda b, i: (0, 0, 0, 0))],
        out_specs=[qblk, qblk],
        out_shape=[jax.ShapeDtypeStruct((nb, t, D_ATT), F32)] * 2,
        scratch_shapes=[pltpu.VMEM((N_KV, 1, GROUP * LANE), F32),
                        pltpu.VMEM((N_KV, LANE, GROUP * LANE), F32)],
        compiler_params=_params("arbitrary", "arbitrary"),
        name="sattn_prompt",
    )(q, sel, kvs, kvw, bt)


def _sattn_sample_kernel(idx_ref, pt_ref, *refs, top, npb, n_pages, wb):
    nblk = N_KV * top
    blk_refs = refs[:nblk]
    (q_ref, ksn_ref, kwn_ref, win_ref, bss_ref, bsw_ref, b0_ref, os_ref, ow_ref) = refs[nblk:]
    b = pl.program_id(0)
    halves = PAGE_SIZE // SEL_BLOCK
    half = lax.broadcasted_iota(jnp.int32, (1, PAGE_SIZE), 1) // SEL_BLOCK
    jw = lax.broadcasted_iota(jnp.int32, (1, wb), 1)
    for g in range(N_KV):
        hs = slice(g * GROUP, (g + 1) * GROUP)
        q4f = q_ref[hs, :]
        q4 = q4f.astype(BF16)
        b0 = b0_ref[hs, :]

        def new_logit(new_ref):
            kn = new_ref[:, g * LANE:g * LANE + HEAD_DIM]
            return jnp.sum(q4f * kn, axis=-1, keepdims=True) * SCALE + b0

        def new_value(new_ref):
            return new_ref[:, g * LANE + HEAD_DIM:(g + 1) * LANE]

        logits, vts = [], []
        for j in range(top):
            ib = idx_ref[b, g * top + j]
            page = jnp.minimum(ib // halves, n_pages - 1)
            s = _mm(q4, blk_refs[g * top + j][0].astype(BF16)) * SCALE + bss_ref[page, hs, :]
            logits.append(jnp.where(half == jnp.where(ib < npb, ib % halves, -1), s, NEG_INF))
            vts.append(blk_refs[g * top + j][1].astype(BF16))
        s_new = new_logit(ksn_ref)
        m = s_new
        for s in logits:
            m = jnp.maximum(m, jnp.max(s, axis=-1, keepdims=True))
        p_new = jnp.exp(s_new - m)
        l = p_new
        acc = p_new * new_value(ksn_ref)
        for s, vt in zip(logits, vts):
            p = jnp.exp(s - m)
            l = l + jnp.sum(p, axis=-1, keepdims=True)
            acc = acc + _nt(p.astype(BF16), vt)
        os_ref[hs, :] = acc / l

        s = _mm(q4, win_ref[g, 0].astype(BF16)) * SCALE + bsw_ref[hs, :]
        s = jnp.where(wb - jw < WINDOW, s, NEG_INF)
        s_new = new_logit(kwn_ref)
        m = jnp.maximum(s_new, jnp.max(s, axis=-1, keepdims=True))
        p = jnp.exp(s - m)
        p_new = jnp.exp(s_new - m)
        l = p_new + jnp.sum(p, axis=-1, keepdims=True)
        acc = p_new * new_value(kwn_ref) + _nt(p.astype(BF16), win_ref[g, 1].astype(BF16))
        ow_ref[hs, :] = acc / l


def _sattn_sample(q, idx, page_table, sel_t, ks_new, kw_new, win_t, bss, bsw, b0, top, npb):
    s, n_pages = page_table.shape
    wb = win_t.shape[-1]
    halves = PAGE_SIZE // SEL_BLOCK

    def blk_spec(g, j):
        def imap(b, idx_r, pt_r):
            ib = idx_r[b, g * top + j]
            return (pt_r[b, jnp.minimum(ib // halves, n_pages - 1)], g, 0, 0, 0)
        return pl.BlockSpec((None, None, 2, HEAD_DIM, PAGE_SIZE), imap)

    per_b = lambda shape: pl.BlockSpec((None,) + shape,
                                       lambda b, i_r, p_r: (b,) + tuple(0 for _ in shape))
    const = lambda shape: pl.BlockSpec(shape, lambda b, i_r, p_r: tuple(0 for _ in shape))
    out = pl.pallas_call(
        functools.partial(_sattn_sample_kernel, top=top, npb=npb, n_pages=n_pages, wb=wb),
        grid_spec=pltpu.PrefetchScalarGridSpec(
            num_scalar_prefetch=2,
            grid=(s,),
            in_specs=[blk_spec(g, j) for g in range(N_KV) for j in range(top)]
            + [per_b((N_HEADS, HEAD_DIM)), per_b((1, KV_COLS)), per_b((1, KV_COLS)),
               per_b(win_t.shape[1:]), const(bss.shape), const(bsw.shape), const(b0.shape)],
            out_specs=[per_b((N_HEADS, HEAD_DIM)), per_b((N_HEADS, HEAD_DIM))]),
        out_shape=[jax.ShapeDtypeStruct((s, N_HEADS, HEAD_DIM), F32)] * 2,
        compiler_params=_params("arbitrary"),
        name="sattn_sample",
    )(idx, page_table, *([sel_t] * (N_KV * top)), q.reshape(s, N_HEADS, HEAD_DIM),
      ks_new.reshape(s, 1, KV_COLS), kw_new.reshape(s, 1, KV_COLS), win_t, bss, bsw, b0)
    return out[0].reshape(s, D_ATT), out[1].reshape(s, D_ATT)


def _outproj_kernel(x_ref, ol_ref, oc_ref, os_ref, ow_ref, gl_ref, gt1_ref, sh2_ref, sc2_ref,
                    gexp_ref, nl_ref, na_ref, n2_ref, wo_ref, x1_ref, h2_ref):
    gates = jax.nn.sigmoid(jnp.dot(gl_ref[...], gexp_ref[...], precision=HIGHEST,
                                   preferred_element_type=F32))
    o_att = (gates[:, :D_ATT] * oc_ref[...] + gates[:, D_ATT:2 * D_ATT] * os_ref[...]
             + gates[:, 2 * D_ATT:] * ow_ref[...])
    n_lru = _rms(ol_ref[...], nl_ref[...]).astype(BF16)
    n_att = _rms(o_att, na_ref[...]).astype(BF16)
    mixed = _mm(n_lru, wo_ref[:D_RNN, :]) + _mm(n_att, wo_ref[D_RNN:, :])
    x1 = x_ref[...] + gt1_ref[...] * mixed
    x1_ref[...] = x1
    h2_ref[...] = (_rms(x1, n2_ref[...]) * (1.0 + sc2_ref[...]) + sh2_ref[...]).astype(BF16)


def _gate_expand():
    e = np.zeros((LANE, 3 * D_ATT), np.float32)
    for br in range(3):
        for h in range(N_HEADS):
            e[br * N_HEADS + h, br * D_ATT + h * HEAD_DIM:br * D_ATT + (h + 1) * HEAD_DIM] = 1.0
    return jnp.asarray(e)


def _outproj(x, o_lru, o_c, o_s, o_w, gl, mod, mod_spec, nl, na, n2, wo, tm):
    n = x.shape[0]
    tok = lambda w: pl.BlockSpec((tm, w), lambda i: (i, 0))
    const = lambda shape: pl.BlockSpec(shape, lambda i: (0, 0))
    gexp = _gate_expand()
    return pl.pallas_call(
        _outproj_kernel,
        grid=(n // tm,),
        in_specs=[tok(D_MODEL), tok(D_RNN), tok(D_ATT), tok(D_ATT), tok(D_ATT), tok(LANE),
                  mod_spec(2), mod_spec(3), mod_spec(4),
                  const(gexp.shape), const(nl.shape), const(na.shape), const(n2.shape),
                  const(wo.shape)],
        out_specs=[tok(D_MODEL), tok(D_MODEL)],
        out_shape=[jax.ShapeDtypeStruct((n, D_MODEL), F32),
                   jax.ShapeDtypeStruct((n, D_MODEL), BF16)],
        compiler_params=_params("arbitrary"),
    )(x, o_lru, o_c, o_s, o_w, gl, mod, mod, mod, gexp, nl, na, n2, wo)


def _extract_top(x, top, row):
    nrow = x.shape[0]
    vals = []
    rank = jnp.full(x.shape, float(top), F32)
    for k in range(top):
        m = jnp.max(x, axis=0, keepdims=True)
        idx = jnp.min(jnp.where(x == m, row, float(nrow)), axis=0, keepdims=True)
        pick = row == idx
        rank = jnp.where(pick, float(k), rank)
        x = jnp.where(pick, REMOVED, x)
        vals.append(m)
    return jnp.concatenate(vals, axis=0), rank


PAIR_COUNTS = tuple(PEER_TOPK // (a + 1) for a in range(PEER_TOPK))
N_PAIRS = sum(PAIR_COUNTS)
PAIR_ROWS = -(-N_PAIRS // 8) * 8


def _peer_score_kernel(h2_ref, wq_ref, bq_ref, sk_ref, bc_ref, a0_ref, r1_ref, e1_ref,
                       q_s, cand_s, *, tn):
    q_s[...] = (_nt(wq_ref[...], h2_ref[...]) + bq_ref[...]).astype(BF16)
    row = lax.broadcasted_iota(jnp.int32, (PEER_KEYS, tn), 0).astype(F32)
    crow = lax.broadcasted_iota(jnp.int32, (PAIR_ROWS, tn), 0).astype(F32)

    def head(h, carry):
        r0 = pl.multiple_of(h * 2 * PEER_KEYS, 2 * PEER_KEYS)
        s0 = _mm(sk_ref[2 * h], q_s[pl.ds(r0, PEER_KEYS), :])
        s1 = _mm(sk_ref[2 * h + 1], q_s[pl.ds(r0 + PEER_KEYS, PEER_KEYS), :])
        v0, rank0 = _extract_top(s0, PEER_TOPK, row)
        v1, rank1 = _extract_top(s1, PEER_TOPK, row)
        off = 0
        for a, cnt in enumerate(PAIR_COUNTS):
            cand_s[off:off + cnt, :] = v1[0:cnt] + v0[a:a + 1]
            off += cnt
        cand_s[N_PAIRS:, :] = jnp.full((PAIR_ROWS - N_PAIRS, tn), REMOVED, F32)
        best, rank_c = _extract_top(cand_s[...], PEER_TOPK, crow)
        z = jnp.sum(jnp.exp(best - best[0:1]), axis=0, keepdims=True)
        cand_s[...] = jnp.where(rank_c < PEER_TOPK, 1.0, 0.0)
        bc = jnp.zeros((PEER_KEYS, tn), F32)
        off = 0
        for a, cnt in enumerate(PAIR_COUNTS):
            kept_a = jnp.sum(cand_s[off:off + cnt, :], axis=0, keepdims=True)
            bc = jnp.where(rank0 == float(a), kept_a, bc)
            off += cnt
        bc_ref[h] = bc
        a0_ref[h] = jnp.where(rank0 < PEER_TOPK, jnp.exp(s0 - v0[0:1]), 0.0) * (0.5 / z)
        r1_ref[h] = rank1.astype(BF16)
        e1_ref[h] = jnp.exp(s1 - v1[0:1]).astype(BF16)
        return carry

    lax.fori_loop(0, PEER_HEADS, head, 0)


def _peer_score(h2, wq_t, bq_t, subkeys, tn):
    n = h2.shape[0]
    dq = wq_t.shape[0]
    tok = pl.BlockSpec((PEER_HEADS, PEER_KEYS, tn), lambda i: (0, 0, i))
    shp = jax.ShapeDtypeStruct((PEER_HEADS, PEER_KEYS, n), F32)
    shp16 = jax.ShapeDtypeStruct((PEER_HEADS, PEER_KEYS, n), BF16)
    return pl.pallas_call(
        functools.partial(_peer_score_kernel, tn=tn),
        grid=(n // tn,),
        in_specs=[pl.BlockSpec((tn, D_MODEL), lambda i: (i, 0)),
                  pl.BlockSpec(wq_t.shape, lambda i: (0, 0)),
                  pl.BlockSpec(bq_t.shape, lambda i: (0, 0)),
                  pl.BlockSpec(subkeys.shape, lambda i: (0, 0, 0))],
        out_specs=[tok, tok, tok, tok],
        out_shape=[shp, shp, shp16, shp16],
        scratch_shapes=[pltpu.VMEM((dq, tn), BF16), pltpu.VMEM((PAIR_ROWS, tn), F32)],
        compiler_params=_params("arbitrary"),
        name="peer_score",
    )(h2, wq_t, bq_t, subkeys)


GELU_C1 = math.sqrt(2.0 / math.pi)
GELU_C2 = 0.044715 * GELU_C1


def _peer_dense_kernel(h2_ref, u_ref, vt_ref, bc_ref, a0_ref, r1_ref, e1_ref, x1_ref,
                       gt2_ref, fg_ref, y_ref, acc_s, act_s, *, tn, eb, lc):
    e = pl.program_id(1)

    @pl.when(e == 0)
    def _():
        acc_s[...] = jnp.zeros(acc_s.shape, F32)

    z = _nt(u_ref[...], h2_ref[...])
    for il in range(eb // PEER_KEYS):
        i = e * (eb // PEER_KEYS) + il
        rs = slice(il * PEER_KEYS, (il + 1) * PEER_KEYS)
        for c0 in range(0, tn, lc):
            ls = slice(c0, c0 + lc)
            w = jnp.zeros((PEER_KEYS, lc), BF16)
            for h in range(PEER_HEADS):
                keep = r1_ref[h, :, ls] < bc_ref[h, pl.ds(i, 1), ls].astype(BF16)
                w = w + (jnp.where(keep, e1_ref[h, :, ls], 0.0)
                         * a0_ref[h, pl.ds(i, 1), ls].astype(BF16))
            zi = z[rs, ls]
            t = jnp.tanh(zi * (GELU_C1 + GELU_C2 * (zi * zi)))
            act_s[rs, ls] = ((zi + zi * t) * w.astype(F32)).astype(BF16)
    acc_s[...] += _mm(vt_ref[...], act_s[...])

    @pl.when(e == pl.num_programs(1) - 1)
    def _():
        x = x1_ref[...] + gt2_ref[...] * acc_s[...].T
        y_ref[...] = _rms(x, fg_ref[...])


def _peer_dense(h2, u, vt, scores, x1, mod, mod_spec2, fg, tn, eb):
    n = h2.shape[0]
    n_exp = u.shape[0]
    bc, a0, r1, e1 = scores
    tok3 = pl.BlockSpec((PEER_HEADS, PEER_KEYS, tn), lambda i, e: (0, 0, i))
    return pl.pallas_call(
        functools.partial(_peer_dense_kernel, tn=tn, eb=eb, lc=min(tn, 2 * LANE)),
        grid=(n // tn, n_exp // eb),
        in_specs=[pl.BlockSpec((tn, D_MODEL), lambda i, e: (i, 0)),
                  pl.BlockSpec((eb, D_MODEL), lambda i, e: (e, 0)),
                  pl.BlockSpec((D_MODEL, eb), lambda i, e: (0, e)),
                  tok3, tok3, tok3, tok3,
                  pl.BlockSpec((tn, D_MODEL), lambda i, e: (i, 0)),
                  mod_spec2(5),
                  pl.BlockSpec((1, D_MODEL), lambda i, e: (0, 0))],
        out_specs=pl.BlockSpec((tn, D_MODEL), lambda i, e: (i, 0)),
        out_shape=jax.ShapeDtypeStruct((n, D_MODEL), F32),
        scratch_shapes=[pltpu.VMEM((D_MODEL, tn), F32), pltpu.VMEM((eb, tn), BF16)],
        compiler_params=_params("arbitrary", "arbitrary"),
        name="peer_dense",
    )(h2, u, vt, bc, a0, r1, e1, x1, mod, fg)


def _block_diag(w):
    nblk, bw, _ = w.shape
    eye = jnp.eye(nblk, dtype=w.dtype)
    return jnp.einsum('nde,nm->ndme', w, eye).reshape(nblk * bw, nblk * bw)


def kernel(x_prompt, x_sample, cache_cmp_kv, cache_sel_kv, cache_win_kv, state_lru_h, state_conv, page_table, c_prompt, c_sample, ada_w, ada_b, norm1_g, norm2_g, w_in, conv_w, conv_b, lru_wa, lru_ba, lru_wi, lru_bi, lru_lambda, cmp_w1, cmp_w2, cmp_pos, out_norm_lru, out_norm_att, w_out, peer_wq, peer_bq, peer_subkeys, peer_u, peer_v, rel_bias, final_g):
    nb, t, d = x_prompt.shape
    ns = x_sample.shape[0]
    n_pages = page_table.shape[1]
    past = n_pages * PAGE_SIZE
    wb = cache_win_kv.shape[2]
    assert x_sample.shape[1] == 1 and ada_w.shape[0] == 1 and d == D_MODEL
    assert t % (SEL_TILES * LANE) == 0 and ns % LANE == 0 and past % LANE == 0
    assert wb == min(WINDOW, past)
    row = lambda v: v.reshape(1, -1)

    w_in_p = jnp.pad(w_in[0], ((0, 0), (0, sum(IN_SPLITS) - w_in.shape[2]))).astype(BF16)
    wg = jnp.concatenate([_block_diag(lru_wa[0]), _block_diag(lru_wi[0])], axis=1).astype(BF16)
    bg = row(jnp.concatenate([lru_ba[0], lru_bi[0]]))
    w1r = cmp_w1[0].reshape(2, 2, CMP_STRIDE, HEAD_DIM, HEAD_DIM)
    eye2 = jnp.eye(2, dtype=F32)
    w1g = jnp.einsum('cmsdh,ce->scdmeh', w1r, eye2).reshape(CMP_STRIDE * LANE, 2 * LANE).astype(BF16)
    w2bd = jnp.einsum('che,cf->chfe', cmp_w2[0], eye2).reshape(LANE, LANE).astype(BF16)
    wq_t = peer_wq[0].T.astype(BF16)
    bq_t = peer_bq[0].reshape(-1, 1)
    subk = peer_subkeys[0].reshape(2 * PEER_HEADS, PEER_KEYS, -1).astype(BF16)
    u_b = peer_u[0].astype(BF16)
    vt_b = peer_v[0].T.astype(BF16)
    wo_b = w_out[0].astype(BF16)

    n_c = nb + ns
    c_all = jnp.pad(jnp.concatenate([c_prompt, c_sample]), ((0, -n_c % 8), (0, 0)))
    mod = _ada(c_all, ada_w[0], ada_b[0])
    mod_p = mod[:nb].reshape(nb, 1, 6 * D_MODEL)
    mod_s = mod[nb:n_c]

    tm_p, tm_s = 512, ns
    tn_p, tn_s = 512, ns
    tn_sc = 256 if ns % 256 == 0 else LANE

    def mod_spec_p(tm):
        per = t // tm
        return lambda j: pl.BlockSpec((None, 1, D_MODEL), lambda i, *_: (i // per, 0, j))

    def mod_spec_s(tm):
        return lambda j: pl.BlockSpec((tm, D_MODEL), lambda i, *_: (i, j))

    thr_last = BUCKET_THR[-1]
    nd = min(t // LANE, -(-(thr_last + LANE - 1) // LANE) + 1)
    bt, bc = _bias_prompt(rel_bias, t, nd)
    bsc, bsw, bss = _bias_sample(rel_bias, past, wb)
    bss = jnp.transpose(bss, (1, 0, 2))
    b0 = rel_bias[0].reshape(N_HEADS, 1)

    ph = _poshid(cmp_pos[0], cmp_w1[0])
    lru_args = (conv_w[0], row(conv_b[0]), wg, bg, row(lru_lambda[0]))
    norms = (row(out_norm_lru[0]), row(out_norm_att[0]), row(norm2_g[0]), wo_b)

    xp = x_prompt.reshape(nb * t, d)
    xr, yg, q, kvc, kvs, kvw, gl = _inproj(xp, mod_p, mod_spec_p(tm_p), row(norm1_g[0]), w_in_p, tm_p)
    seq = lambda a: a.reshape(nb, t, a.shape[-1])
    o_lru, h_p, conv_p = _lru_prompt(seq(xr), seq(yg), *lru_args, tc=256)
    ck = _compress_prompt(seq(kvc), w1g, w2bd, ph)
    o_c, sel = _cattn_prompt(seq(q), ck, bc, 2 * LANE)
    o_s, o_w = _sattn_prompt(seq(q), sel, seq(kvs), seq(kvw), bt)
    flat = lambda a: a.reshape(nb * t, a.shape[-1])
    x1, h2 = _outproj(xp, flat(o_lru), flat(o_c), flat(o_s), flat(o_w), gl, mod_p,
                      mod_spec_p(tm_p), *norms, tm_p)
    scores = _peer_score(h2, wq_t, bq_t, subk, 256)
    y_p = _peer_dense(h2, u_b, vt_b, scores, x1, mod_p, mod_spec_p(tn_p), row(final_g), tn_p, 1024)

    xs = x_sample.reshape(ns, d)
    xr_s, yg_s, q_s, kvc_s, kvs_s, kvw_s, gl_s = _inproj(xs, mod_s, mod_spec_s(tm_s),
                                                          row(norm1_g[0]), w_in_p, tm_s)
    o_lru_s, h_s = _lru_sample(xr_s, yg_s, state_conv[0], state_lru_h[0], *lru_args)
    rows_minor = lambda a: jnp.transpose(a, (0, 2, 3, 4, 1))
    cmp_t = rows_minor(cache_cmp_kv[0]).reshape(-1, N_KV, LANE, PAGE_SIZE)
    ck_s = _compress_sample(cmp_t, page_table, w1g, w2bd, ph)
    o_c_s, idx = _cattn_sample(q_s, ck_s, bsc, past)
    top_s = idx.shape[1] // N_KV
    o_s_s, o_w_s = _sattn_sample(q_s, idx, page_table, rows_minor(cache_sel_kv[0]), kvs_s, kvw_s,
                                 rows_minor(cache_win_kv[0]), bss, bsw, b0, top_s,
                                 past // SEL_BLOCK)
    x1_s, h2_s = _outproj(xs, o_lru_s, o_c_s.reshape(ns, D_ATT), o_s_s, o_w_s, gl_s, mod_s,
                          mod_spec_s(tm_s), *norms, tm_s)
    scores_s = _peer_score(h2_s, wq_t, bq_t, subk, tn_sc if ns % tn_sc == 0 else ns)
    y_s = _peer_dense(h2_s, u_b, vt_b, scores_s, x1_s, mod_s, mod_spec_s(tn_s), row(final_g), tn_s, 1024)

    kv6 = lambda a, n, tt: a.reshape(1, n, tt, N_KV, 2, HEAD_DIM)
    win_p = kv6(kvw, nb, t)[:, :, t - min(WINDOW, t):]
    win_s = jnp.concatenate([cache_win_kv[:, :, 1:], kv6(kvw_s, ns, 1)], axis=2)[:, :, -wb:]
    conv_s = jnp.concatenate([state_conv[:, :, 1:], xr_s.reshape(1, ns, 1, D_RNN)], axis=2)
    return (y_p.reshape(nb, t, d), y_s.reshape(ns, 1, d),
            kv6(kvc, nb, t), kv6(kvc_s, ns, 1),
            kv6(kvs, nb, t), kv6(kvs_s, ns, 1),
            win_p, win_s,
            h_p.reshape(1, nb, D_RNN), h_s.reshape(1, ns, D_RNN),
            conv_p.reshape(1, nb, CONV_W - 1, D_RNN), conv_s)
```

```python
import functools
import math

import numpy as np
import jax
import jax.numpy as jnp
from jax import lax
from jax.experimental import pallas as pl
from jax.experimental.pallas import tpu as pltpu

F32 = jnp.float32
BF16 = jnp.bfloat16
HIGHEST = lax.Precision.HIGHEST

D_MODEL = 1024
D_RNN = 512
LRU_BLOCKS = 8
CONV_W = 4
LRU_C = 8.0
N_HEADS = 8
HEAD_DIM = 64
N_KV = 2
GROUP = 4
D_ATT = 512
KV_COLS = 2 * N_KV * HEAD_DIM
CMP_LEN = 32
CMP_STRIDE = 16
SEL_BLOCK = 64
SEL_TOP = 16
WINDOW = 512
SCALE = HEAD_DIM ** -0.5
LOG2E = math.log2(math.e)
NEG_INF = -1e30
REMOVED = -3e38
FORCE_SCORE = 1e6
NUM_BUCKETS = 32
MAX_DISTANCE = 1024
PEER_HEADS = 8
PEER_KEYS = 128
PEER_TOPK = 16
NORM_EPS = 1e-6
PAGE_SIZE = 128

LANE = 128
VMEM_LIMIT = 56 * 1024 * 1024


def _bucket_thresholds():
    d = np.arange(0, 1 << 15, dtype=np.int64)
    max_exact = NUM_BUCKETS // 2
    df = np.maximum(d, 1).astype(np.float32)
    ratio = (np.log(df / np.float32(max_exact)) / np.float32(math.log(MAX_DISTANCE / max_exact))
             * np.float32(NUM_BUCKETS - max_exact))
    large = np.minimum(max_exact + ratio.astype(np.int32), NUM_BUCKETS - 1)
    bucket = np.where(d < max_exact, d, large)
    assert np.all(np.diff(bucket) >= 0)
    return tuple(int(np.argmax(bucket >= k)) for k in range(1, NUM_BUCKETS))


BUCKET_THR = _bucket_thresholds()


def _params(*sem):
    return pltpu.CompilerParams(dimension_semantics=sem or None, vmem_limit_bytes=VMEM_LIMIT)


def _nt(a, b):
    return lax.dot_general(a, b, (((1,), (1,)), ((), ())), preferred_element_type=F32)


def _mm(a, b):
    return jnp.dot(a, b, preferred_element_type=F32)


def _rms(x, g):
    return x * lax.rsqrt(jnp.mean(x * x, axis=-1, keepdims=True) + NORM_EPS) * g


def _bias_lookup(dist, rb_ref, h):
    b = jnp.full(dist.shape, rb_ref[0, h], F32)
    for k in range(1, NUM_BUCKETS):
        b = jnp.where(dist >= BUCKET_THR[k - 1], rb_ref[k, h], b)
    return b


def _ada_kernel(c_ref, w_ref, b_ref, o_ref):
    c = c_ref[...]
    o_ref[...] = jnp.dot(jax.nn.silu(c), w_ref[...], precision=HIGHEST,
                         preferred_element_type=F32) + b_ref[...]


def _ada(c_all, w, b):
    rows, d = c_all.shape
    cols = w.shape[1]
    tn = 512
    return pl.pallas_call(
        _ada_kernel,
        grid=(cols // tn,),
        in_specs=[pl.BlockSpec((rows, d), lambda j: (0, 0)),
                  pl.BlockSpec((d, tn), lambda j: (0, j)),
                  pl.BlockSpec((1, tn), lambda j: (0, j))],
        out_specs=pl.BlockSpec((rows, tn), lambda j: (0, j)),
        out_shape=jax.ShapeDtypeStruct((rows, cols), F32),
        compiler_params=_params("arbitrary"),
    )(c_all, w, b.reshape(1, cols))


def _bias_prompt_kernel(rb_ref, bt_ref, bc_ref, *, nd, t, nc):
    h = pl.program_id(0)
    row = lax.broadcasted_iota(jnp.int32, (LANE, LANE), 0)
    col = lax.broadcasted_iota(jnp.int32, (LANE, LANE), 1)
    for dl in range(nd):
        dist = dl * LANE + col - row
        b = _bias_lookup(dist, rb_ref, h) * LOG2E
        bt_ref[0, dl] = jnp.where(dist >= 0, b, NEG_INF)
        bt_ref[1, dl] = jnp.where((dist >= 0) & (dist < WINDOW), b, NEG_INF)
    rq = lax.broadcasted_iota(jnp.int32, (LANE, nc), 0)
    ends = lax.broadcasted_iota(jnp.int32, (LANE, nc), 1) * CMP_STRIDE + (CMP_LEN - 1)

    def chunk(i, carry):
        r0 = pl.multiple_of(i * LANE, LANE)
        bc_ref[pl.ds(r0, LANE), :] = _bias_lookup(r0 + rq - ends, rb_ref, h)
        return carry

    lax.fori_loop(0, t // LANE, chunk, 0)


def _bias_prompt(rel_bias, t, nd):
    nc = t // CMP_STRIDE
    return pl.pallas_call(
        functools.partial(_bias_prompt_kernel, nd=nd, t=t, nc=nc),
        grid=(N_HEADS,),
        in_specs=[pl.BlockSpec(memory_space=pltpu.SMEM)],
        out_specs=[pl.BlockSpec((None, 2, nd, LANE, LANE), lambda h: (h, 0, 0, 0, 0)),
                   pl.BlockSpec((None, t, nc), lambda h: (h, 0, 0))],
        out_shape=[jax.ShapeDtypeStruct((N_HEADS, 2, nd, LANE, LANE), F32),
                   jax.ShapeDtypeStruct((N_HEADS, t, nc), F32)],
        compiler_params=_params("arbitrary"),
    )(rel_bias)


def _bias_sample_kernel(rb_ref, bsc_ref, bsw_ref, bss_ref, *, past, ncs, wb, n_pages):
    ends = lax.broadcasted_iota(jnp.int32, (1, ncs), 1) * CMP_STRIDE + (CMP_LEN - 1)
    jw = lax.broadcasted_iota(jnp.int32, (1, wb), 1)
    kpos = (lax.broadcasted_iota(jnp.int32, (n_pages, PAGE_SIZE), 0) * PAGE_SIZE
            + lax.broadcasted_iota(jnp.int32, (n_pages, PAGE_SIZE), 1))
    for h in range(N_HEADS):
        bsc_ref[h:h + 1, :] = _bias_lookup(past - ends, rb_ref, h)
        bsw_ref[h:h + 1, :] = _bias_lookup(wb - jw, rb_ref, h)
        bss_ref[h] = _bias_lookup(past - kpos, rb_ref, h)


def _bias_sample(rel_bias, past, wb):
    ncs = past // CMP_STRIDE
    n_pages = past // PAGE_SIZE
    return pl.pallas_call(
        functools.partial(_bias_sample_kernel, past=past, ncs=ncs, wb=wb, n_pages=n_pages),
        in_specs=[pl.BlockSpec(memory_space=pltpu.SMEM)],
        out_shape=[jax.ShapeDtypeStruct((N_HEADS, ncs), F32),
                   jax.ShapeDtypeStruct((N_HEADS, wb), F32),
                   jax.ShapeDtypeStruct((N_HEADS, n_pages, PAGE_SIZE), F32)],
        compiler_params=_params(),
        name="bias_sample",
    )(rel_bias)


IN_SPLITS = (D_RNN, D_RNN, D_ATT, KV_COLS, KV_COLS, KV_COLS, LANE)


def _inproj_kernel(x_ref, sh_ref, sc_ref, g_ref, w_ref, *out_refs):
    h = _rms(x_ref[...], g_ref[...]) * (1.0 + sc_ref[...]) + sh_ref[...]
    z = _mm(h.astype(BF16), w_ref[...])
    off = 0
    for o_ref, wdt in zip(out_refs, IN_SPLITS):
        o_ref[...] = z[:, off:off + wdt]
        off += wdt


def _inproj(x, mod, mod_spec, g1, w_in_p, tm):
    n = x.shape[0]
    cols = w_in_p.shape[1]
    return pl.pallas_call(
        _inproj_kernel,
        grid=(n // tm,),
        in_specs=[pl.BlockSpec((tm, D_MODEL), lambda i: (i, 0)),
                  mod_spec(0), mod_spec(1),
                  pl.BlockSpec((1, D_MODEL), lambda i: (0, 0)),
                  pl.BlockSpec((D_MODEL, cols), lambda i: (0, 0))],
        out_specs=[pl.BlockSpec((tm, wdt), lambda i: (i, 0)) for wdt in IN_SPLITS],
        out_shape=[jax.ShapeDtypeStruct((n, wdt), F32) for wdt in IN_SPLITS],
        compiler_params=_params("arbitrary"),
    )(x, mod, mod, g1, w_in_p)


def _lru_gates(xc, wg_ref, bg_ref, lam_ref):
    gates = _mm(xc.astype(BF16), wg_ref[...]) + bg_ref[...]
    r = jax.nn.sigmoid(gates[:, :D_RNN])
    i = jax.nn.sigmoid(gates[:, D_RNN:])
    log_a = -LRU_C * r * jax.nn.softplus(-lam_ref[...])
    a = jnp.exp(log_a)
    u = jnp.sqrt(-jnp.tanh(log_a) * (a * a + 1.0)) * i * xc
    return a, u


def _lru_prompt_kernel(xr_ref, yg_ref, cw_ref, cb_ref, wg_ref, bg_ref, lam_ref,
                       o_ref, hl_ref, cv_ref, xp_s, a_s, u_s, hs_s, h_s, *, nb, tc):
    t = pl.program_id(0)

    @pl.when(t == 0)
    def _():
        xp_s[:, 0:8, :] = jnp.zeros((nb, 8, D_RNN), F32)
        h_s[...] = jnp.zeros((nb, 1, D_RNN), F32)

    @pl.when(t > 0)
    def _():
        xp_s[:, 0:8, :] = xp_s[:, tc:tc + 8, :]

    xp_s[:, 8:, :] = xr_ref[...]
    xc = xp_s[:, 5:5 + tc, :] * cw_ref[0:1, :]
    for k in range(1, CONV_W):
        xc = xc + xp_s[:, 5 + k:5 + k + tc, :] * cw_ref[k:k + 1, :]
    xc = xc + cb_ref[...]
    a, u = _lru_gates(xc.reshape(nb * tc, D_RNN), wg_ref, bg_ref, lam_ref)
    a_s[...] = a.reshape(nb, tc, D_RNN)
    u_s[...] = u.reshape(nb, tc, D_RNN)

    def step(tt, h):
        h = a_s[:, pl.ds(tt, 1), :] * h + u_s[:, pl.ds(tt, 1), :]
        hs_s[:, pl.ds(tt, 1), :] = h
        return h

    h = lax.fori_loop(0, tc, step, h_s[...], unroll=8)
    h_s[...] = h
    o_ref[...] = hs_s[...] * jax.nn.gelu(yg_ref[...])
    hl_ref[...] = h
    cv_ref[...] = xp_s[:, tc + 8 - (CONV_W - 1):tc + 8, :]


def _lru_prompt(xr, yg, cw, cb, wg, bg, lam, tc):
    nb, t, _ = xr.shape
    blk = pl.BlockSpec((nb, tc, D_RNN), lambda i: (0, i, 0))
    full = lambda shape: pl.BlockSpec(shape, lambda i: tuple(0 for _ in shape))
    return pl.pallas_call(
        functools.partial(_lru_prompt_kernel, nb=nb, tc=tc),
        grid=(t // tc,),
        in_specs=[blk, blk, full(cw.shape), full(cb.shape), full(wg.shape), full(bg.shape),
                  full(lam.shape)],
        out_specs=[blk, full((nb, 1, D_RNN)), full((nb, CONV_W - 1, D_RNN))],
        out_shape=[jax.ShapeDtypeStruct((nb, t, D_RNN), F32),
                   jax.ShapeDtypeStruct((nb, 1, D_RNN), F32),
                   jax.ShapeDtypeStruct((nb, CONV_W - 1, D_RNN), F32)],
        scratch_shapes=[pltpu.VMEM((nb, tc + 8, D_RNN), F32), pltpu.VMEM((nb, tc, D_RNN), F32),
                        pltpu.VMEM((nb, tc, D_RNN), F32), pltpu.VMEM((nb, tc, D_RNN), F32),
                        pltpu.VMEM((nb, 1, D_RNN), F32)],
        compiler_params=_params("arbitrary"),
    )(xr, yg, cw, cb, wg, bg, lam)


def _lru_sample_kernel(x_ref, yg_ref, b0_ref, b1_ref, b2_ref, h0_ref, cw_ref, cb_ref, wg_ref,
                       bg_ref, lam_ref, o_ref, hn_ref):
    xc = b0_ref[...] * cw_ref[0:1, :]
    xc = xc + b1_ref[...] * cw_ref[1:2, :]
    xc = xc + b2_ref[...] * cw_ref[2:3, :]
    xc = xc + x_ref[...] * cw_ref[3:4, :]
    xc = xc + cb_ref[...]
    a, u = _lru_gates(xc, wg_ref, bg_ref, lam_ref)
    h = a * h0_ref[...] + u
    hn_ref[...] = h
    o_ref[...] = h * jax.nn.gelu(yg_ref[...])


def _lru_sample(x, yg, buf, h0, cw, cb, wg, bg, lam):
    s = x.shape[0]
    return pl.pallas_call(
        _lru_sample_kernel,
        out_shape=[jax.ShapeDtypeStruct((s, D_RNN), F32), jax.ShapeDtypeStruct((s, D_RNN), F32)],
        compiler_params=_params(),
    )(x, yg, buf[:, 0], buf[:, 1], buf[:, 2], h0, cw, cb, wg, bg, lam)


CHUNK_COLS = CMP_STRIDE * KV_COLS


def _poshid_kernel(pos_ref, w_ref, o_ref):
    parts = [jnp.dot(pos_ref[c], w_ref[c], precision=HIGHEST, preferred_element_type=F32)[0:1]
             for c in range(2)]
    o_ref[...] = jnp.concatenate(parts, axis=1)


def _poshid(cmp_pos, cmp_w1):
    k = CMP_LEN * HEAD_DIM
    pos = jnp.broadcast_to(cmp_pos.reshape(2, 1, k), (2, 8, k))
    return pl.pallas_call(
        _poshid_kernel,
        out_shape=jax.ShapeDtypeStruct((1, 2 * HEAD_DIM), F32),
        compiler_params=_params(),
    )(pos, cmp_w1.reshape(2, k, HEAD_DIM))


def _compress_rows(x_s, n, w1_ref, w2_ref, ph_ref):
    outs = []
    for g in range(N_KV):
        xg = jnp.concatenate(
            [x_s[g, pl.ds(s, n, stride=CMP_STRIDE), :].astype(BF16) for s in range(CMP_STRIDE)],
            axis=1)
        p = _mm(xg, w1_ref[...])
        hid = ph_ref[...] + p[:, :LANE] + pltpu.roll(p[:, LANE:], n - 1, axis=0)
        outs.append(_mm(jax.nn.gelu(hid).astype(BF16), w2_ref[...]))
    return jnp.concatenate(outs, axis=1)


def _compress_prompt_kernel(x_ref, w1_ref, w2_ref, ph_ref, o_ref, x_s, *, n):
    for g in range(N_KV):
        x_s[g] = x_ref[:, g * LANE:(g + 1) * LANE]
    o_ref[...] = _compress_rows(x_s, n, w1_ref, w2_ref, ph_ref)


def _compress_prompt(kvc, w1g, w2bd, ph):
    nb, t, _ = kvc.shape
    n = t // CMP_STRIDE
    return pl.pallas_call(
        functools.partial(_compress_prompt_kernel, n=n),
        grid=(nb,),
        in_specs=[pl.BlockSpec((None, t, KV_COLS), lambda b: (b, 0, 0)),
                  pl.BlockSpec(w1g.shape, lambda b: (0, 0)),
                  pl.BlockSpec(w2bd.shape, lambda b: (0, 0)),
                  pl.BlockSpec(ph.shape, lambda b: (0, 0))],
        out_specs=pl.BlockSpec((None, n, KV_COLS), lambda b: (b, 0, 0)),
        out_shape=jax.ShapeDtypeStruct((nb, n, KV_COLS), F32),
        scratch_shapes=[pltpu.VMEM((N_KV, t, LANE), F32)],
        compiler_params=_params("arbitrary"),
        name="compress_prompt",
    )(kvc, w1g, w2bd, ph)


MAX_PAGES_PER_STEP = 32


def _compress_sample_kernel(pt_ref, *refs, n, pps):
    page_refs = refs[:pps]
    w1_ref, w2_ref, ph_ref, o_ref, x_s = refs[pps:]
    p = pl.program_id(1)
    for k in range(pps):
        r0 = pl.multiple_of((p * pps + k) * PAGE_SIZE, PAGE_SIZE)
        for g in range(N_KV):
            x_s[g, pl.ds(r0, PAGE_SIZE), :] = page_refs[k][g].T

    @pl.when(p == pl.num_programs(1) - 1)
    def _():
        o_ref[...] = _compress_rows(x_s, n, w1_ref, w2_ref, ph_ref)


def _compress_sample(cache_t, page_table, w1g, w2bd, ph):
    s, n_pages = page_table.shape
    n = n_pages * PAGE_SIZE // CMP_STRIDE
    pps = math.gcd(n_pages, MAX_PAGES_PER_STEP)

    def page_spec(k):
        return pl.BlockSpec((None, N_KV, LANE, PAGE_SIZE),
                            lambda b, p, pt: (pt[b, p * pps + k], 0, 0, 0))

    const = lambda shape: pl.BlockSpec(shape, lambda b, p, pt: (0, 0))
    return pl.pallas_call(
        functools.partial(_compress_sample_kernel, n=n, pps=pps),
        grid_spec=pltpu.PrefetchScalarGridSpec(
            num_scalar_prefetch=1,
            grid=(s, n_pages // pps),
            in_specs=[page_spec(k) for k in range(pps)]
            + [const(w1g.shape), const(w2bd.shape), const(ph.shape)],
            out_specs=pl.BlockSpec((None, n, KV_COLS), lambda b, p, pt: (b, 0, 0)),
            scratch_shapes=[pltpu.VMEM((N_KV, n_pages * PAGE_SIZE, LANE), F32)]),
        out_shape=jax.ShapeDtypeStruct((s, n, KV_COLS), F32),
        compiler_params=_params("arbitrary", "arbitrary"),
        name="compress_sample",
    )(page_table, *([cache_t] * pps), w1g, w2bd, ph)


def _select_top(sc, top, want_idx):
    rows, nlane = sc.shape
    j = lax.broadcasted_iota(jnp.int32, (rows, nlane), 1).astype(F32)
    mask = jnp.zeros((rows, nlane), F32)
    picks = []
    for _ in range(top):
        m = jnp.max(sc, axis=-1, keepdims=True)
        idx = jnp.min(jnp.where(sc == m, j, float(nlane)), axis=-1, keepdims=True)
        pick = j == idx
        mask = jnp.where(pick, 1.0, mask)
        sc = jnp.where(pick, REMOVED, sc)
        picks.append(idx)
    if want_idx:
        return mask, jnp.concatenate(picks, axis=1).astype(jnp.int32)
    return mask


def _selection_scores(score, blk, qpos, nsel):
    cur = qpos // SEL_BLOCK
    forced = (blk == 0) | (blk == cur) | (blk == cur - 1)
    sc = jnp.where(forced, FORCE_SCORE, jnp.where(blk * SEL_BLOCK <= qpos, score, -1.0))
    return jnp.where(blk < nsel, sc, REMOVED)


def _softmax_rows(s, valid):
    s = jnp.where(valid, s, NEG_INF)
    p = jnp.exp(s - jnp.max(s, axis=-1, keepdims=True))
    return p / jnp.sum(p, axis=-1, keepdims=True)


def _cattn_prompt_kernel(q_ref, ck_ref, bc_ref, covt_ref, oc_ref, sel_ref, *, tq, nc, nsel, top):
    qpos = pl.program_id(1) * tq + lax.broadcasted_iota(jnp.int32, (tq, 1), 0)
    n_idx = lax.broadcasted_iota(jnp.int32, (1, nc), 1)
    valid = (qpos >= n_idx * CMP_STRIDE + (CMP_LEN - 1)) & (n_idx < nc - 1)
    anyv = jnp.max(valid.astype(F32), axis=-1, keepdims=True)
    blk_t = lax.broadcasted_iota(jnp.int32, (nsel, 1), 0)
    qpos_t = pl.program_id(1) * tq + lax.broadcasted_iota(jnp.int32, (1, tq), 1)
    row_t = lax.broadcasted_iota(jnp.int32, (nsel, tq), 0).astype(F32)
    outs, masks = [], []
    for g in range(N_KV):
        kv = ck_ref[:, g * LANE:(g + 1) * LANE].astype(BF16)
        k, v = kv[:, :HEAD_DIM], kv[:, HEAD_DIM:]
        psum = jnp.zeros((tq, nc), F32)
        for r in range(GROUP):
            h = g * GROUP + r
            qh = q_ref[:, h * HEAD_DIM:(h + 1) * HEAD_DIM].astype(BF16)
            p = _softmax_rows(_nt(qh, k) * SCALE + bc_ref[h], valid) * anyv
            outs.append(_mm(p.astype(BF16), v))
            psum = psum + p
        score_t = lax.dot_general(covt_ref[...], psum, (((1,), (1,)), ((), ())),
                                  precision=HIGHEST, preferred_element_type=F32)
        _, rank = _extract_top(_selection_scores(score_t, blk_t, qpos_t, nsel), top, row_t)
        masks.append(jnp.where(rank < top, 1.0, 0.0).T)
    oc_ref[...] = jnp.concatenate(outs, axis=1)
    sel_ref[...] = jnp.concatenate(masks, axis=1)


def _covers(n_cmp_rows, n_cmp, n_sel, n_sel_cols):
    start = np.arange(n_cmp_rows)[:, None] * CMP_STRIDE
    blk = np.arange(n_sel_cols)[None, :]
    cov = ((start < (blk + 1) * SEL_BLOCK) & (start + CMP_LEN > blk * SEL_BLOCK)
           & (np.arange(n_cmp_rows)[:, None] < n_cmp) & (blk < n_sel))
    return jnp.asarray(cov.astype(np.float32))


def _cattn_prompt(q, ck, bc, tq):
    nb, t, _ = q.shape
    nc = ck.shape[1]
    nsel = t // SEL_BLOCK
    top = min(SEL_TOP, nsel)
    cov = _covers(nc, nc - 1, nsel, nsel).T
    return pl.pallas_call(
        functools.partial(_cattn_prompt_kernel, tq=tq, nc=nc, nsel=nsel, top=top),
        grid=(nb, t // tq),
        in_specs=[pl.BlockSpec((None, tq, D_ATT), lambda b, i: (b, i, 0)),
                  pl.BlockSpec((None, nc, KV_COLS), lambda b, i: (b, 0, 0)),
                  pl.BlockSpec((N_HEADS, tq, nc), lambda b, i: (0, i, 0)),
                  pl.BlockSpec(cov.shape, lambda b, i: (0, 0))],
        out_specs=[pl.BlockSpec((None, tq, D_ATT), lambda b, i: (b, i, 0)),
                   pl.BlockSpec((None, tq, N_KV * nsel), lambda b, i: (b, i, 0))],
        out_shape=[jax.ShapeDtypeStruct((nb, t, D_ATT), F32),
                   jax.ShapeDtypeStruct((nb, t, N_KV * nsel), F32)],
        compiler_params=_params("arbitrary", "arbitrary"),
    )(q, ck, bc, cov)


SAMPLES_PER_STEP = 8


def _cattn_sample_kernel(q_ref, ck_ref, bsc_ref, cov_ref, oc_ref, idx_ref, *, past, nc, nsel, top):
    n_idx = lax.broadcasted_iota(jnp.int32, (1, nc), 1)
    valid = (past >= n_idx * CMP_STRIDE + (CMP_LEN - 1)) & (n_idx < nc - 1)
    anyv = jnp.max(valid.astype(F32), axis=-1, keepdims=True)
    nlane = cov_ref.shape[1]
    blk = lax.broadcasted_iota(jnp.int32, (1, nlane), 1)
    idxs = []
    for g in range(N_KV):
        psums = []
        for i in range(SAMPLES_PER_STEP):
            kv = ck_ref[i, :, g * LANE:(g + 1) * LANE].astype(BF16)
            q4 = q_ref[i, g * GROUP:(g + 1) * GROUP, :].astype(BF16)
            s = _nt(q4, kv[:, :HEAD_DIM]) * SCALE + bsc_ref[g * GROUP:(g + 1) * GROUP, :]
            p = _softmax_rows(s, valid) * anyv
            oc_ref[i, g * GROUP:(g + 1) * GROUP, :] = _mm(p.astype(BF16), kv[:, HEAD_DIM:])
            psums.append(jnp.sum(p, axis=0, keepdims=True))
        score = jnp.dot(jnp.concatenate(psums, axis=0), cov_ref[...], precision=HIGHEST,
                        preferred_element_type=F32)
        _, idx = _select_top(_selection_scores(score, blk, past, nsel), top, True)
        idxs.append(idx)
    idx_ref[...] = jnp.concatenate(idxs, axis=1)


def _cattn_sample(q, ck, bsc, past):
    s = q.shape[0]
    nc = ck.shape[1]
    nsel = -(-(past + 1) // SEL_BLOCK)
    top = min(SEL_TOP, nsel)
    nlane = -(-nsel // LANE) * LANE
    cov = _covers(nc, nc - 1, nsel, nlane)
    g = SAMPLES_PER_STEP
    return pl.pallas_call(
        functools.partial(_cattn_sample_kernel, past=past, nc=nc, nsel=nsel, top=top),
        grid=(s // g,),
        in_specs=[pl.BlockSpec((g, N_HEADS, HEAD_DIM), lambda i: (i, 0, 0)),
                  pl.BlockSpec((g, nc, KV_COLS), lambda i: (i, 0, 0)),
                  pl.BlockSpec(bsc.shape, lambda i: (0, 0)),
                  pl.BlockSpec(cov.shape, lambda i: (0, 0))],
        out_specs=[pl.BlockSpec((g, N_HEADS, HEAD_DIM), lambda i: (i, 0, 0)),
                   pl.BlockSpec((g, N_KV * top), lambda i: (i, 0))],
        out_shape=[jax.ShapeDtypeStruct((s, N_HEADS, HEAD_DIM), F32),
                   jax.ShapeDtypeStruct((s, N_KV * top), jnp.int32)],
        compiler_params=_params("arbitrary"),
    )(q.reshape(s, N_HEADS, HEAD_DIM), ck, bsc, cov)


SEL_TILES = 4


def _flash_tiles(qt, kvs, biases, pens, m_ref, acc_ref):
    kv = jnp.concatenate(kvs, axis=0)
    s = _mm(kv.astype(BF16), qt) + jnp.concatenate(biases, axis=0)
    if pens is not None:
        s = s + jnp.concatenate([jnp.concatenate(pens, axis=0)] * GROUP, axis=1)
    m_old = m_ref[...]
    m_new = jnp.maximum(m_old, jnp.max(s, axis=0, keepdims=True))
    row = lax.broadcasted_iota(jnp.int32, (LANE, kv.shape[0]), 0)
    ones_vt = jnp.where(row < HEAD_DIM, 1.0, kv.T).astype(BF16)
    acc_ref[...] = (jnp.exp2(m_old - m_new) * acc_ref[...]
                    + _mm(ones_vt, jnp.exp2(s - m_new).astype(BF16)))
    m_ref[...] = m_new


def _sattn_prompt_kernel(q_ref, sel_ref, kvs_ref, kvw_ref, bt_ref, os_ref, ow_ref,
                         m_s, acc_s, *, nsel, nd, npw):
    c = pl.program_id(1)
    ej = lax.broadcasted_iota(jnp.int32, (LANE, nsel), 1)
    ek = lax.broadcasted_iota(jnp.int32, (LANE, nsel), 0) // SEL_BLOCK
    q_t = (q_ref[...] * (SCALE * LOG2E)).T
    zpad = jnp.zeros((LANE - HEAD_DIM, GROUP * LANE), F32)
    qts = [jnp.concatenate(
        [jnp.concatenate([q_t[(g * GROUP + r) * HEAD_DIM:(g * GROUP + r + 1) * HEAD_DIM]
                          for r in range(GROUP)], axis=1), zpad], axis=0).astype(BF16)
        for g in range(N_KV)]
    selgs = [sel_ref[:, g * nsel:(g + 1) * nsel].astype(BF16) for g in range(N_KV)]

    def reset():
        m_s[...] = jnp.full(m_s.shape, NEG_INF, F32)
        acc_s[...] = jnp.zeros(acc_s.shape, F32)

    def result():
        outs = []
        for g in range(N_KV):
            acc = acc_s[g]
            o = (acc / acc[0:1, :]).T[:, HEAD_DIM:]
            outs += [o[r * LANE:(r + 1) * LANE] for r in range(GROUP)]
        return jnp.concatenate(outs, axis=1)

    def bias_tiles(kind, g, kb):
        dl = jnp.clip(c - kb, 0, nd - 1)
        if kind == 1:
            dl = jnp.where(kb >= 0, dl, 2 * npw - 1)
        return jnp.concatenate([bt_ref[g * GROUP + r, kind, dl] for r in range(GROUP)], axis=1)

    def kv_tile(ref, g, kb):
        k0 = pl.multiple_of(jnp.maximum(kb, 0) * LANE, LANE)
        return ref[pl.ds(k0, LANE), g * LANE:(g + 1) * LANE]

    def update(kind, ref, g, kbs, pens):
        _flash_tiles(qts[g], [kv_tile(ref, g, kb) for kb in kbs],
                     [bias_tiles(kind, g, kb) for kb in kbs], pens, m_s.at[g], acc_s.at[g])

    def sel_step(kp, carry):
        kbs = tuple(SEL_TILES * kp + k for k in range(SEL_TILES))
        for g in range(N_KV):
            pens = []
            for kb in kbs:
                first = jnp.where(kb <= c, (LANE // SEL_BLOCK) * kb, -(LANE // SEL_BLOCK))
                expand = (ej == first + ek).astype(BF16)
                chosen = _nt(expand, selgs[g]) > 0.5
                pens.append(jnp.where(chosen, 0.0, NEG_INF))
            update(0, kvs_ref, g, kbs, pens)
        return carry

    reset()
    lax.fori_loop(0, (c + SEL_TILES) // SEL_TILES, sel_step, 0)
    os_ref[...] = result()

    reset()
    for kp in range(npw):
        kbs = (c - 2 * (npw - kp) + 1, c - 2 * (npw - kp) + 2)
        for g in range(N_KV):
            update(1, kvw_ref, g, kbs, None)
    ow_ref[...] = result()


def _sattn_prompt(q, sel, kvs, kvw, bt):
    nb, t, _ = q.shape
    nsel = t // SEL_BLOCK
    nd = bt.shape[2]
    npw = (WINDOW // LANE + 2) // 2
    qblk = pl.BlockSpec((None, LANE, D_ATT), lambda b, i: (b, i, 0))
    kvblk = pl.BlockSpec((None, t, KV_COLS), lambda b, i: (b, 0, 0))
    return pl.pallas_call(
        functools.partial(_sattn_prompt_kernel, nsel=nsel, nd=nd, npw=npw),
        grid=(nb, t // LANE),
        in_specs=[qblk, pl.BlockSpec((None, LANE, N_KV * nsel), lambda b, i: (b, i, 0)),
                  kvblk, kvblk, pl.BlockSpec(bt.shape, lambda b, i: (0, 0, 0, 0, 0))],
        out_specs=[qblk, qblk],
        out_shape=[jax.ShapeDtypeStruct((nb, t, D_ATT), F32)] * 2,
        scratch_shapes=[pltpu.VMEM((N_KV, 1, GROUP * LANE), F32),
                        pltpu.VMEM((N_KV, LANE, GROUP * LANE), F32)],
        compiler_params=_params("arbitrary", "arbitrary"),
        name="sattn_prompt",
    )(q, sel, kvs, kvw, bt)


def _sattn_sample_kernel(idx_ref, pt_ref, *refs, top, npb, n_pages, wb):
    nblk = N_KV * top
    blk_refs = refs[:nblk]
    (q_ref, ksn_ref, kwn_ref, win_ref, bss_ref, bsw_ref, b0_ref, os_ref, ow_ref) = refs[nblk:]
    b = pl.program_id(0)
    halves = PAGE_SIZE // SEL_BLOCK
    half = lax.broadcasted_iota(jnp.int32, (1, PAGE_SIZE), 1) // SEL_BLOCK
    jw = lax.broadcasted_iota(jnp.int32, (1, wb), 1)
    for g in range(N_KV):
        hs = slice(g * GROUP, (g + 1) * GROUP)
        q4f = q_ref[hs, :]
        q4 = q4f.astype(BF16)
        b0 = b0_ref[hs, :]

        def new_logit(new_ref):
            kn = new_ref[:, g * LANE:g * LANE + HEAD_DIM]
            return jnp.sum(q4f * kn, axis=-1, keepdims=True) * SCALE + b0

        def new_value(new_ref):
            return new_ref[:, g * LANE + HEAD_DIM:(g + 1) * LANE]

        logits, vts = [], []
        for j in range(top):
            ib = idx_ref[b, g * top + j]
            page = jnp.minimum(ib // halves, n_pages - 1)
            s = _mm(q4, blk_refs[g * top + j][0].astype(BF16)) * SCALE + bss_ref[page, hs, :]
            logits.append(jnp.where(half == jnp.where(ib < npb, ib % halves, -1), s, NEG_INF))
            vts.append(blk_refs[g * top + j][1].astype(BF16))
        s_new = new_logit(ksn_ref)
        m = s_new
        for s in logits:
            m = jnp.maximum(m, jnp.max(s, axis=-1, keepdims=True))
        p_new = jnp.exp(s_new - m)
        l = p_new
        acc = p_new * new_value(ksn_ref)
        for s, vt in zip(logits, vts):
            p = jnp.exp(s - m)
            l = l + jnp.sum(p, axis=-1, keepdims=True)
            acc = acc + _nt(p.astype(BF16), vt)
        os_ref[hs, :] = acc / l

        s = _mm(q4, win_ref[g, 0].astype(BF16)) * SCALE + bsw_ref[hs, :]
        s = jnp.where(wb - jw < WINDOW, s, NEG_INF)
        s_new = new_logit(kwn_ref)
        m = jnp.maximum(s_new, jnp.max(s, axis=-1, keepdims=True))
        p = jnp.exp(s - m)
        p_new = jnp.exp(s_new - m)
        l = p_new + jnp.sum(p, axis=-1, keepdims=True)
        acc = p_new * new_value(kwn_ref) + _nt(p.astype(BF16), win_ref[g, 1].astype(BF16))
        ow_ref[hs, :] = acc / l


def _sattn_sample(q, idx, page_table, sel_t, ks_new, kw_new, win_t, bss, bsw, b0, top, npb):
    s, n_pages = page_table.shape
    wb = win_t.shape[-1]
    halves = PAGE_SIZE // SEL_BLOCK

    def blk_spec(g, j):
        def imap(b, idx_r, pt_r):
            ib = idx_r[b, g * top + j]
            return (pt_r[b, jnp.minimum(ib // halves, n_pages - 1)], g, 0, 0, 0)
        return pl.BlockSpec((None, None, 2, HEAD_DIM, PAGE_SIZE), imap)

    per_b = lambda shape: pl.BlockSpec((None,) + shape,
                                       lambda b, i_r, p_r: (b,) + tuple(0 for _ in shape))
    const = lambda shape: pl.BlockSpec(shape, lambda b, i_r, p_r: tuple(0 for _ in shape))
    out = pl.pallas_call(
        functools.partial(_sattn_sample_kernel, top=top, npb=npb, n_pages=n_pages, wb=wb),
        grid_spec=pltpu.PrefetchScalarGridSpec(
            num_scalar_prefetch=2,
            grid=(s,),
            in_specs=[blk_spec(g, j) for g in range(N_KV) for j in range(top)]
            + [per_b((N_HEADS, HEAD_DIM)), per_b((1, KV_COLS)), per_b((1, KV_COLS)),
               per_b(win_t.shape[1:]), const(bss.shape), const(bsw.shape), const(b0.shape)],
            out_specs=[per_b((N_HEADS, HEAD_DIM)), per_b((N_HEADS, HEAD_DIM))]),
        out_shape=[jax.ShapeDtypeStruct((s, N_HEADS, HEAD_DIM), F32)] * 2,
        compiler_params=_params("arbitrary"),
        name="sattn_sample",
    )(idx, page_table, *([sel_t] * (N_KV * top)), q.reshape(s, N_HEADS, HEAD_DIM),
      ks_new.reshape(s, 1, KV_COLS), kw_new.reshape(s, 1, KV_COLS), win_t, bss, bsw, b0)
    return out[0].reshape(s, D_ATT), out[1].reshape(s, D_ATT)


def _outproj_kernel(x_ref, ol_ref, oc_ref, os_ref, ow_ref, gl_ref, gt1_ref, sh2_ref, sc2_ref,
                    gexp_ref, nl_ref, na_ref, n2_ref, wo_ref, x1_ref, h2_ref):
    gates = jax.nn.sigmoid(jnp.dot(gl_ref[...], gexp_ref[...], precision=HIGHEST,
                                   preferred_element_type=F32))
    o_att = (gates[:, :D_ATT] * oc_ref[...] + gates[:, D_ATT:2 * D_ATT] * os_ref[...]
             + gates[:, 2 * D_ATT:] * ow_ref[...])
    n_lru = _rms(ol_ref[...], nl_ref[...]).astype(BF16)
    n_att = _rms(o_att, na_ref[...]).astype(BF16)
    mixed = _mm(n_lru, wo_ref[:D_RNN, :]) + _mm(n_att, wo_ref[D_RNN:, :])
    x1 = x_ref[...] + gt1_ref[...] * mixed
    x1_ref[...] = x1
    h2_ref[...] = (_rms(x1, n2_ref[...]) * (1.0 + sc2_ref[...]) + sh2_ref[...]).astype(BF16)


def _gate_expand():
    e = np.zeros((LANE, 3 * D_ATT), np.float32)
    for br in range(3):
        for h in range(N_HEADS):
            e[br * N_HEADS + h, br * D_ATT + h * HEAD_DIM:br * D_ATT + (h + 1) * HEAD_DIM] = 1.0
    return jnp.asarray(e)


def _outproj(x, o_lru, o_c, o_s, o_w, gl, mod, mod_spec, nl, na, n2, wo, tm):
    n = x.shape[0]
    tok = lambda w: pl.BlockSpec((tm, w), lambda i: (i, 0))
    const = lambda shape: pl.BlockSpec(shape, lambda i: (0, 0))
    gexp = _gate_expand()
    return pl.pallas_call(
        _outproj_kernel,
        grid=(n // tm,),
        in_specs=[tok(D_MODEL), tok(D_RNN), tok(D_ATT), tok(D_ATT), tok(D_ATT), tok(LANE),
                  mod_spec(2), mod_spec(3), mod_spec(4),
                  const(gexp.shape), const(nl.shape), const(na.shape), const(n2.shape),
                  const(wo.shape)],
        out_specs=[tok(D_MODEL), tok(D_MODEL)],
        out_shape=[jax.ShapeDtypeStruct((n, D_MODEL), F32),
                   jax.ShapeDtypeStruct((n, D_MODEL), BF16)],
        compiler_params=_params("arbitrary"),
    )(x, o_lru, o_c, o_s, o_w, gl, mod, mod, mod, gexp, nl, na, n2, wo)


def _extract_top(x, top, row):
    nrow = x.shape[0]
    vals = []
    rank = jnp.full(x.shape, float(top), F32)
    for k in range(top):
        m = jnp.max(x, axis=0, keepdims=True)
        idx = jnp.min(jnp.where(x == m, row, float(nrow)), axis=0, keepdims=True)
        pick = row == idx
        rank = jnp.where(pick, float(k), rank)
        x = jnp.where(pick, REMOVED, x)
        vals.append(m)
    return jnp.concatenate(vals, axis=0), rank


PAIR_COUNTS = tuple(PEER_TOPK // (a + 1) for a in range(PEER_TOPK))
N_PAIRS = sum(PAIR_COUNTS)
PAIR_ROWS = -(-N_PAIRS // 8) * 8


def _peer_score_kernel(h2_ref, wq_ref, bq_ref, sk_ref, bc_ref, a0_ref, r1_ref, e1_ref,
                       q_s, cand_s, *, tn):
    q_s[...] = (_nt(wq_ref[...], h2_ref[...]) + bq_ref[...]).astype(BF16)
    row = lax.broadcasted_iota(jnp.int32, (PEER_KEYS, tn), 0).astype(F32)
    crow = lax.broadcasted_iota(jnp.int32, (PAIR_ROWS, tn), 0).astype(F32)

    def head(h, carry):
        r0 = pl.multiple_of(h * 2 * PEER_KEYS, 2 * PEER_KEYS)
        s0 = _mm(sk_ref[2 * h], q_s[pl.ds(r0, PEER_KEYS), :])
        s1 = _mm(sk_ref[2 * h + 1], q_s[pl.ds(r0 + PEER_KEYS, PEER_KEYS), :])
        v0, rank0 = _extract_top(s0, PEER_TOPK, row)
        v1, rank1 = _extract_top(s1, PEER_TOPK, row)
        off = 0
        for a, cnt in enumerate(PAIR_COUNTS):
            cand_s[off:off + cnt, :] = v1[0:cnt] + v0[a:a + 1]
            off += cnt
        cand_s[N_PAIRS:, :] = jnp.full((PAIR_ROWS - N_PAIRS, tn), REMOVED, F32)
        best, rank_c = _extract_top(cand_s[...], PEER_TOPK, crow)
        z = jnp.sum(jnp.exp(best - best[0:1]), axis=0, keepdims=True)
        cand_s[...] = jnp.where(rank_c < PEER_TOPK, 1.0, 0.0)
        bc = jnp.zeros((PEER_KEYS, tn), F32)
        off = 0
        for a, cnt in enumerate(PAIR_COUNTS):
            kept_a = jnp.sum(cand_s[off:off + cnt, :], axis=0, keepdims=True)
            bc = jnp.where(rank0 == float(a), kept_a, bc)
            off += cnt
        bc_ref[h] = bc
        a0_ref[h] = jnp.where(rank0 < PEER_TOPK, jnp.exp(s0 - v0[0:1]), 0.0) * (0.5 / z)
        r1_ref[h] = rank1.astype(BF16)
        e1_ref[h] = jnp.exp(s1 - v1[0:1]).astype(BF16)
        return carry

    lax.fori_loop(0, PEER_HEADS, head, 0)


def _peer_score(h2, wq_t, bq_t, subkeys, tn):
    n = h2.shape[0]
    dq = wq_t.shape[0]
    tok = pl.BlockSpec((PEER_HEADS, PEER_KEYS, tn), lambda i: (0, 0, i))
    shp = jax.ShapeDtypeStruct((PEER_HEADS, PEER_KEYS, n), F32)
    shp16 = jax.ShapeDtypeStruct((PEER_HEADS, PEER_KEYS, n), BF16)
    return pl.pallas_call(
        functools.partial(_peer_score_kernel, tn=tn),
        grid=(n // tn,),
        in_specs=[pl.BlockSpec((tn, D_MODEL), lambda i: (i, 0)),
                  pl.BlockSpec(wq_t.shape, lambda i: (0, 0)),
                  pl.BlockSpec(bq_t.shape, lambda i: (0, 0)),
                  pl.BlockSpec(subkeys.shape, lambda i: (0, 0, 0))],
        out_specs=[tok, tok, tok, tok],
        out_shape=[shp, shp, shp16, shp16],
        scratch_shapes=[pltpu.VMEM((dq, tn), BF16), pltpu.VMEM((PAIR_ROWS, tn), F32)],
        compiler_params=_params("arbitrary"),
        name="peer_score",
    )(h2, wq_t, bq_t, subkeys)


GELU_C1 = math.sqrt(2.0 / math.pi)
GELU_C2 = 0.044715 * GELU_C1


def _peer_dense_kernel(h2_ref, u_ref, vt_ref, bc_ref, a0_ref, r1_ref, e1_ref, x1_ref,
                       gt2_ref, fg_ref, y_ref, acc_s, act_s, *, tn, eb, lc):
    e = pl.program_id(1)

    @pl.when(e == 0)
    def _():
        acc_s[...] = jnp.zeros(acc_s.shape, F32)

    z = _nt(u_ref[...], h2_ref[...]).astype(BF16)
    for il in range(eb // PEER_KEYS):
        i = e * (eb // PEER_KEYS) + il
        rs = slice(il * PEER_KEYS, (il + 1) * PEER_KEYS)
        for c0 in range(0, tn, lc):
            ls = slice(c0, c0 + lc)
            w = jnp.zeros((PEER_KEYS, lc), BF16)
            for h in range(PEER_HEADS):
                keep = r1_ref[h, :, ls] < bc_ref[h, pl.ds(i, 1), ls].astype(BF16)
                w = w + (jnp.where(keep, e1_ref[h, :, ls], 0.0)
                         * a0_ref[h, pl.ds(i, 1), ls].astype(BF16))
            zi = z[rs, ls]
            t = jnp.tanh(zi * (GELU_C1 + GELU_C2 * (zi * zi)))
            act_s[rs, ls] = (zi + zi * t) * w
    acc_s[...] += _mm(vt_ref[...], act_s[...])

    @pl.when(e == pl.num_programs(1) - 1)
    def _():
        x = x1_ref[...] + gt2_ref[...] * acc_s[...].T
        y_ref[...] = _rms(x, fg_ref[...])


def _peer_dense(h2, u, vt, scores, x1, mod, mod_spec2, fg, tn, eb):
    n = h2.shape[0]
    n_exp = u.shape[0]
    bc, a0, r1, e1 = scores
    tok3 = pl.BlockSpec((PEER_HEADS, PEER_KEYS, tn), lambda i, e: (0, 0, i))
    return pl.pallas_call(
        functools.partial(_peer_dense_kernel, tn=tn, eb=eb, lc=min(tn, 2 * LANE)),
        grid=(n // tn, n_exp // eb),
        in_specs=[pl.BlockSpec((tn, D_MODEL), lambda i, e: (i, 0)),
                  pl.BlockSpec((eb, D_MODEL), lambda i, e: (e, 0)),
                  pl.BlockSpec((D_MODEL, eb), lambda i, e: (0, e)),
                  tok3, tok3, tok3, tok3,
                  pl.BlockSpec((tn, D_MODEL), lambda i, e: (i, 0)),
                  mod_spec2(5),
                  pl.BlockSpec((1, D_MODEL), lambda i, e: (0, 0))],
        out_specs=pl.BlockSpec((tn, D_MODEL), lambda i, e: (i, 0)),
        out_shape=jax.ShapeDtypeStruct((n, D_MODEL), F32),
        scratch_shapes=[pltpu.VMEM((D_MODEL, tn), F32), pltpu.VMEM((eb, tn), BF16)],
        compiler_params=_params("arbitrary", "arbitrary"),
        name="peer_dense",
    )(h2, u, vt, bc, a0, r1, e1, x1, mod, fg)


def _block_diag(w):
    nblk, bw, _ = w.shape
    eye = jnp.eye(nblk, dtype=w.dtype)
    return jnp.einsum('nde,nm->ndme', w, eye).reshape(nblk * bw, nblk * bw)


def kernel(x_prompt, x_sample, cache_cmp_kv, cache_sel_kv, cache_win_kv, state_lru_h, state_conv, page_table, c_prompt, c_sample, ada_w, ada_b, norm1_g, norm2_g, w_in, conv_w, conv_b, lru_wa, lru_ba, lru_wi, lru_bi, lru_lambda, cmp_w1, cmp_w2, cmp_pos, out_norm_lru, out_norm_att, w_out, peer_wq, peer_bq, peer_subkeys, peer_u, peer_v, rel_bias, final_g):
    nb, t, d = x_prompt.shape
    ns = x_sample.shape[0]
    n_pages = page_table.shape[1]
    past = n_pages * PAGE_SIZE
    wb = cache_win_kv.shape[2]
    assert x_sample.shape[1] == 1 and ada_w.shape[0] == 1 and d == D_MODEL
    assert t % (SEL_TILES * LANE) == 0 and ns % LANE == 0 and past % LANE == 0
    assert wb == min(WINDOW, past)
    row = lambda v: v.reshape(1, -1)

    w_in_p = jnp.pad(w_in[0], ((0, 0), (0, sum(IN_SPLITS) - w_in.shape[2]))).astype(BF16)
    wg = jnp.concatenate([_block_diag(lru_wa[0]), _block_diag(lru_wi[0])], axis=1).astype(BF16)
    bg = row(jnp.concatenate([lru_ba[0], lru_bi[0]]))
    w1r = cmp_w1[0].reshape(2, 2, CMP_STRIDE, HEAD_DIM, HEAD_DIM)
    eye2 = jnp.eye(2, dtype=F32)
    w1g = jnp.einsum('cmsdh,ce->scdmeh', w1r, eye2).reshape(CMP_STRIDE * LANE, 2 * LANE).astype(BF16)
    w2bd = jnp.einsum('che,cf->chfe', cmp_w2[0], eye2).reshape(LANE, LANE).astype(BF16)
    wq_t = peer_wq[0].T.astype(BF16)
    bq_t = peer_bq[0].reshape(-1, 1)
    subk = peer_subkeys[0].reshape(2 * PEER_HEADS, PEER_KEYS, -1).astype(BF16)
    u_b = peer_u[0].astype(BF16)
    vt_b = peer_v[0].T.astype(BF16)
    wo_b = w_out[0].astype(BF16)

    n_c = nb + ns
    c_all = jnp.pad(jnp.concatenate([c_prompt, c_sample]), ((0, -n_c % 8), (0, 0)))
    mod = _ada(c_all, ada_w[0], ada_b[0])
    mod_p = mod[:nb].reshape(nb, 1, 6 * D_MODEL)
    mod_s = mod[nb:n_c]

    tm_p, tm_s = 512, ns
    tn_p, tn_s = 512, ns
    tn_sc = 256 if ns % 256 == 0 else LANE

    def mod_spec_p(tm):
        per = t // tm
        return lambda j: pl.BlockSpec((None, 1, D_MODEL), lambda i, *_: (i // per, 0, j))

    def mod_spec_s(tm):
        return lambda j: pl.BlockSpec((tm, D_MODEL), lambda i, *_: (i, j))

    thr_last = BUCKET_THR[-1]
    nd = max(min(t // LANE, -(-(thr_last + LANE - 1) // LANE) + 1), WINDOW // LANE + 2)
    bt, bc = _bias_prompt(rel_bias, t, nd)
    bsc, bsw, bss = _bias_sample(rel_bias, past, wb)
    bss = jnp.transpose(bss, (1, 0, 2))
    b0 = rel_bias[0].reshape(N_HEADS, 1)

    ph = _poshid(cmp_pos[0], cmp_w1[0])
    lru_args = (conv_w[0], row(conv_b[0]), wg, bg, row(lru_lambda[0]))
    norms = (row(out_norm_lru[0]), row(out_norm_att[0]), row(norm2_g[0]), wo_b)

    xp = x_prompt.reshape(nb * t, d)
    xr, yg, q, kvc, kvs, kvw, gl = _inproj(xp, mod_p, mod_spec_p(tm_p), row(norm1_g[0]), w_in_p, tm_p)
    seq = lambda a: a.reshape(nb, t, a.shape[-1])
    o_lru, h_p, conv_p = _lru_prompt(seq(xr), seq(yg), *lru_args, tc=256)
    ck = _compress_prompt(seq(kvc), w1g, w2bd, ph)
    o_c, sel = _cattn_prompt(seq(q), ck, bc, 2 * LANE)
    o_s, o_w = _sattn_prompt(seq(q), sel, seq(kvs), seq(kvw), bt)
    flat = lambda a: a.reshape(nb * t, a.shape[-1])
    x1, h2 = _outproj(xp, flat(o_lru), flat(o_c), flat(o_s), flat(o_w), gl, mod_p,
                      mod_spec_p(tm_p), *norms, tm_p)
    scores = _peer_score(h2, wq_t, bq_t, subk, 256)
    y_p = _peer_dense(h2, u_b, vt_b, scores, x1, mod_p, mod_spec_p(tn_p), row(final_g), tn_p, 2048)

    xs = x_sample.reshape(ns, d)
    xr_s, yg_s, q_s, kvc_s, kvs_s, kvw_s, gl_s = _inproj(xs, mod_s, mod_spec_s(tm_s),
                                                          row(norm1_g[0]), w_in_p, tm_s)
    o_lru_s, h_s = _lru_sample(xr_s, yg_s, state_conv[0], state_lru_h[0], *lru_args)
    rows_minor = lambda a: jnp.transpose(a, (0, 2, 3, 4, 1))
    cmp_t = rows_minor(cache_cmp_kv[0]).reshape(-1, N_KV, LANE, PAGE_SIZE)
    ck_s = _compress_sample(cmp_t, page_table, w1g, w2bd, ph)
    o_c_s, idx = _cattn_sample(q_s, ck_s, bsc, past)
    top_s = idx.shape[1] // N_KV
    o_s_s, o_w_s = _sattn_sample(q_s, idx, page_table, rows_minor(cache_sel_kv[0]), kvs_s, kvw_s,
                                 rows_minor(cache_win_kv[0]), bss, bsw, b0, top_s,
                                 past // SEL_BLOCK)
    x1_s, h2_s = _outproj(xs, o_lru_s, o_c_s.reshape(ns, D_ATT), o_s_s, o_w_s, gl_s, mod_s,
                          mod_spec_s(tm_s), *norms, tm_s)
    scores_s = _peer_score(h2_s, wq_t, bq_t, subk, tn_sc if ns % tn_sc == 0 else ns)
    y_s = _peer_dense(h2_s, u_b, vt_b, scores_s, x1_s, mod_s, mod_spec_s(tn_s), row(final_g), tn_s, 2048)

    kv6 = lambda a, n, tt: a.reshape(1, n, tt, N_KV, 2, HEAD_DIM)
    win_p = kv6(kvw, nb, t)[:, :, t - min(WINDOW, t):]
    win_s = jnp.concatenate([cache_win_kv[:, :, 1:], kv6(kvw_s, ns, 1)], axis=2)[:, :, -wb:]
    conv_s = jnp.concatenate([state_conv[:, :, 1:], xr_s.reshape(1, ns, 1, D_RNN)], axis=2)
    return (y_p.reshape(nb, t, d), y_s.reshape(ns, 1, d),
            kv6(kvc, nb, t), kv6(kvc_s, ns, 1),
            kv6(kvs, nb, t), kv6(kvs_s, ns, 1),
            win_p, win_s,
            h_p.reshape(1, nb, D_RNN), h_s.reshape(1, ns, D_RNN),
            conv_p.reshape(1, nb, CONV_W - 1, D_RNN), conv_s)
```

```python
import functools
import math

import numpy as np
import jax
import jax.numpy as jnp
from jax import lax
from jax.experimental import pallas as pl
from jax.experimental.pallas import tpu as pltpu

F32 = jnp.float32
BF16 = jnp.bfloat16
HIGHEST = lax.Precision.HIGHEST

D_MODEL = 1024
D_RNN = 512
LRU_BLOCKS = 8
CONV_W = 4
LRU_C = 8.0
N_HEADS = 8
HEAD_DIM = 64
N_KV = 2
GROUP = 4
D_ATT = 512
KV_COLS = 2 * N_KV * HEAD_DIM
CMP_LEN = 32
CMP_STRIDE = 16
SEL_BLOCK = 64
SEL_TOP = 16
WINDOW = 512
SCALE = HEAD_DIM ** -0.5
LOG2E = math.log2(math.e)
NEG_INF = -1e30
REMOVED = -3e38
FORCE_SCORE = 1e6
NUM_BUCKETS = 32
MAX_DISTANCE = 1024
PEER_HEADS = 8
PEER_KEYS = 128
PEER_TOPK = 16
NORM_EPS = 1e-6
PAGE_SIZE = 128

LANE = 128
VMEM_LIMIT = 56 * 1024 * 1024


def _bucket_thresholds():
    d = np.arange(0, 1 << 15, dtype=np.int64)
    max_exact = NUM_BUCKETS // 2
    df = np.maximum(d, 1).astype(np.float32)
    ratio = (np.log(df / np.float32(max_exact)) / np.float32(math.log(MAX_DISTANCE / max_exact))
             * np.float32(NUM_BUCKETS - max_exact))
    large = np.minimum(max_exact + ratio.astype(np.int32), NUM_BUCKETS - 1)
    bucket = np.where(d < max_exact, d, large)
    assert np.all(np.diff(bucket) >= 0)
    return tuple(int(np.argmax(bucket >= k)) for k in range(1, NUM_BUCKETS))


BUCKET_THR = _bucket_thresholds()


def _params(*sem):
    return pltpu.CompilerParams(dimension_semantics=sem or None, vmem_limit_bytes=VMEM_LIMIT)


def _nt(a, b):
    return lax.dot_general(a, b, (((1,), (1,)), ((), ())), preferred_element_type=F32)


def _mm(a, b):
    return jnp.dot(a, b, preferred_element_type=F32)


def _rms(x, g):
    return x * lax.rsqrt(jnp.mean(x * x, axis=-1, keepdims=True) + NORM_EPS) * g


def _bias_lookup(dist, rb_ref, h):
    b = jnp.full(dist.shape, rb_ref[0, h], F32)
    for k in range(1, NUM_BUCKETS):
        b = jnp.where(dist >= BUCKET_THR[k - 1], rb_ref[k, h], b)
    return b


def _ada_kernel(c_ref, w_ref, b_ref, o_ref):
    c = c_ref[...]
    o_ref[...] = jnp.dot(jax.nn.silu(c), w_ref[...], precision=HIGHEST,
                         preferred_element_type=F32) + b_ref[...]


def _ada(c_all, w, b):
    rows, d = c_all.shape
    cols = w.shape[1]
    tn = 512
    return pl.pallas_call(
        _ada_kernel,
        grid=(cols // tn,),
        in_specs=[pl.BlockSpec((rows, d), lambda j: (0, 0)),
                  pl.BlockSpec((d, tn), lambda j: (0, j)),
                  pl.BlockSpec((1, tn), lambda j: (0, j))],
        out_specs=pl.BlockSpec((rows, tn), lambda j: (0, j)),
        out_shape=jax.ShapeDtypeStruct((rows, cols), F32),
        compiler_params=_params("arbitrary"),
    )(c_all, w, b.reshape(1, cols))


def _bias_prompt_kernel(rb_ref, bt_ref, bc_ref, *, nd, t, nc):
    h = pl.program_id(0)
    row = lax.broadcasted_iota(jnp.int32, (LANE, LANE), 0)
    col = lax.broadcasted_iota(jnp.int32, (LANE, LANE), 1)
    for dl in range(nd):
        dist = dl * LANE + col - row
        b = _bias_lookup(dist, rb_ref, h) * LOG2E
        bt_ref[0, dl] = jnp.where(dist >= 0, b, NEG_INF)
        bt_ref[1, dl] = jnp.where((dist >= 0) & (dist < WINDOW), b, NEG_INF)
    rq = lax.broadcasted_iota(jnp.int32, (LANE, nc), 0)
    ends = lax.broadcasted_iota(jnp.int32, (LANE, nc), 1) * CMP_STRIDE + (CMP_LEN - 1)

    def chunk(i, carry):
        r0 = pl.multiple_of(i * LANE, LANE)
        bc_ref[pl.ds(r0, LANE), :] = _bias_lookup(r0 + rq - ends, rb_ref, h)
        return carry

    lax.fori_loop(0, t // LANE, chunk, 0)


def _bias_prompt(rel_bias, t, nd):
    nc = t // CMP_STRIDE
    return pl.pallas_call(
        functools.partial(_bias_prompt_kernel, nd=nd, t=t, nc=nc),
        grid=(N_HEADS,),
        in_specs=[pl.BlockSpec(memory_space=pltpu.SMEM)],
        out_specs=[pl.BlockSpec((None, 2, nd, LANE, LANE), lambda h: (h, 0, 0, 0, 0)),
                   pl.BlockSpec((None, t, nc), lambda h: (h, 0, 0))],
        out_shape=[jax.ShapeDtypeStruct((N_HEADS, 2, nd, LANE, LANE), F32),
                   jax.ShapeDtypeStruct((N_HEADS, t, nc), F32)],
        compiler_params=_params("arbitrary"),
    )(rel_bias)


def _bias_sample_kernel(rb_ref, bsc_ref, bsw_ref, bss_ref, *, past, ncs, wb, n_pages):
    ends = lax.broadcasted_iota(jnp.int32, (1, ncs), 1) * CMP_STRIDE + (CMP_LEN - 1)
    jw = lax.broadcasted_iota(jnp.int32, (1, wb), 1)
    kpos = (lax.broadcasted_iota(jnp.int32, (n_pages, PAGE_SIZE), 0) * PAGE_SIZE
            + lax.broadcasted_iota(jnp.int32, (n_pages, PAGE_SIZE), 1))
    for h in range(N_HEADS):
        bsc_ref[h:h + 1, :] = _bias_lookup(past - ends, rb_ref, h)
        bsw_ref[h:h + 1, :] = _bias_lookup(wb - jw, rb_ref, h)
        bss_ref[h] = _bias_lookup(past - kpos, rb_ref, h)


def _bias_sample(rel_bias, past, wb):
    ncs = past // CMP_STRIDE
    n_pages = past // PAGE_SIZE
    return pl.pallas_call(
        functools.partial(_bias_sample_kernel, past=past, ncs=ncs, wb=wb, n_pages=n_pages),
        in_specs=[pl.BlockSpec(memory_space=pltpu.SMEM)],
        out_shape=[jax.ShapeDtypeStruct((N_HEADS, ncs), F32),
                   jax.ShapeDtypeStruct((N_HEADS, wb), F32),
                   jax.ShapeDtypeStruct((N_HEADS, n_pages, PAGE_SIZE), F32)],
        compiler_params=_params(),
        name="bias_sample",
    )(rel_bias)


IN_SPLITS = (D_RNN, D_RNN, D_ATT, KV_COLS, KV_COLS, KV_COLS, LANE)


def _inproj_kernel(x_ref, sh_ref, sc_ref, g_ref, w_ref, *out_refs):
    h = _rms(x_ref[...], g_ref[...]) * (1.0 + sc_ref[...]) + sh_ref[...]
    z = _mm(h.astype(BF16), w_ref[...])
    off = 0
    for o_ref, wdt in zip(out_refs, IN_SPLITS):
        o_ref[...] = z[:, off:off + wdt]
        off += wdt


def _inproj(x, mod, mod_spec, g1, w_in_p, tm):
    n = x.shape[0]
    cols = w_in_p.shape[1]
    return pl.pallas_call(
        _inproj_kernel,
        grid=(n // tm,),
        in_specs=[pl.BlockSpec((tm, D_MODEL), lambda i: (i, 0)),
                  mod_spec(0), mod_spec(1),
                  pl.BlockSpec((1, D_MODEL), lambda i: (0, 0)),
                  pl.BlockSpec((D_MODEL, cols), lambda i: (0, 0))],
        out_specs=[pl.BlockSpec((tm, wdt), lambda i: (i, 0)) for wdt in IN_SPLITS],
        out_shape=[jax.ShapeDtypeStruct((n, wdt), F32) for wdt in IN_SPLITS],
        compiler_params=_params("arbitrary"),
    )(x, mod, mod, g1, w_in_p)


def _lru_gates(xc, wg_ref, bg_ref, lam_ref):
    gates = _mm(xc.astype(BF16), wg_ref[...]) + bg_ref[...]
    r = jax.nn.sigmoid(gates[:, :D_RNN])
    i = jax.nn.sigmoid(gates[:, D_RNN:])
    log_a = -LRU_C * r * jax.nn.softplus(-lam_ref[...])
    a = jnp.exp(log_a)
    u = jnp.sqrt(-jnp.tanh(log_a) * (a * a + 1.0)) * i * xc
    return a, u


def _lru_prompt_kernel(xr_ref, yg_ref, cw_ref, cb_ref, wg_ref, bg_ref, lam_ref,
                       o_ref, hl_ref, cv_ref, xp_s, a_s, u_s, hs_s, h_s, *, nb, tc):
    t = pl.program_id(0)

    @pl.when(t == 0)
    def _():
        xp_s[:, 0:8, :] = jnp.zeros((nb, 8, D_RNN), F32)
        h_s[...] = jnp.zeros((nb, 1, D_RNN), F32)

    @pl.when(t > 0)
    def _():
        xp_s[:, 0:8, :] = xp_s[:, tc:tc + 8, :]

    xp_s[:, 8:, :] = xr_ref[...]
    xc = xp_s[:, 5:5 + tc, :] * cw_ref[0:1, :]
    for k in range(1, CONV_W):
        xc = xc + xp_s[:, 5 + k:5 + k + tc, :] * cw_ref[k:k + 1, :]
    xc = xc + cb_ref[...]
    a, u = _lru_gates(xc.reshape(nb * tc, D_RNN), wg_ref, bg_ref, lam_ref)
    a_s[...] = a.reshape(nb, tc, D_RNN)
    u_s[...] = u.reshape(nb, tc, D_RNN)

    def step(tt, h):
        h = a_s[:, pl.ds(tt, 1), :] * h + u_s[:, pl.ds(tt, 1), :]
        hs_s[:, pl.ds(tt, 1), :] = h
        return h

    h = lax.fori_loop(0, tc, step, h_s[...], unroll=8)
    h_s[...] = h
    o_ref[...] = hs_s[...] * jax.nn.gelu(yg_ref[...])
    hl_ref[...] = h
    cv_ref[...] = xp_s[:, tc + 8 - (CONV_W - 1):tc + 8, :]


def _lru_prompt(xr, yg, cw, cb, wg, bg, lam, tc):
    nb, t, _ = xr.shape
    blk = pl.BlockSpec((nb, tc, D_RNN), lambda i: (0, i, 0))
    full = lambda shape: pl.BlockSpec(shape, lambda i: tuple(0 for _ in shape))
    return pl.pallas_call(
        functools.partial(_lru_prompt_kernel, nb=nb, tc=tc),
        grid=(t // tc,),
        in_specs=[blk, blk, full(cw.shape), full(cb.shape), full(wg.shape), full(bg.shape),
                  full(lam.shape)],
        out_specs=[blk, full((nb, 1, D_RNN)), full((nb, CONV_W - 1, D_RNN))],
        out_shape=[jax.ShapeDtypeStruct((nb, t, D_RNN), F32),
                   jax.ShapeDtypeStruct((nb, 1, D_RNN), F32),
                   jax.ShapeDtypeStruct((nb, CONV_W - 1, D_RNN), F32)],
        scratch_shapes=[pltpu.VMEM((nb, tc + 8, D_RNN), F32), pltpu.VMEM((nb, tc, D_RNN), F32),
                        pltpu.VMEM((nb, tc, D_RNN), F32), pltpu.VMEM((nb, tc, D_RNN), F32),
                        pltpu.VMEM((nb, 1, D_RNN), F32)],
        compiler_params=_params("arbitrary"),
    )(xr, yg, cw, cb, wg, bg, lam)


def _lru_sample_kernel(x_ref, yg_ref, b0_ref, b1_ref, b2_ref, h0_ref, cw_ref, cb_ref, wg_ref,
                       bg_ref, lam_ref, o_ref, hn_ref):
    xc = b0_ref[...] * cw_ref[0:1, :]
    xc = xc + b1_ref[...] * cw_ref[1:2, :]
    xc = xc + b2_ref[...] * cw_ref[2:3, :]
    xc = xc + x_ref[...] * cw_ref[3:4, :]
    xc = xc + cb_ref[...]
    a, u = _lru_gates(xc, wg_ref, bg_ref, lam_ref)
    h = a * h0_ref[...] + u
    hn_ref[...] = h
    o_ref[...] = h * jax.nn.gelu(yg_ref[...])


def _lru_sample(x, yg, buf, h0, cw, cb, wg, bg, lam):
    s = x.shape[0]
    return pl.pallas_call(
        _lru_sample_kernel,
        out_shape=[jax.ShapeDtypeStruct((s, D_RNN), F32), jax.ShapeDtypeStruct((s, D_RNN), F32)],
        compiler_params=_params(),
    )(x, yg, buf[:, 0], buf[:, 1], buf[:, 2], h0, cw, cb, wg, bg, lam)


CHUNK_COLS = CMP_STRIDE * KV_COLS


def _poshid_kernel(pos_ref, w_ref, o_ref):
    parts = [jnp.dot(pos_ref[c], w_ref[c], precision=HIGHEST, preferred_element_type=F32)[0:1]
             for c in range(2)]
    o_ref[...] = jnp.concatenate(parts, axis=1)


def _poshid(cmp_pos, cmp_w1):
    k = CMP_LEN * HEAD_DIM
    pos = jnp.broadcast_to(cmp_pos.reshape(2, 1, k), (2, 8, k))
    return pl.pallas_call(
        _poshid_kernel,
        out_shape=jax.ShapeDtypeStruct((1, 2 * HEAD_DIM), F32),
        compiler_params=_params(),
    )(pos, cmp_w1.reshape(2, k, HEAD_DIM))


def _compress_rows(x_s, n, w1_ref, w2_ref, ph_ref):
    outs = []
    for g in range(N_KV):
        xg = jnp.concatenate([x_s[g, s].astype(BF16) for s in range(CMP_STRIDE)],
                             axis=1)
        p = _mm(xg, w1_ref[...])
        hid = ph_ref[...] + p[:, :LANE] + pltpu.roll(p[:, LANE:], n - 1, axis=0)
        outs.append(_mm(jax.nn.gelu(hid).astype(BF16), w2_ref[...]))
    return jnp.concatenate(outs, axis=1)


def _compress_prompt_kernel(x_ref, w1_ref, w2_ref, ph_ref, o_ref, rows_s, x_s, *, n):
    for g in range(N_KV):
        rows_s[g] = x_ref[:, g * LANE:(g + 1) * LANE]
        for s in range(CMP_STRIDE):
            x_s[g, s] = rows_s[g, pl.ds(s, n, stride=CMP_STRIDE), :]
    o_ref[...] = _compress_rows(x_s, n, w1_ref, w2_ref, ph_ref)


def _compress_prompt(kvc, w1g, w2bd, ph):
    nb, t, _ = kvc.shape
    n = t // CMP_STRIDE
    return pl.pallas_call(
        functools.partial(_compress_prompt_kernel, n=n),
        grid=(nb,),
        in_specs=[pl.BlockSpec((None, t, KV_COLS), lambda b: (b, 0, 0)),
                  pl.BlockSpec(w1g.shape, lambda b: (0, 0)),
                  pl.BlockSpec(w2bd.shape, lambda b: (0, 0)),
                  pl.BlockSpec(ph.shape, lambda b: (0, 0))],
        out_specs=pl.BlockSpec((None, n, KV_COLS), lambda b: (b, 0, 0)),
        out_shape=jax.ShapeDtypeStruct((nb, n, KV_COLS), F32),
        scratch_shapes=[pltpu.VMEM((N_KV, t, LANE), F32),
                        pltpu.VMEM((N_KV, CMP_STRIDE, n, LANE), F32)],
        compiler_params=_params("arbitrary"),
        name="compress_prompt",
    )(kvc, w1g, w2bd, ph)


MAX_PAGES_PER_STEP = 32


CHUNKS_PER_PAGE = PAGE_SIZE // CMP_STRIDE


def _page_row_permutation():
    perm = np.zeros((PAGE_SIZE, PAGE_SIZE), np.float32)
    for s in range(CMP_STRIDE):
        for c in range(CHUNKS_PER_PAGE):
            perm[s * CHUNKS_PER_PAGE + c, CMP_STRIDE * c + s] = 1.0
    return jnp.asarray(perm, BF16)


def _compress_sample_kernel(pt_ref, *refs, n, pps):
    page_refs = refs[:pps]
    perm_ref, w1_ref, w2_ref, ph_ref, o_ref, x_s = refs[pps:]
    p = pl.program_id(1)
    for k in range(pps):
        c0 = pl.multiple_of((p * pps + k) * CHUNKS_PER_PAGE, CHUNKS_PER_PAGE)
        for g in range(N_KV):
            rows = _nt(perm_ref[...], page_refs[k][g].astype(BF16))
            for s in range(CMP_STRIDE):
                x_s[g, s, pl.ds(c0, CHUNKS_PER_PAGE), :] = (
                    rows[s * CHUNKS_PER_PAGE:(s + 1) * CHUNKS_PER_PAGE])

    @pl.when(p == pl.num_programs(1) - 1)
    def _():
        o_ref[...] = _compress_rows(x_s, n, w1_ref, w2_ref, ph_ref)


def _compress_sample(cache_t, page_table, w1g, w2bd, ph):
    s, n_pages = page_table.shape
    n = n_pages * PAGE_SIZE // CMP_STRIDE
    pps = math.gcd(n_pages, MAX_PAGES_PER_STEP)

    def page_spec(k):
        return pl.BlockSpec((None, N_KV, LANE, PAGE_SIZE),
                            lambda b, p, pt: (pt[b, p * pps + k], 0, 0, 0))

    const = lambda shape: pl.BlockSpec(shape, lambda b, p, pt: (0, 0))
    perm = _page_row_permutation()
    return pl.pallas_call(
        functools.partial(_compress_sample_kernel, n=n, pps=pps),
        grid_spec=pltpu.PrefetchScalarGridSpec(
            num_scalar_prefetch=1,
            grid=(s, n_pages // pps),
            in_specs=[page_spec(k) for k in range(pps)]
            + [const(perm.shape), const(w1g.shape), const(w2bd.shape), const(ph.shape)],
            out_specs=pl.BlockSpec((None, n, KV_COLS), lambda b, p, pt: (b, 0, 0)),
            scratch_shapes=[pltpu.VMEM((N_KV, CMP_STRIDE, n, LANE), F32)]),
        out_shape=jax.ShapeDtypeStruct((s, n, KV_COLS), F32),
        compiler_params=_params("arbitrary", "arbitrary"),
        name="compress_sample",
    )(page_table, *([cache_t] * pps), perm, w1g, w2bd, ph)


def _select_top(sc, top, want_idx):
    rows, nlane = sc.shape
    j = lax.broadcasted_iota(jnp.int32, (rows, nlane), 1).astype(F32)
    mask = jnp.zeros((rows, nlane), F32)
    picks = []
    for _ in range(top):
        m = jnp.max(sc, axis=-1, keepdims=True)
        idx = jnp.min(jnp.where(sc == m, j, float(nlane)), axis=-1, keepdims=True)
        pick = j == idx
        mask = jnp.where(pick, 1.0, mask)
        sc = jnp.where(pick, REMOVED, sc)
        picks.append(idx)
    if want_idx:
        return mask, jnp.concatenate(picks, axis=1).astype(jnp.int32)
    return mask


def _selection_scores(score, blk, qpos, nsel):
    cur = qpos // SEL_BLOCK
    forced = (blk == 0) | (blk == cur) | (blk == cur - 1)
    sc = jnp.where(forced, FORCE_SCORE, jnp.where(blk * SEL_BLOCK <= qpos, score, -1.0))
    return jnp.where(blk < nsel, sc, REMOVED)


def _softmax_rows(s, valid):
    s = jnp.where(valid, s, NEG_INF)
    p = jnp.exp(s - jnp.max(s, axis=-1, keepdims=True))
    return p / jnp.sum(p, axis=-1, keepdims=True)


def _cattn_prompt_kernel(q_ref, ck_ref, bc_ref, covt_ref, oc_ref, sel_ref, *, tq, nc, nsel, top):
    qpos = pl.program_id(1) * tq + lax.broadcasted_iota(jnp.int32, (tq, 1), 0)
    n_idx = lax.broadcasted_iota(jnp.int32, (1, nc), 1)
    valid = (qpos >= n_idx * CMP_STRIDE + (CMP_LEN - 1)) & (n_idx < nc - 1)
    anyv = jnp.max(valid.astype(F32), axis=-1, keepdims=True)
    blk_t = lax.broadcasted_iota(jnp.int32, (nsel, 1), 0)
    qpos_t = pl.program_id(1) * tq + lax.broadcasted_iota(jnp.int32, (1, tq), 1)
    row_t = lax.broadcasted_iota(jnp.int32, (nsel, tq), 0).astype(F32)
    outs, masks = [], []
    for g in range(N_KV):
        kv = ck_ref[:, g * LANE:(g + 1) * LANE].astype(BF16)
        k, v = kv[:, :HEAD_DIM], kv[:, HEAD_DIM:]
        psum = jnp.zeros((tq, nc), F32)
        for r in range(GROUP):
            h = g * GROUP + r
            qh = q_ref[:, h * HEAD_DIM:(h + 1) * HEAD_DIM].astype(BF16)
            p = _softmax_rows(_nt(qh, k) * SCALE + bc_ref[h], valid) * anyv
            outs.append(_mm(p.astype(BF16), v))
            psum = psum + p
        score_t = lax.dot_general(covt_ref[...], psum, (((1,), (1,)), ((), ())),
                                  precision=HIGHEST, preferred_element_type=F32)
        _, rank = _extract_top(_selection_scores(score_t, blk_t, qpos_t, nsel), top, row_t)
        masks.append(jnp.where(rank < top, 1.0, 0.0).T)
    oc_ref[...] = jnp.concatenate(outs, axis=1)
    sel_ref[...] = jnp.concatenate(masks, axis=1)


def _covers(n_cmp_rows, n_cmp, n_sel, n_sel_cols):
    start = np.arange(n_cmp_rows)[:, None] * CMP_STRIDE
    blk = np.arange(n_sel_cols)[None, :]
    cov = ((start < (blk + 1) * SEL_BLOCK) & (start + CMP_LEN > blk * SEL_BLOCK)
           & (np.arange(n_cmp_rows)[:, None] < n_cmp) & (blk < n_sel))
    return jnp.asarray(cov.astype(np.float32))


def _cattn_prompt(q, ck, bc, tq):
    nb, t, _ = q.shape
    nc = ck.shape[1]
    nsel = t // SEL_BLOCK
    top = min(SEL_TOP, nsel)
    cov = _covers(nc, nc - 1, nsel, nsel).T
    return pl.pallas_call(
        functools.partial(_cattn_prompt_kernel, tq=tq, nc=nc, nsel=nsel, top=top),
        grid=(nb, t // tq),
        in_specs=[pl.BlockSpec((None, tq, D_ATT), lambda b, i: (b, i, 0)),
                  pl.BlockSpec((None, nc, KV_COLS), lambda b, i: (b, 0, 0)),
                  pl.BlockSpec((N_HEADS, tq, nc), lambda b, i: (0, i, 0)),
                  pl.BlockSpec(cov.shape, lambda b, i: (0, 0))],
        out_specs=[pl.BlockSpec((None, tq, D_ATT), lambda b, i: (b, i, 0)),
                   pl.BlockSpec((None, tq, N_KV * nsel), lambda b, i: (b, i, 0))],
        out_shape=[jax.ShapeDtypeStruct((nb, t, D_ATT), F32),
                   jax.ShapeDtypeStruct((nb, t, N_KV * nsel), F32)],
        compiler_params=_params("arbitrary", "arbitrary"),
    )(q, ck, bc, cov)


SAMPLES_PER_STEP = 8


def _cattn_sample_kernel(q_ref, ck_ref, bsc_ref, cov_ref, oc_ref, idx_ref, *, past, nc, nsel, top):
    n_idx = lax.broadcasted_iota(jnp.int32, (1, nc), 1)
    valid = (past >= n_idx * CMP_STRIDE + (CMP_LEN - 1)) & (n_idx < nc - 1)
    anyv = jnp.max(valid.astype(F32), axis=-1, keepdims=True)
    nlane = cov_ref.shape[1]
    blk = lax.broadcasted_iota(jnp.int32, (1, nlane), 1)
    idxs = []
    for g in range(N_KV):
        psums = []
        for i in range(SAMPLES_PER_STEP):
            kv = ck_ref[i, :, g * LANE:(g + 1) * LANE].astype(BF16)
            q4 = q_ref[i, g * GROUP:(g + 1) * GROUP, :].astype(BF16)
            s = _nt(q4, kv[:, :HEAD_DIM]) * SCALE + bsc_ref[g * GROUP:(g + 1) * GROUP, :]
            p = _softmax_rows(s, valid) * anyv
            oc_ref[i, g * GROUP:(g + 1) * GROUP, :] = _mm(p.astype(BF16), kv[:, HEAD_DIM:])
            psums.append(jnp.sum(p, axis=0, keepdims=True))
        score = jnp.dot(jnp.concatenate(psums, axis=0), cov_ref[...], precision=HIGHEST,
                        preferred_element_type=F32)
        _, idx = _select_top(_selection_scores(score, blk, past, nsel), top, True)
        idxs.append(idx)
    idx_ref[...] = jnp.concatenate(idxs, axis=1)


def _cattn_sample(q, ck, bsc, past):
    s = q.shape[0]
    nc = ck.shape[1]
    nsel = -(-(past + 1) // SEL_BLOCK)
    top = min(SEL_TOP, nsel)
    nlane = -(-nsel // LANE) * LANE
    cov = _covers(nc, nc - 1, nsel, nlane)
    g = SAMPLES_PER_STEP
    return pl.pallas_call(
        functools.partial(_cattn_sample_kernel, past=past, nc=nc, nsel=nsel, top=top),
        grid=(s // g,),
        in_specs=[pl.BlockSpec((g, N_HEADS, HEAD_DIM), lambda i: (i, 0, 0)),
                  pl.BlockSpec((g, nc, KV_COLS), lambda i: (i, 0, 0)),
                  pl.BlockSpec(bsc.shape, lambda i: (0, 0)),
                  pl.BlockSpec(cov.shape, lambda i: (0, 0))],
        out_specs=[pl.BlockSpec((g, N_HEADS, HEAD_DIM), lambda i: (i, 0, 0)),
                   pl.BlockSpec((g, N_KV * top), lambda i: (i, 0))],
        out_shape=[jax.ShapeDtypeStruct((s, N_HEADS, HEAD_DIM), F32),
                   jax.ShapeDtypeStruct((s, N_KV * top), jnp.int32)],
        compiler_params=_params("arbitrary"),
    )(q.reshape(s, N_HEADS, HEAD_DIM), ck, bsc, cov)


SEL_TILES = 8


def _flash_tiles(qt, kvs, biases, pens, m_ref, acc_ref):
    kv = jnp.concatenate(kvs, axis=0)
    s = _mm(kv.astype(BF16), qt) + jnp.concatenate(biases, axis=0)
    if pens is not None:
        s = s + jnp.concatenate([jnp.concatenate(pens, axis=0)] * GROUP, axis=1)
    m_old = m_ref[...]
    m_new = jnp.maximum(m_old, jnp.max(s, axis=0, keepdims=True))
    row = lax.broadcasted_iota(jnp.int32, (LANE, kv.shape[0]), 0)
    ones_vt = jnp.where(row < HEAD_DIM, 1.0, kv.T).astype(BF16)
    acc_ref[...] = (jnp.exp2(m_old - m_new) * acc_ref[...]
                    + _mm(ones_vt, jnp.exp2(s - m_new).astype(BF16)))
    m_ref[...] = m_new


def _sattn_prompt_kernel(q_ref, sel_ref, kvs_ref, kvw_ref, bt_ref, os_ref, ow_ref,
                         m_s, acc_s, *, nsel, nd, npw):
    c = pl.program_id(1)
    ej = lax.broadcasted_iota(jnp.int32, (LANE, nsel), 1)
    ek = lax.broadcasted_iota(jnp.int32, (LANE, nsel), 0) // SEL_BLOCK
    q_t = (q_ref[...] * (SCALE * LOG2E)).T
    zpad = jnp.zeros((LANE - HEAD_DIM, GROUP * LANE), F32)
    qts = [jnp.concatenate(
        [jnp.concatenate([q_t[(g * GROUP + r) * HEAD_DIM:(g * GROUP + r + 1) * HEAD_DIM]
                          for r in range(GROUP)], axis=1), zpad], axis=0).astype(BF16)
        for g in range(N_KV)]
    selgs = [sel_ref[:, g * nsel:(g + 1) * nsel].astype(BF16) for g in range(N_KV)]

    def reset():
        m_s[...] = jnp.full(m_s.shape, NEG_INF, F32)
        acc_s[...] = jnp.zeros(acc_s.shape, F32)

    def result():
        outs = []
        for g in range(N_KV):
            acc = acc_s[g]
            o = (acc / acc[0:1, :]).T[:, HEAD_DIM:]
            outs += [o[r * LANE:(r + 1) * LANE] for r in range(GROUP)]
        return jnp.concatenate(outs, axis=1)

    def bias_tiles(kind, g, kb):
        dl = jnp.clip(c - kb, 0, nd - 1)
        if kind == 1:
            dl = jnp.where(kb >= 0, dl, 2 * npw - 1)
        return jnp.concatenate([bt_ref[g * GROUP + r, kind, dl] for r in range(GROUP)], axis=1)

    def kv_tile(ref, g, kb):
        k0 = pl.multiple_of(jnp.maximum(kb, 0) * LANE, LANE)
        return ref[pl.ds(k0, LANE), g * LANE:(g + 1) * LANE]

    def update(kind, ref, g, kbs, pens):
        _flash_tiles(qts[g], [kv_tile(ref, g, kb) for kb in kbs],
                     [bias_tiles(kind, g, kb) for kb in kbs], pens, m_s.at[g], acc_s.at[g])

    def sel_step(kp, carry):
        kbs = tuple(SEL_TILES * kp + k for k in range(SEL_TILES))
        for g in range(N_KV):
            pens = []
            for kb in kbs:
                first = jnp.where(kb <= c, (LANE // SEL_BLOCK) * kb, -(LANE // SEL_BLOCK))
                expand = (ej == first + ek).astype(BF16)
                chosen = _nt(expand, selgs[g]) > 0.5
                pens.append(jnp.where(chosen, 0.0, NEG_INF))
            update(0, kvs_ref, g, kbs, pens)
        return carry

    reset()
    lax.fori_loop(0, (c + SEL_TILES) // SEL_TILES, sel_step, 0)
    os_ref[...] = result()

    reset()
    kbs = tuple(c - (2 * npw - 1) + k for k in range(2 * npw))
    for g in range(N_KV):
        update(1, kvw_ref, g, kbs, None)
    ow_ref[...] = result()


def _sattn_prompt(q, sel, kvs, kvw, bt):
    nb, t, _ = q.shape
    nsel = t // SEL_BLOCK
    nd = bt.shape[2]
    npw = (WINDOW // LANE + 2) // 2
    qblk = pl.BlockSpec((None, LANE, D_ATT), lambda b, i: (b, i, 0))
    kvblk = pl.BlockSpec((None, t, KV_COLS), lambda b, i: (b, 0, 0))
    return pl.pallas_call(
        functools.partial(_sattn_prompt_kernel, nsel=nsel, nd=nd, npw=npw),
        grid=(nb, t // LANE),
        in_specs=[qblk, pl.BlockSpec((None, LANE, N_KV * nsel), lambda b, i: (b, i, 0)),
                  kvblk, kvblk, pl.BlockSpec(bt.shape, lambda b, i: (0, 0, 0, 0, 0))],
        out_specs=[qblk, qblk],
        out_shape=[jax.ShapeDtypeStruct((nb, t, D_ATT), F32)] * 2,
        scratch_shapes=[pltpu.VMEM((N_KV, 1, GROUP * LANE), F32),
                        pltpu.VMEM((N_KV, LANE, GROUP * LANE), F32)],
        compiler_params=_params("arbitrary", "arbitrary"),
        name="sattn_prompt",
    )(q, sel, kvs, kvw, bt)


def _sattn_sample_kernel(idx_ref, pt_ref, *refs, top, npb, n_pages, wb):
    nblk = N_KV * top
    blk_refs = refs[:nblk]
    (q_ref, ksn_ref, kwn_ref, win_ref, bss_ref, bsw_ref, b0_ref, os_ref, ow_ref) = refs[nblk:]
    b = pl.program_id(0)
    halves = PAGE_SIZE // SEL_BLOCK
    half = lax.broadcasted_iota(jnp.int32, (1, PAGE_SIZE), 1) // SEL_BLOCK
    jw = lax.broadcasted_iota(jnp.int32, (1, wb), 1)
    for g in range(N_KV):
        hs = slice(g * GROUP, (g + 1) * GROUP)
        q4f = q_ref[hs, :]
        q4 = q4f.astype(BF16)
        b0 = b0_ref[hs, :]

        def new_logit(new_ref):
            kn = new_ref[:, g * LANE:g * LANE + HEAD_DIM]
            return jnp.sum(q4f * kn, axis=-1, keepdims=True) * SCALE + b0

        def new_value(new_ref):
            return new_ref[:, g * LANE + HEAD_DIM:(g + 1) * LANE]

        logits, vts = [], []
        for j in range(top):
            ib = idx_ref[b, g * top + j]
            page = jnp.minimum(ib // halves, n_pages - 1)
            s = _mm(q4, blk_refs[g * top + j][0].astype(BF16)) * SCALE + bss_ref[page, hs, :]
            logits.append(jnp.where(half == jnp.where(ib < npb, ib % halves, -1), s, NEG_INF))
            vts.append(blk_refs[g * top + j][1].astype(BF16))
        s_new = new_logit(ksn_ref)
        m = s_new
        for s in logits:
            m = jnp.maximum(m, jnp.max(s, axis=-1, keepdims=True))
        p_new = jnp.exp(s_new - m)
        l = p_new
        acc = p_new * new_value(ksn_ref)
        for s, vt in zip(logits, vts):
            p = jnp.exp(s - m)
            l = l + jnp.sum(p, axis=-1, keepdims=True)
            acc = acc + _nt(p.astype(BF16), vt)
        os_ref[hs, :] = acc / l

        s = _mm(q4, win_ref[g, 0].astype(BF16)) * SCALE + bsw_ref[hs, :]
        s = jnp.where(wb - jw < WINDOW, s, NEG_INF)
        s_new = new_logit(kwn_ref)
        m = jnp.maximum(s_new, jnp.max(s, axis=-1, keepdims=True))
        p = jnp.exp(s - m)
        p_new = jnp.exp(s_new - m)
        l = p_new + jnp.sum(p, axis=-1, keepdims=True)
        acc = p_new * new_value(kwn_ref) + _nt(p.astype(BF16), win_ref[g, 1].astype(BF16))
        ow_ref[hs, :] = acc / l


def _sattn_sample(q, idx, page_table, sel_t, ks_new, kw_new, win_t, bss, bsw, b0, top, npb):
    s, n_pages = page_table.shape
    wb = win_t.shape[-1]
    halves = PAGE_SIZE // SEL_BLOCK

    def blk_spec(g, j):
        def imap(b, idx_r, pt_r):
            ib = idx_r[b, g * top + j]
            return (pt_r[b, jnp.minimum(ib // halves, n_pages - 1)], g, 0, 0, 0)
        return pl.BlockSpec((None, None, 2, HEAD_DIM, PAGE_SIZE), imap)

    per_b = lambda shape: pl.BlockSpec((None,) + shape,
                                       lambda b, i_r, p_r: (b,) + tuple(0 for _ in shape))
    const = lambda shape: pl.BlockSpec(shape, lambda b, i_r, p_r: tuple(0 for _ in shape))
    out = pl.pallas_call(
        functools.partial(_sattn_sample_kernel, top=top, npb=npb, n_pages=n_pages, wb=wb),
        grid_spec=pltpu.PrefetchScalarGridSpec(
            num_scalar_prefetch=2,
            grid=(s,),
            in_specs=[blk_spec(g, j) for g in range(N_KV) for j in range(top)]
            + [per_b((N_HEADS, HEAD_DIM)), per_b((1, KV_COLS)), per_b((1, KV_COLS)),
               per_b(win_t.shape[1:]), const(bss.shape), const(bsw.shape), const(b0.shape)],
            out_specs=[per_b((N_HEADS, HEAD_DIM)), per_b((N_HEADS, HEAD_DIM))]),
        out_shape=[jax.ShapeDtypeStruct((s, N_HEADS, HEAD_DIM), F32)] * 2,
        compiler_params=_params("arbitrary"),
        name="sattn_sample",
    )(idx, page_table, *([sel_t] * (N_KV * top)), q.reshape(s, N_HEADS, HEAD_DIM),
      ks_new.reshape(s, 1, KV_COLS), kw_new.reshape(s, 1, KV_COLS), win_t, bss, bsw, b0)
    return out[0].reshape(s, D_ATT), out[1].reshape(s, D_ATT)


def _outproj_kernel(x_ref, ol_ref, oc_ref, os_ref, ow_ref, gl_ref, gt1_ref, sh2_ref, sc2_ref,
                    gexp_ref, nl_ref, na_ref, n2_ref, wo_ref, x1_ref, h2_ref):
    gates = jax.nn.sigmoid(jnp.dot(gl_ref[...], gexp_ref[...], precision=HIGHEST,
                                   preferred_element_type=F32))
    o_att = (gates[:, :D_ATT] * oc_ref[...] + gates[:, D_ATT:2 * D_ATT] * os_ref[...]
             + gates[:, 2 * D_ATT:] * ow_ref[...])
    n_lru = _rms(ol_ref[...], nl_ref[...]).astype(BF16)
    n_att = _rms(o_att, na_ref[...]).astype(BF16)
    mixed = _mm(n_lru, wo_ref[:D_RNN, :]) + _mm(n_att, wo_ref[D_RNN:, :])
    x1 = x_ref[...] + gt1_ref[...] * mixed
    x1_ref[...] = x1
    h2_ref[...] = (_rms(x1, n2_ref[...]) * (1.0 + sc2_ref[...]) + sh2_ref[...]).astype(BF16)


def _gate_expand():
    e = np.zeros((LANE, 3 * D_ATT), np.float32)
    for br in range(3):
        for h in range(N_HEADS):
            e[br * N_HEADS + h, br * D_ATT + h * HEAD_DIM:br * D_ATT + (h + 1) * HEAD_DIM] = 1.0
    return jnp.asarray(e)


def _outproj(x, o_lru, o_c, o_s, o_w, gl, mod, mod_spec, nl, na, n2, wo, tm):
    n = x.shape[0]
    tok = lambda w: pl.BlockSpec((tm, w), lambda i: (i, 0))
    const = lambda shape: pl.BlockSpec(shape, lambda i: (0, 0))
    gexp = _gate_expand()
    return pl.pallas_call(
        _outproj_kernel,
        grid=(n // tm,),
        in_specs=[tok(D_MODEL), tok(D_RNN), tok(D_ATT), tok(D_ATT), tok(D_ATT), tok(LANE),
                  mod_spec(2), mod_spec(3), mod_spec(4),
                  const(gexp.shape), const(nl.shape), const(na.shape), const(n2.shape),
                  const(wo.shape)],
        out_specs=[tok(D_MODEL), tok(D_MODEL)],
        out_shape=[jax.ShapeDtypeStruct((n, D_MODEL), F32),
                   jax.ShapeDtypeStruct((n, D_MODEL), BF16)],
        compiler_params=_params("arbitrary"),
    )(x, o_lru, o_c, o_s, o_w, gl, mod, mod, mod, gexp, nl, na, n2, wo)


def _extract_top(x, top, row):
    nrow = x.shape[0]
    vals = []
    rank = jnp.full(x.shape, float(top), F32)
    for k in range(top):
        m = jnp.max(x, axis=0, keepdims=True)
        idx = jnp.min(jnp.where(x == m, row, float(nrow)), axis=0, keepdims=True)
        pick = row == idx
        rank = jnp.where(pick, float(k), rank)
        x = jnp.where(pick, REMOVED, x)
        vals.append(m)
    return jnp.concatenate(vals, axis=0), rank


PAIR_COUNTS = tuple(PEER_TOPK // (a + 1) for a in range(PEER_TOPK))
N_PAIRS = sum(PAIR_COUNTS)
PAIR_ROWS = -(-N_PAIRS // 8) * 8


def _peer_score_kernel(h2_ref, wq_ref, bq_ref, sk_ref, bc_ref, a0_ref, r1_ref, e1_ref,
                       q_s, cand_s, *, tn):
    q_s[...] = (_nt(wq_ref[...], h2_ref[...]) + bq_ref[...]).astype(BF16)
    row = lax.broadcasted_iota(jnp.int32, (PEER_KEYS, tn), 0).astype(F32)
    crow = lax.broadcasted_iota(jnp.int32, (PAIR_ROWS, tn), 0).astype(F32)

    def head(h, carry):
        r0 = pl.multiple_of(h * 2 * PEER_KEYS, 2 * PEER_KEYS)
        s0 = _mm(sk_ref[2 * h], q_s[pl.ds(r0, PEER_KEYS), :])
        s1 = _mm(sk_ref[2 * h + 1], q_s[pl.ds(r0 + PEER_KEYS, PEER_KEYS), :])
        v0, rank0 = _extract_top(s0, PEER_TOPK, row)
        v1, rank1 = _extract_top(s1, PEER_TOPK, row)
        off = 0
        for a, cnt in enumerate(PAIR_COUNTS):
            cand_s[off:off + cnt, :] = v1[0:cnt] + v0[a:a + 1]
            off += cnt
        cand_s[N_PAIRS:, :] = jnp.full((PAIR_ROWS - N_PAIRS, tn), REMOVED, F32)
        best, rank_c = _extract_top(cand_s[...], PEER_TOPK, crow)
        z = jnp.sum(jnp.exp(best - best[0:1]), axis=0, keepdims=True)
        cand_s[...] = jnp.where(rank_c < PEER_TOPK, 1.0, 0.0)
        bc = jnp.zeros((PEER_KEYS, tn), F32)
        off = 0
        for a, cnt in enumerate(PAIR_COUNTS):
            kept_a = jnp.sum(cand_s[off:off + cnt, :], axis=0, keepdims=True)
            bc = jnp.where(rank0 == float(a), kept_a, bc)
            off += cnt
        bc_ref[h] = bc
        a0_ref[h] = jnp.where(rank0 < PEER_TOPK, jnp.exp(s0 - v0[0:1]), 0.0) * (0.5 / z)
        r1_ref[h] = rank1.astype(BF16)
        e1_ref[h] = jnp.exp(s1 - v1[0:1]).astype(BF16)
        return carry

    lax.fori_loop(0, PEER_HEADS, head, 0)


def _peer_score(h2, wq_t, bq_t, subkeys, tn):
    n = h2.shape[0]
    dq = wq_t.shape[0]
    tok = pl.BlockSpec((PEER_HEADS, PEER_KEYS, tn), lambda i: (0, 0, i))
    shp = jax.ShapeDtypeStruct((PEER_HEADS, PEER_KEYS, n), F32)
    shp16 = jax.ShapeDtypeStruct((PEER_HEADS, PEER_KEYS, n), BF16)
    return pl.pallas_call(
        functools.partial(_peer_score_kernel, tn=tn),
        grid=(n // tn,),
        in_specs=[pl.BlockSpec((tn, D_MODEL), lambda i: (i, 0)),
                  pl.BlockSpec(wq_t.shape, lambda i: (0, 0)),
                  pl.BlockSpec(bq_t.shape, lambda i: (0, 0)),
                  pl.BlockSpec(subkeys.shape, lambda i: (0, 0, 0))],
        out_specs=[tok, tok, tok, tok],
        out_shape=[shp, shp, shp16, shp16],
        scratch_shapes=[pltpu.VMEM((dq, tn), BF16), pltpu.VMEM((PAIR_ROWS, tn), F32)],
        compiler_params=_params("arbitrary"),
        name="peer_score",
    )(h2, wq_t, bq_t, subkeys)


GELU_C1 = math.sqrt(2.0 / math.pi)
GELU_C2 = 0.044715 * GELU_C1


def _peer_dense_kernel(h2_ref, u_ref, vt_ref, bc_ref, a0_ref, r1_ref, e1_ref, x1_ref,
                       gt2_ref, fg_ref, y_ref, acc_s, act_s, *, tn, eb, lc):
    e = pl.program_id(1)

    @pl.when(e == 0)
    def _():
        acc_s[...] = jnp.zeros(acc_s.shape, F32)

    z = _nt(u_ref[...], h2_ref[...]).astype(BF16)
    for il in range(eb // PEER_KEYS):
        i = e * (eb // PEER_KEYS) + il
        rs = slice(il * PEER_KEYS, (il + 1) * PEER_KEYS)
        for c0 in range(0, tn, lc):
            ls = slice(c0, c0 + lc)
            w = jnp.zeros((PEER_KEYS, lc), BF16)
            for h in range(PEER_HEADS):
                keep = r1_ref[h, :, ls] < bc_ref[h, pl.ds(i, 1), ls].astype(BF16)
                w = w + (jnp.where(keep, e1_ref[h, :, ls], 0.0)
                         * a0_ref[h, pl.ds(i, 1), ls].astype(BF16))
            zi = z[rs, ls]
            t = jnp.tanh(zi * (GELU_C1 + GELU_C2 * (zi * zi)))
            act_s[rs, ls] = (zi + zi * t) * w
    acc_s[...] += _mm(vt_ref[...], act_s[...])

    @pl.when(e == pl.num_programs(1) - 1)
    def _():
        x = x1_ref[...] + gt2_ref[...] * acc_s[...].T
        y_ref[...] = _rms(x, fg_ref[...])


def _peer_dense(h2, u, vt, scores, x1, mod, mod_spec2, fg, tn, eb):
    n = h2.shape[0]
    n_exp = u.shape[0]
    bc, a0, r1, e1 = scores
    tok3 = pl.BlockSpec((PEER_HEADS, PEER_KEYS, tn), lambda i, e: (0, 0, i))
    return pl.pallas_call(
        functools.partial(_peer_dense_kernel, tn=tn, eb=eb, lc=min(tn, 2 * LANE)),
        grid=(n // tn, n_exp // eb),
        in_specs=[pl.BlockSpec((tn, D_MODEL), lambda i, e: (i, 0)),
                  pl.BlockSpec((eb, D_MODEL), lambda i, e: (e, 0)),
                  pl.BlockSpec((D_MODEL, eb), lambda i, e: (0, e)),
                  tok3, tok3, tok3, tok3,
                  pl.BlockSpec((tn, D_MODEL), lambda i, e: (i, 0)),
                  mod_spec2(5),
                  pl.BlockSpec((1, D_MODEL), lambda i, e: (0, 0))],
        out_specs=pl.BlockSpec((tn, D_MODEL), lambda i, e: (i, 0)),
        out_shape=jax.ShapeDtypeStruct((n, D_MODEL), F32),
        scratch_shapes=[pltpu.VMEM((D_MODEL, tn), F32), pltpu.VMEM((eb, tn), BF16)],
        compiler_params=_params("arbitrary", "arbitrary"),
        name="peer_dense",
    )(h2, u, vt, bc, a0, r1, e1, x1, mod, fg)


def _block_diag(w):
    nblk, bw, _ = w.shape
    eye = jnp.eye(nblk, dtype=w.dtype)
    return jnp.einsum('nde,nm->ndme', w, eye).reshape(nblk * bw, nblk * bw)


def kernel(x_prompt, x_sample, cache_cmp_kv, cache_sel_kv, cache_win_kv, state_lru_h, state_conv, page_table, c_prompt, c_sample, ada_w, ada_b, norm1_g, norm2_g, w_in, conv_w, conv_b, lru_wa, lru_ba, lru_wi, lru_bi, lru_lambda, cmp_w1, cmp_w2, cmp_pos, out_norm_lru, out_norm_att, w_out, peer_wq, peer_bq, peer_subkeys, peer_u, peer_v, rel_bias, final_g):
    nb, t, d = x_prompt.shape
    ns = x_sample.shape[0]
    n_pages = page_table.shape[1]
    past = n_pages * PAGE_SIZE
    wb = cache_win_kv.shape[2]
    assert x_sample.shape[1] == 1 and ada_w.shape[0] == 1 and d == D_MODEL
    assert t % (SEL_TILES * LANE) == 0 and ns % LANE == 0 and past % LANE == 0
    assert wb == min(WINDOW, past)
    row = lambda v: v.reshape(1, -1)

    w_in_p = jnp.pad(w_in[0], ((0, 0), (0, sum(IN_SPLITS) - w_in.shape[2]))).astype(BF16)
    wg = jnp.concatenate([_block_diag(lru_wa[0]), _block_diag(lru_wi[0])], axis=1).astype(BF16)
    bg = row(jnp.concatenate([lru_ba[0], lru_bi[0]]))
    w1r = cmp_w1[0].reshape(2, 2, CMP_STRIDE, HEAD_DIM, HEAD_DIM)
    eye2 = jnp.eye(2, dtype=F32)
    w1g = jnp.einsum('cmsdh,ce->scdmeh', w1r, eye2).reshape(CMP_STRIDE * LANE, 2 * LANE).astype(BF16)
    w2bd = jnp.einsum('che,cf->chfe', cmp_w2[0], eye2).reshape(LANE, LANE).astype(BF16)
    wq_t = peer_wq[0].T.astype(BF16)
    bq_t = peer_bq[0].reshape(-1, 1)
    subk = peer_subkeys[0].reshape(2 * PEER_HEADS, PEER_KEYS, -1).astype(BF16)
    u_b = peer_u[0].astype(BF16)
    vt_b = peer_v[0].T.astype(BF16)
    wo_b = w_out[0].astype(BF16)

    n_c = nb + ns
    c_all = jnp.pad(jnp.concatenate([c_prompt, c_sample]), ((0, -n_c % 8), (0, 0)))
    mod = _ada(c_all, ada_w[0], ada_b[0])
    mod_p = mod[:nb].reshape(nb, 1, 6 * D_MODEL)
    mod_s = mod[nb:n_c]

    tm_p, tm_s = 512, ns
    tn_p, tn_s = 512, ns
    tn_sc = 256 if ns % 256 == 0 else LANE

    def mod_spec_p(tm):
        per = t // tm
        return lambda j: pl.BlockSpec((None, 1, D_MODEL), lambda i, *_: (i // per, 0, j))

    def mod_spec_s(tm):
        return lambda j: pl.BlockSpec((tm, D_MODEL), lambda i, *_: (i, j))

    thr_last = BUCKET_THR[-1]
    nd = max(min(t // LANE, -(-(thr_last + LANE - 1) // LANE) + 1), WINDOW // LANE + 2)
    bt, bc = _bias_prompt(rel_bias, t, nd)
    bsc, bsw, bss = _bias_sample(rel_bias, past, wb)
    bss = jnp.transpose(bss, (1, 0, 2))
    b0 = rel_bias[0].reshape(N_HEADS, 1)

    ph = _poshid(cmp_pos[0], cmp_w1[0])
    lru_args = (conv_w[0], row(conv_b[0]), wg, bg, row(lru_lambda[0]))
    norms = (row(out_norm_lru[0]), row(out_norm_att[0]), row(norm2_g[0]), wo_b)

    xp = x_prompt.reshape(nb * t, d)
    xr, yg, q, kvc, kvs, kvw, gl = _inproj(xp, mod_p, mod_spec_p(tm_p), row(norm1_g[0]), w_in_p, tm_p)
    seq = lambda a: a.reshape(nb, t, a.shape[-1])
    o_lru, h_p, conv_p = _lru_prompt(seq(xr), seq(yg), *lru_args, tc=256)
    ck = _compress_prompt(seq(kvc), w1g, w2bd, ph)
    o_c, sel = _cattn_prompt(seq(q), ck, bc, 2 * LANE)
    o_s, o_w = _sattn_prompt(seq(q), sel, seq(kvs), seq(kvw), bt)
    flat = lambda a: a.reshape(nb * t, a.shape[-1])
    x1, h2 = _outproj(xp, flat(o_lru), flat(o_c), flat(o_s), flat(o_w), gl, mod_p,
                      mod_spec_p(tm_p), *norms, tm_p)
    scores = _peer_score(h2, wq_t, bq_t, subk, 256)
    y_p = _peer_dense(h2, u_b, vt_b, scores, x1, mod_p, mod_spec_p(tn_p), row(final_g), tn_p, 2048)

    xs = x_sample.reshape(ns, d)
    xr_s, yg_s, q_s, kvc_s, kvs_s, kvw_s, gl_s = _inproj(xs, mod_s, mod_spec_s(tm_s),
                                                          row(norm1_g[0]), w_in_p, tm_s)
    o_lru_s, h_s = _lru_sample(xr_s, yg_s, state_conv[0], state_lru_h[0], *lru_args)
    rows_minor = lambda a: jnp.transpose(a, (0, 2, 3, 4, 1))
    cmp_t = rows_minor(cache_cmp_kv[0]).reshape(-1, N_KV, LANE, PAGE_SIZE)
    ck_s = _compress_sample(cmp_t, page_table, w1g, w2bd, ph)
    o_c_s, idx = _cattn_sample(q_s, ck_s, bsc, past)
    top_s = idx.shape[1] // N_KV
    o_s_s, o_w_s = _sattn_sample(q_s, idx, page_table, rows_minor(cache_sel_kv[0]), kvs_s, kvw_s,
                                 rows_minor(cache_win_kv[0]), bss, bsw, b0, top_s,
                                 past // SEL_BLOCK)
    x1_s, h2_s = _outproj(xs, o_lru_s, o_c_s.reshape(ns, D_ATT), o_s_s, o_w_s, gl_s, mod_s,
                          mod_spec_s(tm_s), *norms, tm_s)
    scores_s = _peer_score(h2_s, wq_t, bq_t, subk, tn_sc if ns % tn_sc == 0 else ns)
    y_s = _peer_dense(h2_s, u_b, vt_b, scores_s, x1_s, mod_s, mod_spec_s(tn_s), row(final_g), tn_s, 2048)

    kv6 = lambda a, n, tt: a.reshape(1, n, tt, N_KV, 2, HEAD_DIM)
    win_p = kv6(kvw, nb, t)[:, :, t - min(WINDOW, t):]
    win_s = jnp.concatenate([cache_win_kv[:, :, 1:], kv6(kvw_s, ns, 1)], axis=2)[:, :, -wb:]
    conv_s = jnp.concatenate([state_conv[:, :, 1:], xr_s.reshape(1, ns, 1, D_RNN)], axis=2)
    return (y_p.reshape(nb, t, d), y_s.reshape(ns, 1, d),
            kv6(kvc, nb, t), kv6(kvc_s, ns, 1),
            kv6(kvs, nb, t), kv6(kvs_s, ns, 1),
            win_p, win_s,
            h_p.reshape(1, nb, D_RNN), h_s.reshape(1, ns, D_RNN),
            conv_p.reshape(1, nb, CONV_W - 1, D_RNN), conv_s)
```

```python
import functools
import math

import numpy as np
import jax
import jax.numpy as jnp
from jax import lax
from jax.experimental import pallas as pl
from jax.experimental.pallas import tpu as pltpu

F32 = jnp.float32
BF16 = jnp.bfloat16
HIGHEST = lax.Precision.HIGHEST

D_MODEL = 1024
D_RNN = 512
LRU_BLOCKS = 8
CONV_W = 4
LRU_C = 8.0
N_HEADS = 8
HEAD_DIM = 64
N_KV = 2
GROUP = 4
D_ATT = 512
KV_COLS = 2 * N_KV * HEAD_DIM
CMP_LEN = 32
CMP_STRIDE = 16
SEL_BLOCK = 64
SEL_TOP = 16
WINDOW = 512
SCALE = HEAD_DIM ** -0.5
LOG2E = math.log2(math.e)
NEG_INF = -1e30
REMOVED = -3e38
FORCE_SCORE = 1e6
NUM_BUCKETS = 32
MAX_DISTANCE = 1024
PEER_HEADS = 8
PEER_KEYS = 128
PEER_TOPK = 16
NORM_EPS = 1e-6
PAGE_SIZE = 128

LANE = 128
VMEM_LIMIT = 56 * 1024 * 1024

TOKEN_TILE = 512
SCORE_TILE = 256
EXPERT_TILE = 2048
CATTN_TILE = 512
LRU_TILE = 256


def _bucket_thresholds():
    d = np.arange(0, 1 << 15, dtype=np.int64)
    max_exact = NUM_BUCKETS // 2
    df = np.maximum(d, 1).astype(np.float32)
    ratio = (np.log(df / np.float32(max_exact)) / np.float32(math.log(MAX_DISTANCE / max_exact))
             * np.float32(NUM_BUCKETS - max_exact))
    large = np.minimum(max_exact + ratio.astype(np.int32), NUM_BUCKETS - 1)
    bucket = np.where(d < max_exact, d, large)
    assert np.all(np.diff(bucket) >= 0)
    return tuple(int(np.argmax(bucket >= k)) for k in range(1, NUM_BUCKETS))


BUCKET_THR = _bucket_thresholds()


def _params(*sem):
    return pltpu.CompilerParams(dimension_semantics=sem or None, vmem_limit_bytes=VMEM_LIMIT)


def _nt(a, b):
    return lax.dot_general(a, b, (((1,), (1,)), ((), ())), preferred_element_type=F32)


def _mm(a, b):
    return jnp.dot(a, b, preferred_element_type=F32)


def _rms(x, g):
    return x * lax.rsqrt(jnp.mean(x * x, axis=-1, keepdims=True) + NORM_EPS) * g


def _bias_lookup(dist, rb_ref, h):
    b = jnp.full(dist.shape, rb_ref[0, h], F32)
    for k in range(1, NUM_BUCKETS):
        b = jnp.where(dist >= BUCKET_THR[k - 1], rb_ref[k, h], b)
    return b


def _ada_kernel(c_ref, w_ref, b_ref, o_ref):
    c = c_ref[...]
    o_ref[...] = jnp.dot(jax.nn.silu(c), w_ref[...], precision=HIGHEST,
                         preferred_element_type=F32) + b_ref[...]


def _ada(c_all, w, b):
    rows, d = c_all.shape
    cols = w.shape[1]
    tn = 512
    return pl.pallas_call(
        _ada_kernel,
        grid=(cols // tn,),
        in_specs=[pl.BlockSpec((rows, d), lambda j: (0, 0)),
                  pl.BlockSpec((d, tn), lambda j: (0, j)),
                  pl.BlockSpec((1, tn), lambda j: (0, j))],
        out_specs=pl.BlockSpec((rows, tn), lambda j: (0, j)),
        out_shape=jax.ShapeDtypeStruct((rows, cols), F32),
        compiler_params=_params("arbitrary"),
    )(c_all, w, b.reshape(1, cols))


def _bias_prompt_kernel(rb_ref, bt_ref, bc_ref, *, nd, t, nc):
    h = pl.program_id(0)
    row = lax.broadcasted_iota(jnp.int32, (LANE, LANE), 0)
    col = lax.broadcasted_iota(jnp.int32, (LANE, LANE), 1)
    for dl in range(nd):
        dist = dl * LANE + col - row
        b = _bias_lookup(dist, rb_ref, h) * LOG2E
        bt_ref[0, dl] = jnp.where(dist >= 0, b, NEG_INF)
        bt_ref[1, dl] = jnp.where((dist >= 0) & (dist < WINDOW), b, NEG_INF)
    rq = lax.broadcasted_iota(jnp.int32, (LANE, nc), 0)
    ends = lax.broadcasted_iota(jnp.int32, (LANE, nc), 1) * CMP_STRIDE + (CMP_LEN - 1)

    def chunk(i, carry):
        r0 = pl.multiple_of(i * LANE, LANE)
        bc_ref[pl.ds(r0, LANE), :] = _bias_lookup(r0 + rq - ends, rb_ref, h)
        return carry

    lax.fori_loop(0, t // LANE, chunk, 0)


def _bias_prompt(rel_bias, t, nd):
    nc = t // CMP_STRIDE
    return pl.pallas_call(
        functools.partial(_bias_prompt_kernel, nd=nd, t=t, nc=nc),
        grid=(N_HEADS,),
        in_specs=[pl.BlockSpec(memory_space=pltpu.SMEM)],
        out_specs=[pl.BlockSpec((None, 2, nd, LANE, LANE), lambda h: (h, 0, 0, 0, 0)),
                   pl.BlockSpec((None, t, nc), lambda h: (h, 0, 0))],
        out_shape=[jax.ShapeDtypeStruct((N_HEADS, 2, nd, LANE, LANE), F32),
                   jax.ShapeDtypeStruct((N_HEADS, t, nc), F32)],
        compiler_params=_params("arbitrary"),
    )(rel_bias)


def _bias_sample_kernel(rb_ref, bsc_ref, bsw_ref, bss_ref, *, past, ncs, wb, n_pages):
    ends = lax.broadcasted_iota(jnp.int32, (1, ncs), 1) * CMP_STRIDE + (CMP_LEN - 1)
    jw = lax.broadcasted_iota(jnp.int32, (1, wb), 1)
    kpos = (lax.broadcasted_iota(jnp.int32, (n_pages, PAGE_SIZE), 0) * PAGE_SIZE
            + lax.broadcasted_iota(jnp.int32, (n_pages, PAGE_SIZE), 1))
    for h in range(N_HEADS):
        bsc_ref[h:h + 1, :] = _bias_lookup(past - ends, rb_ref, h)
        bsw_ref[h:h + 1, :] = _bias_lookup(wb - jw, rb_ref, h)
        bss_ref[h] = _bias_lookup(past - kpos, rb_ref, h)


def _bias_sample(rel_bias, past, wb):
    ncs = past // CMP_STRIDE
    n_pages = past // PAGE_SIZE
    return pl.pallas_call(
        functools.partial(_bias_sample_kernel, past=past, ncs=ncs, wb=wb, n_pages=n_pages),
        in_specs=[pl.BlockSpec(memory_space=pltpu.SMEM)],
        out_shape=[jax.ShapeDtypeStruct((N_HEADS, ncs), F32),
                   jax.ShapeDtypeStruct((N_HEADS, wb), F32),
                   jax.ShapeDtypeStruct((N_HEADS, n_pages, PAGE_SIZE), F32)],
        compiler_params=_params(),
        name="bias_sample",
    )(rel_bias)


IN_SPLITS = (D_RNN, D_RNN, D_ATT, KV_COLS, KV_COLS, KV_COLS, LANE)


def _inproj_kernel(x_ref, sh_ref, sc_ref, g_ref, w_ref, *out_refs):
    h = _rms(x_ref[...], g_ref[...]) * (1.0 + sc_ref[...]) + sh_ref[...]
    z = _mm(h.astype(BF16), w_ref[...])
    off = 0
    for o_ref, wdt in zip(out_refs, IN_SPLITS):
        o_ref[...] = z[:, off:off + wdt]
        off += wdt


def _inproj(x, mod, mod_spec, g1, w_in_p, tm):
    n = x.shape[0]
    cols = w_in_p.shape[1]
    return pl.pallas_call(
        _inproj_kernel,
        grid=(n // tm,),
        in_specs=[pl.BlockSpec((tm, D_MODEL), lambda i: (i, 0)),
                  mod_spec(0), mod_spec(1),
                  pl.BlockSpec((1, D_MODEL), lambda i: (0, 0)),
                  pl.BlockSpec((D_MODEL, cols), lambda i: (0, 0))],
        out_specs=[pl.BlockSpec((tm, wdt), lambda i: (i, 0)) for wdt in IN_SPLITS],
        out_shape=[jax.ShapeDtypeStruct((n, wdt), F32) for wdt in IN_SPLITS],
        compiler_params=_params("arbitrary"),
    )(x, mod, mod, g1, w_in_p)


def _lru_gates(xc, wg_ref, bg_ref, lam_ref):
    gates = _mm(xc.astype(BF16), wg_ref[...]) + bg_ref[...]
    r = jax.nn.sigmoid(gates[:, :D_RNN])
    i = jax.nn.sigmoid(gates[:, D_RNN:])
    log_a = -LRU_C * r * jax.nn.softplus(-lam_ref[...])
    a = jnp.exp(log_a)
    u = jnp.sqrt(-jnp.tanh(log_a) * (a * a + 1.0)) * i * xc
    return a, u


def _lru_prompt_kernel(xr_ref, yg_ref, cw_ref, cb_ref, wg_ref, bg_ref, lam_ref,
                       o_ref, hl_ref, cv_ref, xp_s, a_s, u_s, hs_s, h_s, *, nb, tc):
    t = pl.program_id(0)

    @pl.when(t == 0)
    def _():
        xp_s[:, 0:8, :] = jnp.zeros((nb, 8, D_RNN), F32)
        h_s[...] = jnp.zeros((nb, 1, D_RNN), F32)

    @pl.when(t > 0)
    def _():
        xp_s[:, 0:8, :] = xp_s[:, tc:tc + 8, :]

    xp_s[:, 8:, :] = xr_ref[...]
    xc = xp_s[:, 5:5 + tc, :] * cw_ref[0:1, :]
    for k in range(1, CONV_W):
        xc = xc + xp_s[:, 5 + k:5 + k + tc, :] * cw_ref[k:k + 1, :]
    xc = xc + cb_ref[...]
    a, u = _lru_gates(xc.reshape(nb * tc, D_RNN), wg_ref, bg_ref, lam_ref)
    a_s[...] = a.reshape(nb, tc, D_RNN)
    u_s[...] = u.reshape(nb, tc, D_RNN)

    def step(tt, h):
        h = a_s[:, pl.ds(tt, 1), :] * h + u_s[:, pl.ds(tt, 1), :]
        hs_s[:, pl.ds(tt, 1), :] = h
        return h

    h = lax.fori_loop(0, tc, step, h_s[...], unroll=8)
    h_s[...] = h
    o_ref[...] = hs_s[...] * jax.nn.gelu(yg_ref[...])
    hl_ref[...] = h
    cv_ref[...] = xp_s[:, tc + 8 - (CONV_W - 1):tc + 8, :]


def _lru_prompt(xr, yg, cw, cb, wg, bg, lam, tc):
    nb, t, _ = xr.shape
    blk = pl.BlockSpec((nb, tc, D_RNN), lambda i: (0, i, 0))
    full = lambda shape: pl.BlockSpec(shape, lambda i: tuple(0 for _ in shape))
    return pl.pallas_call(
        functools.partial(_lru_prompt_kernel, nb=nb, tc=tc),
        grid=(t // tc,),
        in_specs=[blk, blk, full(cw.shape), full(cb.shape), full(wg.shape), full(bg.shape),
                  full(lam.shape)],
        out_specs=[blk, full((nb, 1, D_RNN)), full((nb, CONV_W - 1, D_RNN))],
        out_shape=[jax.ShapeDtypeStruct((nb, t, D_RNN), F32),
                   jax.ShapeDtypeStruct((nb, 1, D_RNN), F32),
                   jax.ShapeDtypeStruct((nb, CONV_W - 1, D_RNN), F32)],
        scratch_shapes=[pltpu.VMEM((nb, tc + 8, D_RNN), F32), pltpu.VMEM((nb, tc, D_RNN), F32),
                        pltpu.VMEM((nb, tc, D_RNN), F32), pltpu.VMEM((nb, tc, D_RNN), F32),
                        pltpu.VMEM((nb, 1, D_RNN), F32)],
        compiler_params=_params("arbitrary"),
    )(xr, yg, cw, cb, wg, bg, lam)


def _lru_sample_kernel(x_ref, yg_ref, b0_ref, b1_ref, b2_ref, h0_ref, cw_ref, cb_ref, wg_ref,
                       bg_ref, lam_ref, o_ref, hn_ref):
    xc = b0_ref[...] * cw_ref[0:1, :]
    xc = xc + b1_ref[...] * cw_ref[1:2, :]
    xc = xc + b2_ref[...] * cw_ref[2:3, :]
    xc = xc + x_ref[...] * cw_ref[3:4, :]
    xc = xc + cb_ref[...]
    a, u = _lru_gates(xc, wg_ref, bg_ref, lam_ref)
    h = a * h0_ref[...] + u
    hn_ref[...] = h
    o_ref[...] = h * jax.nn.gelu(yg_ref[...])


def _lru_sample(x, yg, buf, h0, cw, cb, wg, bg, lam):
    s = x.shape[0]
    return pl.pallas_call(
        _lru_sample_kernel,
        out_shape=[jax.ShapeDtypeStruct((s, D_RNN), F32), jax.ShapeDtypeStruct((s, D_RNN), F32)],
        compiler_params=_params(),
    )(x, yg, buf[:, 0], buf[:, 1], buf[:, 2], h0, cw, cb, wg, bg, lam)


CHUNK_COLS = CMP_STRIDE * KV_COLS


def _poshid_kernel(pos_ref, w_ref, o_ref):
    parts = [jnp.dot(pos_ref[c], w_ref[c], precision=HIGHEST, preferred_element_type=F32)[0:1]
             for c in range(2)]
    o_ref[...] = jnp.concatenate(parts, axis=1)


def _poshid(cmp_pos, cmp_w1):
    k = CMP_LEN * HEAD_DIM
    pos = jnp.broadcast_to(cmp_pos.reshape(2, 1, k), (2, 8, k))
    return pl.pallas_call(
        _poshid_kernel,
        out_shape=jax.ShapeDtypeStruct((1, 2 * HEAD_DIM), F32),
        compiler_params=_params(),
    )(pos, cmp_w1.reshape(2, k, HEAD_DIM))


def _compress_rows(x_s, n, w1_ref, w2_ref, ph_ref):
    outs = []
    for g in range(N_KV):
        xg = jnp.concatenate(
            [x_s[g, pl.ds(s, n, stride=CMP_STRIDE), :].astype(BF16) for s in range(CMP_STRIDE)],
            axis=1)
        p = _mm(xg, w1_ref[...])
        hid = ph_ref[...] + p[:, :LANE] + pltpu.roll(p[:, LANE:], n - 1, axis=0)
        outs.append(_mm(jax.nn.gelu(hid).astype(BF16), w2_ref[...]))
    return jnp.concatenate(outs, axis=1)


def _compress_prompt_kernel(x_ref, w1_ref, w2_ref, ph_ref, o_ref, x_s, *, n):
    for g in range(N_KV):
        x_s[g] = x_ref[:, g * LANE:(g + 1) * LANE]
    o_ref[...] = _compress_rows(x_s, n, w1_ref, w2_ref, ph_ref)


def _compress_prompt(kvc, w1g, w2bd, ph):
    nb, t, _ = kvc.shape
    n = t // CMP_STRIDE
    return pl.pallas_call(
        functools.partial(_compress_prompt_kernel, n=n),
        grid=(nb,),
        in_specs=[pl.BlockSpec((None, t, KV_COLS), lambda b: (b, 0, 0)),
                  pl.BlockSpec(w1g.shape, lambda b: (0, 0)),
                  pl.BlockSpec(w2bd.shape, lambda b: (0, 0)),
                  pl.BlockSpec(ph.shape, lambda b: (0, 0))],
        out_specs=pl.BlockSpec((None, n, KV_COLS), lambda b: (b, 0, 0)),
        out_shape=jax.ShapeDtypeStruct((nb, n, KV_COLS), F32),
        scratch_shapes=[pltpu.VMEM((N_KV, t, LANE), F32)],
        compiler_params=_params("arbitrary"),
        name="compress_prompt",
    )(kvc, w1g, w2bd, ph)


MAX_PAGES_PER_STEP = 32


def _compress_sample_kernel(pt_ref, *refs, n, pps):
    page_refs = refs[:pps]
    w1_ref, w2_ref, ph_ref, o_ref, x_s = refs[pps:]
    p = pl.program_id(1)
    for k in range(pps):
        r0 = pl.multiple_of((p * pps + k) * PAGE_SIZE, PAGE_SIZE)
        for g in range(N_KV):
            x_s[g, pl.ds(r0, PAGE_SIZE), :] = page_refs[k][g].T

    @pl.when(p == pl.num_programs(1) - 1)
    def _():
        o_ref[...] = _compress_rows(x_s, n, w1_ref, w2_ref, ph_ref)


def _compress_sample(cache_t, page_table, w1g, w2bd, ph):
    s, n_pages = page_table.shape
    n = n_pages * PAGE_SIZE // CMP_STRIDE
    pps = math.gcd(n_pages, MAX_PAGES_PER_STEP)

    def page_spec(k):
        return pl.BlockSpec((None, N_KV, LANE, PAGE_SIZE),
                            lambda b, p, pt: (pt[b, p * pps + k], 0, 0, 0))

    const = lambda shape: pl.BlockSpec(shape, lambda b, p, pt: (0, 0))
    return pl.pallas_call(
        functools.partial(_compress_sample_kernel, n=n, pps=pps),
        grid_spec=pltpu.PrefetchScalarGridSpec(
            num_scalar_prefetch=1,
            grid=(s, n_pages // pps),
            in_specs=[page_spec(k) for k in range(pps)]
            + [const(w1g.shape), const(w2bd.shape), const(ph.shape)],
            out_specs=pl.BlockSpec((None, n, KV_COLS), lambda b, p, pt: (b, 0, 0)),
            scratch_shapes=[pltpu.VMEM((N_KV, n_pages * PAGE_SIZE, LANE), F32)]),
        out_shape=jax.ShapeDtypeStruct((s, n, KV_COLS), F32),
        compiler_params=_params("arbitrary", "arbitrary"),
        name="compress_sample",
    )(page_table, *([cache_t] * pps), w1g, w2bd, ph)


def _select_top(sc, top, want_idx):
    rows, nlane = sc.shape
    j = lax.broadcasted_iota(jnp.int32, (rows, nlane), 1).astype(F32)
    mask = jnp.zeros((rows, nlane), F32)
    picks = []
    for _ in range(top):
        m = jnp.max(sc, axis=-1, keepdims=True)
        idx = jnp.min(jnp.where(sc == m, j, float(nlane)), axis=-1, keepdims=True)
        pick = j == idx
        mask = jnp.where(pick, 1.0, mask)
        sc = jnp.where(pick, REMOVED, sc)
        picks.append(idx)
    if want_idx:
        return mask, jnp.concatenate(picks, axis=1).astype(jnp.int32)
    return mask


def _selection_scores(score, blk, qpos, nsel):
    cur = qpos // SEL_BLOCK
    forced = (blk == 0) | (blk == cur) | (blk == cur - 1)
    sc = jnp.where(forced, FORCE_SCORE, jnp.where(blk * SEL_BLOCK <= qpos, score, -1.0))
    return jnp.where(blk < nsel, sc, REMOVED)


def _softmax_rows(s, valid):
    s = jnp.where(valid, s, NEG_INF)
    p = jnp.exp(s - jnp.max(s, axis=-1, keepdims=True))
    return p / jnp.sum(p, axis=-1, keepdims=True)


def _cattn_prompt_kernel(q_ref, ck_ref, bc_ref, covt_ref, oc_ref, sel_ref, *, tq, nc, nsel, top):
    qpos = pl.program_id(1) * tq + lax.broadcasted_iota(jnp.int32, (tq, 1), 0)
    n_idx = lax.broadcasted_iota(jnp.int32, (1, nc), 1)
    valid = (qpos >= n_idx * CMP_STRIDE + (CMP_LEN - 1)) & (n_idx < nc - 1)
    anyv = jnp.max(valid.astype(F32), axis=-1, keepdims=True)
    blk_t = lax.broadcasted_iota(jnp.int32, (nsel, 1), 0)
    qpos_t = pl.program_id(1) * tq + lax.broadcasted_iota(jnp.int32, (1, tq), 1)
    row_t = lax.broadcasted_iota(jnp.int32, (nsel, tq), 0).astype(F32)
    outs, masks = [], []
    for g in range(N_KV):
        kv = ck_ref[:, g * LANE:(g + 1) * LANE].astype(BF16)
        k, v = kv[:, :HEAD_DIM], kv[:, HEAD_DIM:]
        psum = jnp.zeros((tq, nc), F32)
        for r in range(GROUP):
            h = g * GROUP + r
            qh = q_ref[:, h * HEAD_DIM:(h + 1) * HEAD_DIM].astype(BF16)
            p = _softmax_rows(_nt(qh, k) * SCALE + bc_ref[h], valid) * anyv
            outs.append(_mm(p.astype(BF16), v))
            psum = psum + p
        score_t = lax.dot_general(covt_ref[...], psum, (((1,), (1,)), ((), ())),
                                  precision=HIGHEST, preferred_element_type=F32)
        _, rank = _extract_top(_selection_scores(score_t, blk_t, qpos_t, nsel), top, row_t)
        masks.append(jnp.where(rank < top, 1.0, 0.0).T)
    oc_ref[...] = jnp.concatenate(outs, axis=1)
    sel_ref[...] = jnp.concatenate(masks, axis=1)


def _covers(n_cmp_rows, n_cmp, n_sel, n_sel_cols):
    start = np.arange(n_cmp_rows)[:, None] * CMP_STRIDE
    blk = np.arange(n_sel_cols)[None, :]
    cov = ((start < (blk + 1) * SEL_BLOCK) & (start + CMP_LEN > blk * SEL_BLOCK)
           & (np.arange(n_cmp_rows)[:, None] < n_cmp) & (blk < n_sel))
    return jnp.asarray(cov.astype(np.float32))


def _cattn_prompt(q, ck, bc, tq):
    nb, t, _ = q.shape
    nc = ck.shape[1]
    nsel = t // SEL_BLOCK
    top = min(SEL_TOP, nsel)
    cov = _covers(nc, nc - 1, nsel, nsel).T
    return pl.pallas_call(
        functools.partial(_cattn_prompt_kernel, tq=tq, nc=nc, nsel=nsel, top=top),
        grid=(nb, t // tq),
        in_specs=[pl.BlockSpec((None, tq, D_ATT), lambda b, i: (b, i, 0)),
                  pl.BlockSpec((None, nc, KV_COLS), lambda b, i: (b, 0, 0)),
                  pl.BlockSpec((N_HEADS, tq, nc), lambda b, i: (0, i, 0)),
                  pl.BlockSpec(cov.shape, lambda b, i: (0, 0))],
        out_specs=[pl.BlockSpec((None, tq, D_ATT), lambda b, i: (b, i, 0)),
                   pl.BlockSpec((None, tq, N_KV * nsel), lambda b, i: (b, i, 0))],
        out_shape=[jax.ShapeDtypeStruct((nb, t, D_ATT), F32),
                   jax.ShapeDtypeStruct((nb, t, N_KV * nsel), F32)],
        compiler_params=_params("arbitrary", "arbitrary"),
    )(q, ck, bc, cov)


SAMPLES_PER_STEP = 8


def _cattn_sample_kernel(q_ref, ck_ref, bsc_ref, cov_ref, oc_ref, idx_ref, *, past, nc, nsel, top):
    n_idx = lax.broadcasted_iota(jnp.int32, (1, nc), 1)
    valid = (past >= n_idx * CMP_STRIDE + (CMP_LEN - 1)) & (n_idx < nc - 1)
    anyv = jnp.max(valid.astype(F32), axis=-1, keepdims=True)
    nlane = cov_ref.shape[1]
    blk = lax.broadcasted_iota(jnp.int32, (1, nlane), 1)
    idxs = []
    for g in range(N_KV):
        psums = []
        for i in range(SAMPLES_PER_STEP):
            kv = ck_ref[i, :, g * LANE:(g + 1) * LANE].astype(BF16)
            q4 = q_ref[i, g * GROUP:(g + 1) * GROUP, :].astype(BF16)
            s = _nt(q4, kv[:, :HEAD_DIM]) * SCALE + bsc_ref[g * GROUP:(g + 1) * GROUP, :]
            p = _softmax_rows(s, valid) * anyv
            oc_ref[i, g * GROUP:(g + 1) * GROUP, :] = _mm(p.astype(BF16), kv[:, HEAD_DIM:])
            psums.append(jnp.sum(p, axis=0, keepdims=True))
        score = jnp.dot(jnp.concatenate(psums, axis=0), cov_ref[...], precision=HIGHEST,
                        preferred_element_type=F32)
        _, idx = _select_top(_selection_scores(score, blk, past, nsel), top, True)
        idxs.append(idx)
    idx_ref[...] = jnp.concatenate(idxs, axis=1)


def _cattn_sample(q, ck, bsc, past):
    s = q.shape[0]
    nc = ck.shape[1]
    nsel = -(-(past + 1) // SEL_BLOCK)
    top = min(SEL_TOP, nsel)
    nlane = -(-nsel // LANE) * LANE
    cov = _covers(nc, nc - 1, nsel, nlane)
    g = SAMPLES_PER_STEP
    return pl.pallas_call(
        functools.partial(_cattn_sample_kernel, past=past, nc=nc, nsel=nsel, top=top),
        grid=(s // g,),
        in_specs=[pl.BlockSpec((g, N_HEADS, HEAD_DIM), lambda i: (i, 0, 0)),
                  pl.BlockSpec((g, nc, KV_COLS), lambda i: (i, 0, 0)),
                  pl.BlockSpec(bsc.shape, lambda i: (0, 0)),
                  pl.BlockSpec(cov.shape, lambda i: (0, 0))],
        out_specs=[pl.BlockSpec((g, N_HEADS, HEAD_DIM), lambda i: (i, 0, 0)),
                   pl.BlockSpec((g, N_KV * top), lambda i: (i, 0))],
        out_shape=[jax.ShapeDtypeStruct((s, N_HEADS, HEAD_DIM), F32),
                   jax.ShapeDtypeStruct((s, N_KV * top), jnp.int32)],
        compiler_params=_params("arbitrary"),
    )(q.reshape(s, N_HEADS, HEAD_DIM), ck, bsc, cov)


SEL_TILES = 8


def _flash_tiles(qt, kvs, biases, pens, m_ref, acc_ref):
    kv = jnp.concatenate(kvs, axis=0)
    s = _mm(kv.astype(BF16), qt) + jnp.concatenate(biases, axis=0)
    if pens is not None:
        s = s + jnp.concatenate([jnp.concatenate(pens, axis=0)] * GROUP, axis=1)
    m_old = m_ref[...]
    m_new = jnp.maximum(m_old, jnp.max(s, axis=0, keepdims=True))
    row = lax.broadcasted_iota(jnp.int32, (LANE, kv.shape[0]), 0)
    ones_vt = jnp.where(row < HEAD_DIM, 1.0, kv.T).astype(BF16)
    acc_ref[...] = (jnp.exp2(m_old - m_new) * acc_ref[...]
                    + _mm(ones_vt, jnp.exp2(s - m_new).astype(BF16)))
    m_ref[...] = m_new


def _sattn_prompt_kernel(q_ref, sel_ref, kvs_ref, kvw_ref, bt_ref, os_ref, ow_ref,
                         m_s, acc_s, *, nsel, nd, npw):
    c = pl.program_id(1)
    ej = lax.broadcasted_iota(jnp.int32, (LANE, nsel), 1)
    ek = lax.broadcasted_iota(jnp.int32, (LANE, nsel), 0) // SEL_BLOCK
    q_t = (q_ref[...] * (SCALE * LOG2E)).T
    zpad = jnp.zeros((LANE - HEAD_DIM, GROUP * LANE), F32)
    qts = [jnp.concatenate(
        [jnp.concatenate([q_t[(g * GROUP + r) * HEAD_DIM:(g * GROUP + r + 1) * HEAD_DIM]
                          for r in range(GROUP)], axis=1), zpad], axis=0).astype(BF16)
        for g in range(N_KV)]
    selgs = [sel_ref[:, g * nsel:(g + 1) * nsel].astype(BF16) for g in range(N_KV)]

    def reset():
        m_s[...] = jnp.full(m_s.shape, NEG_INF, F32)
        acc_s[...] = jnp.zeros(acc_s.shape, F32)

    def result():
        outs = []
        for g in range(N_KV):
            acc = acc_s[g]
            o = (acc / acc[0:1, :]).T[:, HEAD_DIM:]
            outs += [o[r * LANE:(r + 1) * LANE] for r in range(GROUP)]
        return jnp.concatenate(outs, axis=1)

    def bias_tiles(kind, g, kb):
        dl = jnp.clip(c - kb, 0, nd - 1)
        if kind == 1:
            dl = jnp.where(kb >= 0, dl, 2 * npw - 1)
        return jnp.concatenate([bt_ref[g * GROUP + r, kind, dl] for r in range(GROUP)], axis=1)

    def kv_tile(ref, g, kb):
        k0 = pl.multiple_of(jnp.maximum(kb, 0) * LANE, LANE)
        return ref[pl.ds(k0, LANE), g * LANE:(g + 1) * LANE]

    def update(kind, ref, g, kbs, pens):
        _flash_tiles(qts[g], [kv_tile(ref, g, kb) for kb in kbs],
                     [bias_tiles(kind, g, kb) for kb in kbs], pens, m_s.at[g], acc_s.at[g])

    def sel_step(kp, carry):
        kbs = tuple(SEL_TILES * kp + k for k in range(SEL_TILES))
        for g in range(N_KV):
            pens = []
            for kb in kbs:
                first = jnp.where(kb <= c, (LANE // SEL_BLOCK) * kb, -(LANE // SEL_BLOCK))
                expand = (ej == first + ek).astype(BF16)
                chosen = _nt(expand, selgs[g]) > 0.5
                pens.append(jnp.where(chosen, 0.0, NEG_INF))
            update(0, kvs_ref, g, kbs, pens)
        return carry

    reset()
    lax.fori_loop(0, (c + SEL_TILES) // SEL_TILES, sel_step, 0)
    os_ref[...] = result()

    reset()
    kbs = tuple(c - (2 * npw - 1) + k for k in range(2 * npw))
    for g in range(N_KV):
        update(1, kvw_ref, g, kbs, None)
    ow_ref[...] = result()


def _sattn_prompt(q, sel, kvs, kvw, bt):
    nb, t, _ = q.shape
    nsel = t // SEL_BLOCK
    nd = bt.shape[2]
    npw = (WINDOW // LANE + 2) // 2
    qblk = pl.BlockSpec((None, LANE, D_ATT), lambda b, i: (b, i, 0))
    kvblk = pl.BlockSpec((None, t, KV_COLS), lambda b, i: (b, 0, 0))
    return pl.pallas_call(
        functools.partial(_sattn_prompt_kernel, nsel=nsel, nd=nd, npw=npw),
        grid=(nb, t // LANE),
        in_specs=[qblk, pl.BlockSpec((None, LANE, N_KV * nsel), lambda b, i: (b, i, 0)),
                  kvblk, kvblk, pl.BlockSpec(bt.shape, lambda b, i: (0, 0, 0, 0, 0))],
        out_specs=[qblk, qblk],
        out_shape=[jax.ShapeDtypeStruct((nb, t, D_ATT), F32)] * 2,
        scratch_shapes=[pltpu.VMEM((N_KV, 1, GROUP * LANE), F32),
                        pltpu.VMEM((N_KV, LANE, GROUP * LANE), F32)],
        compiler_params=_params("arbitrary", "arbitrary"),
        name="sattn_prompt",
    )(q, sel, kvs, kvw, bt)


def _sattn_sample_kernel(idx_ref, pt_ref, *refs, top, npb, n_pages, wb):
    nblk = N_KV * top
    blk_refs = refs[:nblk]
    (q_ref, ksn_ref, kwn_ref, win_ref, bss_ref, bsw_ref, b0_ref, os_ref, ow_ref) = refs[nblk:]
    b = pl.program_id(0)
    halves = PAGE_SIZE // SEL_BLOCK
    half = lax.broadcasted_iota(jnp.int32, (1, PAGE_SIZE), 1) // SEL_BLOCK
    jw = lax.broadcasted_iota(jnp.int32, (1, wb), 1)
    for g in range(N_KV):
        hs = slice(g * GROUP, (g + 1) * GROUP)
        q4f = q_ref[hs, :]
        q4 = q4f.astype(BF16)
        b0 = b0_ref[hs, :]

        def new_logit(new_ref):
            kn = new_ref[:, g * LANE:g * LANE + HEAD_DIM]
            return jnp.sum(q4f * kn, axis=-1, keepdims=True) * SCALE + b0

        def new_value(new_ref):
            return new_ref[:, g * LANE + HEAD_DIM:(g + 1) * LANE]

        logits, vts = [], []
        for j in range(top):
            ib = idx_ref[b, g * top + j]
            page = jnp.minimum(ib // halves, n_pages - 1)
            s = _mm(q4, blk_refs[g * top + j][0].astype(BF16)) * SCALE + bss_ref[page, hs, :]
            logits.append(jnp.where(half == jnp.where(ib < npb, ib % halves, -1), s, NEG_INF))
            vts.append(blk_refs[g * top + j][1].astype(BF16))
        s_new = new_logit(ksn_ref)
        m = s_new
        for s in logits:
            m = jnp.maximum(m, jnp.max(s, axis=-1, keepdims=True))
        p_new = jnp.exp(s_new - m)
        l = p_new
        acc = p_new * new_value(ksn_ref)
        for s, vt in zip(logits, vts):
            p = jnp.exp(s - m)
            l = l + jnp.sum(p, axis=-1, keepdims=True)
            acc = acc + _nt(p.astype(BF16), vt)
        os_ref[hs, :] = acc / l

        s = _mm(q4, win_ref[g, 0].astype(BF16)) * SCALE + bsw_ref[hs, :]
        s = jnp.where(wb - jw < WINDOW, s, NEG_INF)
        s_new = new_logit(kwn_ref)
        m = jnp.maximum(s_new, jnp.max(s, axis=-1, keepdims=True))
        p = jnp.exp(s - m)
        p_new = jnp.exp(s_new - m)
        l = p_new + jnp.sum(p, axis=-1, keepdims=True)
        acc = p_new * new_value(kwn_ref) + _nt(p.astype(BF16), win_ref[g, 1].astype(BF16))
        ow_ref[hs, :] = acc / l


def _sattn_sample(q, idx, page_table, sel_t, ks_new, kw_new, win_t, bss, bsw, b0, top, npb):
    s, n_pages = page_table.shape
    wb = win_t.shape[-1]
    halves = PAGE_SIZE // SEL_BLOCK

    def blk_spec(g, j):
        def imap(b, idx_r, pt_r):
            ib = idx_r[b, g * top + j]
            return (pt_r[b, jnp.minimum(ib // halves, n_pages - 1)], g, 0, 0, 0)
        return pl.BlockSpec((None, None, 2, HEAD_DIM, PAGE_SIZE), imap)

    per_b = lambda shape: pl.BlockSpec((None,) + shape,
                                       lambda b, i_r, p_r: (b,) + tuple(0 for _ in shape))
    const = lambda shape: pl.BlockSpec(shape, lambda b, i_r, p_r: tuple(0 for _ in shape))
    out = pl.pallas_call(
        functools.partial(_sattn_sample_kernel, top=top, npb=npb, n_pages=n_pages, wb=wb),
        grid_spec=pltpu.PrefetchScalarGridSpec(
            num_scalar_prefetch=2,
            grid=(s,),
            in_specs=[blk_spec(g, j) for g in range(N_KV) for j in range(top)]
            + [per_b((N_HEADS, HEAD_DIM)), per_b((1, KV_COLS)), per_b((1, KV_COLS)),
               per_b(win_t.shape[1:]), const(bss.shape), const(bsw.shape), const(b0.shape)],
            out_specs=[per_b((N_HEADS, HEAD_DIM)), per_b((N_HEADS, HEAD_DIM))]),
        out_shape=[jax.ShapeDtypeStruct((s, N_HEADS, HEAD_DIM), F32)] * 2,
        compiler_params=_params("arbitrary"),
        name="sattn_sample",
    )(idx, page_table, *([sel_t] * (N_KV * top)), q.reshape(s, N_HEADS, HEAD_DIM),
      ks_new.reshape(s, 1, KV_COLS), kw_new.reshape(s, 1, KV_COLS), win_t, bss, bsw, b0)
    return out[0].reshape(s, D_ATT), out[1].reshape(s, D_ATT)


def _outproj_kernel(x_ref, ol_ref, oc_ref, os_ref, ow_ref, gl_ref, gt1_ref, sh2_ref, sc2_ref,
                    gexp_ref, nl_ref, na_ref, n2_ref, wo_ref, x1_ref, h2_ref):
    gates = jax.nn.sigmoid(jnp.dot(gl_ref[...], gexp_ref[...], precision=HIGHEST,
                                   preferred_element_type=F32))
    o_att = (gates[:, :D_ATT] * oc_ref[...] + gates[:, D_ATT:2 * D_ATT] * os_ref[...]
             + gates[:, 2 * D_ATT:] * ow_ref[...])
    n_lru = _rms(ol_ref[...], nl_ref[...]).astype(BF16)
    n_att = _rms(o_att, na_ref[...]).astype(BF16)
    mixed = _mm(n_lru, wo_ref[:D_RNN, :]) + _mm(n_att, wo_ref[D_RNN:, :])
    x1 = x_ref[...] + gt1_ref[...] * mixed
    x1_ref[...] = x1
    h2_ref[...] = (_rms(x1, n2_ref[...]) * (1.0 + sc2_ref[...]) + sh2_ref[...]).astype(BF16)


def _gate_expand():
    e = np.zeros((LANE, 3 * D_ATT), np.float32)
    for br in range(3):
        for h in range(N_HEADS):
            e[br * N_HEADS + h, br * D_ATT + h * HEAD_DIM:br * D_ATT + (h + 1) * HEAD_DIM] = 1.0
    return jnp.asarray(e)


def _outproj(x, o_lru, o_c, o_s, o_w, gl, mod, mod_spec, nl, na, n2, wo, tm):
    n = x.shape[0]
    tok = lambda w: pl.BlockSpec((tm, w), lambda i: (i, 0))
    const = lambda shape: pl.BlockSpec(shape, lambda i: (0, 0))
    gexp = _gate_expand()
    return pl.pallas_call(
        _outproj_kernel,
        grid=(n // tm,),
        in_specs=[tok(D_MODEL), tok(D_RNN), tok(D_ATT), tok(D_ATT), tok(D_ATT), tok(LANE),
                  mod_spec(2), mod_spec(3), mod_spec(4),
                  const(gexp.shape), const(nl.shape), const(na.shape), const(n2.shape),
                  const(wo.shape)],
        out_specs=[tok(D_MODEL), tok(D_MODEL)],
        out_shape=[jax.ShapeDtypeStruct((n, D_MODEL), F32),
                   jax.ShapeDtypeStruct((n, D_MODEL), BF16)],
        compiler_params=_params("arbitrary"),
    )(x, o_lru, o_c, o_s, o_w, gl, mod, mod, mod, gexp, nl, na, n2, wo)


def _extract_top(x, top, row):
    nrow = x.shape[0]
    vals = []
    rank = jnp.full(x.shape, float(top), F32)
    for k in range(top):
        m = jnp.max(x, axis=0, keepdims=True)
        idx = jnp.min(jnp.where(x == m, row, float(nrow)), axis=0, keepdims=True)
        pick = row == idx
        rank = jnp.where(pick, float(k), rank)
        x = jnp.where(pick, REMOVED, x)
        vals.append(m)
    return jnp.concatenate(vals, axis=0), rank


PAIR_COUNTS = tuple(PEER_TOPK // (a + 1) for a in range(PEER_TOPK))
N_PAIRS = sum(PAIR_COUNTS)
PAIR_ROWS = -(-N_PAIRS // 8) * 8


def _peer_score_kernel(h2_ref, wq_ref, bq_ref, sk_ref, bc_ref, a0_ref, r1_ref, e1_ref,
                       q_s, cand_s, *, tn):
    q_s[...] = (_nt(wq_ref[...], h2_ref[...]) + bq_ref[...]).astype(BF16)
    row = lax.broadcasted_iota(jnp.int32, (PEER_KEYS, tn), 0).astype(F32)
    crow = lax.broadcasted_iota(jnp.int32, (PAIR_ROWS, tn), 0).astype(F32)

    def head(h, carry):
        r0 = pl.multiple_of(h * 2 * PEER_KEYS, 2 * PEER_KEYS)
        s0 = _mm(sk_ref[2 * h], q_s[pl.ds(r0, PEER_KEYS), :])
        s1 = _mm(sk_ref[2 * h + 1], q_s[pl.ds(r0 + PEER_KEYS, PEER_KEYS), :])
        v0, rank0 = _extract_top(s0, PEER_TOPK, row)
        v1, rank1 = _extract_top(s1, PEER_TOPK, row)
        off = 0
        for a, cnt in enumerate(PAIR_COUNTS):
            cand_s[off:off + cnt, :] = v1[0:cnt] + v0[a:a + 1]
            off += cnt
        cand_s[N_PAIRS:, :] = jnp.full((PAIR_ROWS - N_PAIRS, tn), REMOVED, F32)
        best, rank_c = _extract_top(cand_s[...], PEER_TOPK, crow)
        z = jnp.sum(jnp.exp(best - best[0:1]), axis=0, keepdims=True)
        cand_s[...] = jnp.where(rank_c < PEER_TOPK, 1.0, 0.0)
        bc = jnp.zeros((PEER_KEYS, tn), F32)
        off = 0
        for a, cnt in enumerate(PAIR_COUNTS):
            kept_a = jnp.sum(cand_s[off:off + cnt, :], axis=0, keepdims=True)
            bc = jnp.where(rank0 == float(a), kept_a, bc)
            off += cnt
        bc_ref[h] = bc
        a0_ref[h] = jnp.where(rank0 < PEER_TOPK, jnp.exp(s0 - v0[0:1]), 0.0) * (0.5 / z)
        r1_ref[h] = rank1.astype(BF16)
        e1_ref[h] = jnp.exp(s1 - v1[0:1]).astype(BF16)
        return carry

    lax.fori_loop(0, PEER_HEADS, head, 0)


def _peer_score(h2, wq_t, bq_t, subkeys, tn):
    n = h2.shape[0]
    dq = wq_t.shape[0]
    tok = pl.BlockSpec((PEER_HEADS, PEER_KEYS, tn), lambda i: (0, 0, i))
    shp = jax.ShapeDtypeStruct((PEER_HEADS, PEER_KEYS, n), F32)
    shp16 = jax.ShapeDtypeStruct((PEER_HEADS, PEER_KEYS, n), BF16)
    return pl.pallas_call(
        functools.partial(_peer_score_kernel, tn=tn),
        grid=(n // tn,),
        in_specs=[pl.BlockSpec((tn, D_MODEL), lambda i: (i, 0)),
                  pl.BlockSpec(wq_t.shape, lambda i: (0, 0)),
                  pl.BlockSpec(bq_t.shape, lambda i: (0, 0)),
                  pl.BlockSpec(subkeys.shape, lambda i: (0, 0, 0))],
        out_specs=[tok, tok, tok, tok],
        out_shape=[shp, shp, shp16, shp16],
        scratch_shapes=[pltpu.VMEM((dq, tn), BF16), pltpu.VMEM((PAIR_ROWS, tn), F32)],
        compiler_params=_params("arbitrary"),
        name="peer_score",
    )(h2, wq_t, bq_t, subkeys)


GELU_C1 = math.sqrt(2.0 / math.pi)
GELU_C2 = 0.044715 * GELU_C1


def _peer_dense_kernel(h2_ref, u_ref, vt_ref, bc_ref, a0_ref, r1_ref, e1_ref, x1_ref,
                       gt2_ref, fg_ref, y_ref, acc_s, act_s, *, tn, eb, lc):
    e = pl.program_id(1)

    @pl.when(e == 0)
    def _():
        acc_s[...] = jnp.zeros(acc_s.shape, F32)

    z = _nt(u_ref[...], h2_ref[...]).astype(BF16)
    for il in range(eb // PEER_KEYS):
        i = e * (eb // PEER_KEYS) + il
        rs = slice(il * PEER_KEYS, (il + 1) * PEER_KEYS)
        for c0 in range(0, tn, lc):
            ls = slice(c0, c0 + lc)
            w = jnp.zeros((PEER_KEYS, lc), BF16)
            for h in range(PEER_HEADS):
                keep = r1_ref[h, :, ls] < bc_ref[h, pl.ds(i, 1), ls].astype(BF16)
                w = w + (jnp.where(keep, e1_ref[h, :, ls], 0.0)
                         * a0_ref[h, pl.ds(i, 1), ls].astype(BF16))
            zi = z[rs, ls]
            t = jnp.tanh(zi * (GELU_C1 + GELU_C2 * (zi * zi)))
            act_s[rs, ls] = (zi + zi * t) * w
    acc_s[...] += _mm(vt_ref[...], act_s[...])

    @pl.when(e == pl.num_programs(1) - 1)
    def _():
        x = x1_ref[...] + gt2_ref[...] * acc_s[...].T
        y_ref[...] = _rms(x, fg_ref[...])


def _peer_dense(h2, u, vt, scores, x1, mod, mod_spec2, fg, tn, eb):
    n = h2.shape[0]
    n_exp = u.shape[0]
    bc, a0, r1, e1 = scores
    tok3 = pl.BlockSpec((PEER_HEADS, PEER_KEYS, tn), lambda i, e: (0, 0, i))
    return pl.pallas_call(
        functools.partial(_peer_dense_kernel, tn=tn, eb=eb, lc=min(tn, 2 * LANE)),
        grid=(n // tn, n_exp // eb),
        in_specs=[pl.BlockSpec((tn, D_MODEL), lambda i, e: (i, 0)),
                  pl.BlockSpec((eb, D_MODEL), lambda i, e: (e, 0)),
                  pl.BlockSpec((D_MODEL, eb), lambda i, e: (0, e)),
                  tok3, tok3, tok3, tok3,
                  pl.BlockSpec((tn, D_MODEL), lambda i, e: (i, 0)),
                  mod_spec2(5),
                  pl.BlockSpec((1, D_MODEL), lambda i, e: (0, 0))],
        out_specs=pl.BlockSpec((tn, D_MODEL), lambda i, e: (i, 0)),
        out_shape=jax.ShapeDtypeStruct((n, D_MODEL), F32),
        scratch_shapes=[pltpu.VMEM((D_MODEL, tn), F32), pltpu.VMEM((eb, tn), BF16)],
        compiler_params=_params("arbitrary", "arbitrary"),
        name="peer_dense",
    )(h2, u, vt, bc, a0, r1, e1, x1, mod, fg)


def _block_diag(w):
    nblk, bw, _ = w.shape
    eye = jnp.eye(nblk, dtype=w.dtype)
    return jnp.einsum('nde,nm->ndme', w, eye).reshape(nblk * bw, nblk * bw)


def kernel(x_prompt, x_sample, cache_cmp_kv, cache_sel_kv, cache_win_kv, state_lru_h, state_conv, page_table, c_prompt, c_sample, ada_w, ada_b, norm1_g, norm2_g, w_in, conv_w, conv_b, lru_wa, lru_ba, lru_wi, lru_bi, lru_lambda, cmp_w1, cmp_w2, cmp_pos, out_norm_lru, out_norm_att, w_out, peer_wq, peer_bq, peer_subkeys, peer_u, peer_v, rel_bias, final_g):
    nb, t, d = x_prompt.shape
    ns = x_sample.shape[0]
    n_pages = page_table.shape[1]
    past = n_pages * PAGE_SIZE
    wb = cache_win_kv.shape[2]
    assert x_sample.shape[1] == 1 and ada_w.shape[0] == 1 and d == D_MODEL
    assert t % (SEL_TILES * LANE) == 0 and ns % LANE == 0 and past % LANE == 0
    assert wb == min(WINDOW, past)
    row = lambda v: v.reshape(1, -1)

    w_in_p = jnp.pad(w_in[0], ((0, 0), (0, sum(IN_SPLITS) - w_in.shape[2]))).astype(BF16)
    wg = jnp.concatenate([_block_diag(lru_wa[0]), _block_diag(lru_wi[0])], axis=1).astype(BF16)
    bg = row(jnp.concatenate([lru_ba[0], lru_bi[0]]))
    w1r = cmp_w1[0].reshape(2, 2, CMP_STRIDE, HEAD_DIM, HEAD_DIM)
    eye2 = jnp.eye(2, dtype=F32)
    w1g = jnp.einsum('cmsdh,ce->scdmeh', w1r, eye2).reshape(CMP_STRIDE * LANE, 2 * LANE).astype(BF16)
    w2bd = jnp.einsum('che,cf->chfe', cmp_w2[0], eye2).reshape(LANE, LANE).astype(BF16)
    wq_t = peer_wq[0].T.astype(BF16)
    bq_t = peer_bq[0].reshape(-1, 1)
    subk = peer_subkeys[0].reshape(2 * PEER_HEADS, PEER_KEYS, -1).astype(BF16)
    u_b = peer_u[0].astype(BF16)
    vt_b = peer_v[0].T.astype(BF16)
    wo_b = w_out[0].astype(BF16)

    n_c = nb + ns
    c_all = jnp.pad(jnp.concatenate([c_prompt, c_sample]), ((0, -n_c % 8), (0, 0)))
    mod = _ada(c_all, ada_w[0], ada_b[0])
    mod_p = mod[:nb].reshape(nb, 1, 6 * D_MODEL)
    mod_s = mod[nb:n_c]

    tm_p, tm_s = TOKEN_TILE, ns
    tn_p, tn_s = TOKEN_TILE, ns
    tn_sc = SCORE_TILE if ns % SCORE_TILE == 0 else LANE

    def mod_spec_p(tm):
        per = t // tm
        return lambda j: pl.BlockSpec((None, 1, D_MODEL), lambda i, *_: (i // per, 0, j))

    def mod_spec_s(tm):
        return lambda j: pl.BlockSpec((tm, D_MODEL), lambda i, *_: (i, j))

    thr_last = BUCKET_THR[-1]
    nd = max(min(t // LANE, -(-(thr_last + LANE - 1) // LANE) + 1), WINDOW // LANE + 2)
    bt, bc = _bias_prompt(rel_bias, t, nd)
    bsc, bsw, bss = _bias_sample(rel_bias, past, wb)
    bss = jnp.transpose(bss, (1, 0, 2))
    b0 = rel_bias[0].reshape(N_HEADS, 1)

    ph = _poshid(cmp_pos[0], cmp_w1[0])
    lru_args = (conv_w[0], row(conv_b[0]), wg, bg, row(lru_lambda[0]))
    norms = (row(out_norm_lru[0]), row(out_norm_att[0]), row(norm2_g[0]), wo_b)

    xp = x_prompt.reshape(nb * t, d)
    xr, yg, q, kvc, kvs, kvw, gl = _inproj(xp, mod_p, mod_spec_p(tm_p), row(norm1_g[0]), w_in_p, tm_p)
    seq = lambda a: a.reshape(nb, t, a.shape[-1])
    o_lru, h_p, conv_p = _lru_prompt(seq(xr), seq(yg), *lru_args, tc=LRU_TILE)
    ck = _compress_prompt(seq(kvc), w1g, w2bd, ph)
    o_c, sel = _cattn_prompt(seq(q), ck, bc, CATTN_TILE)
    o_s, o_w = _sattn_prompt(seq(q), sel, seq(kvs), seq(kvw), bt)
    flat = lambda a: a.reshape(nb * t, a.shape[-1])
    x1, h2 = _outproj(xp, flat(o_lru), flat(o_c), flat(o_s), flat(o_w), gl, mod_p,
                      mod_spec_p(tm_p), *norms, tm_p)
    scores = _peer_score(h2, wq_t, bq_t, subk, SCORE_TILE)
    y_p = _peer_dense(h2, u_b, vt_b, scores, x1, mod_p, mod_spec_p(tn_p), row(final_g), tn_p,
                      EXPERT_TILE)

    xs = x_sample.reshape(ns, d)
    xr_s, yg_s, q_s, kvc_s, kvs_s, kvw_s, gl_s = _inproj(xs, mod_s, mod_spec_s(tm_s),
                                                          row(norm1_g[0]), w_in_p, tm_s)
    o_lru_s, h_s = _lru_sample(xr_s, yg_s, state_conv[0], state_lru_h[0], *lru_args)
    rows_minor = lambda a: jnp.transpose(a, (0, 2, 3, 4, 1))
    cmp_t = rows_minor(cache_cmp_kv[0]).reshape(-1, N_KV, LANE, PAGE_SIZE)
    ck_s = _compress_sample(cmp_t, page_table, w1g, w2bd, ph)
    o_c_s, idx = _cattn_sample(q_s, ck_s, bsc, past)
    top_s = idx.shape[1] // N_KV
    o_s_s, o_w_s = _sattn_sample(q_s, idx, page_table, rows_minor(cache_sel_kv[0]), kvs_s, kvw_s,
                                 rows_minor(cache_win_kv[0]), bss, bsw, b0, top_s,
                                 past // SEL_BLOCK)
    x1_s, h2_s = _outproj(xs, o_lru_s, o_c_s.reshape(ns, D_ATT), o_s_s, o_w_s, gl_s, mod_s,
                          mod_spec_s(tm_s), *norms, tm_s)
    scores_s = _peer_score(h2_s, wq_t, bq_t, subk, tn_sc if ns % tn_sc == 0 else ns)
    y_s = _peer_dense(h2_s, u_b, vt_b, scores_s, x1_s, mod_s, mod_spec_s(tn_s), row(final_g), tn_s,
                      EXPERT_TILE)

    kv6 = lambda a, n, tt: a.reshape(1, n, tt, N_KV, 2, HEAD_DIM)
    win_p = kv6(kvw, nb, t)[:, :, t - min(WINDOW, t):]
    win_s = jnp.concatenate([cache_win_kv[:, :, 1:], kv6(kvw_s, ns, 1)], axis=2)[:, :, -wb:]
    conv_s = jnp.concatenate([state_conv[:, :, 1:], xr_s.reshape(1, ns, 1, D_RNN)], axis=2)
    return (y_p.reshape(nb, t, d), y_s.reshape(ns, 1, d),
            kv6(kvc, nb, t), kv6(kvc_s, ns, 1),
            kv6(kvs, nb, t), kv6(kvs_s, ns, 1),
            win_p, win_s,
            h_p.reshape(1, nb, D_RNN), h_s.reshape(1, ns, D_RNN),
            conv_p.reshape(1, nb, CONV_W - 1, D_RNN), conv_s)
```

```python
import functools
import math

import numpy as np
import jax
import jax.numpy as jnp
from jax import lax
from jax.experimental import pallas as pl
from jax.experimental.pallas import tpu as pltpu

F32 = jnp.float32
BF16 = jnp.bfloat16
HIGHEST = lax.Precision.HIGHEST

D_MODEL = 1024
D_RNN = 512
LRU_BLOCKS = 8
CONV_W = 4
LRU_C = 8.0
N_HEADS = 8
HEAD_DIM = 64
N_KV = 2
GROUP = 4
D_ATT = 512
KV_COLS = 2 * N_KV * HEAD_DIM
CMP_LEN = 32
CMP_STRIDE = 16
SEL_BLOCK = 64
SEL_TOP = 16
WINDOW = 512
SCALE = HEAD_DIM ** -0.5
LOG2E = math.log2(math.e)
NEG_INF = -1e30
REMOVED = -3e38
FORCE_SCORE = 1e6
NUM_BUCKETS = 32
MAX_DISTANCE = 1024
PEER_HEADS = 8
PEER_KEYS = 128
PEER_TOPK = 16
NORM_EPS = 1e-6
PAGE_SIZE = 128

LANE = 128
VMEM_LIMIT = 56 * 1024 * 1024

TOKEN_TILE = 512
SCORE_TILE = 512
EXPERT_TILE = 2048
CATTN_TILE = 512
LRU_TILE = 256


def _bucket_thresholds():
    d = np.arange(0, 1 << 15, dtype=np.int64)
    max_exact = NUM_BUCKETS // 2
    df = np.maximum(d, 1).astype(np.float32)
    ratio = (np.log(df / np.float32(max_exact)) / np.float32(math.log(MAX_DISTANCE / max_exact))
             * np.float32(NUM_BUCKETS - max_exact))
    large = np.minimum(max_exact + ratio.astype(np.int32), NUM_BUCKETS - 1)
    bucket = np.where(d < max_exact, d, large)
    assert np.all(np.diff(bucket) >= 0)
    return tuple(int(np.argmax(bucket >= k)) for k in range(1, NUM_BUCKETS))


BUCKET_THR = _bucket_thresholds()


def _params(*sem):
    return pltpu.CompilerParams(dimension_semantics=sem or None, vmem_limit_bytes=VMEM_LIMIT)


def _nt(a, b):
    return lax.dot_general(a, b, (((1,), (1,)), ((), ())), preferred_element_type=F32)


def _mm(a, b):
    return jnp.dot(a, b, preferred_element_type=F32)


def _rms(x, g):
    return x * lax.rsqrt(jnp.mean(x * x, axis=-1, keepdims=True) + NORM_EPS) * g


def _bias_lookup(dist, rb_ref, h):
    b = jnp.full(dist.shape, rb_ref[0, h], F32)
    for k in range(1, NUM_BUCKETS):
        b = jnp.where(dist >= BUCKET_THR[k - 1], rb_ref[k, h], b)
    return b


def _ada_kernel(c_ref, w_ref, b_ref, o_ref):
    c = c_ref[...]
    o_ref[...] = jnp.dot(jax.nn.silu(c), w_ref[...], precision=HIGHEST,
                         preferred_element_type=F32) + b_ref[...]


def _ada(c_all, w, b):
    rows, d = c_all.shape
    cols = w.shape[1]
    tn = 512
    return pl.pallas_call(
        _ada_kernel,
        grid=(cols // tn,),
        in_specs=[pl.BlockSpec((rows, d), lambda j: (0, 0)),
                  pl.BlockSpec((d, tn), lambda j: (0, j)),
                  pl.BlockSpec((1, tn), lambda j: (0, j))],
        out_specs=pl.BlockSpec((rows, tn), lambda j: (0, j)),
        out_shape=jax.ShapeDtypeStruct((rows, cols), F32),
        compiler_params=_params("arbitrary"),
    )(c_all, w, b.reshape(1, cols))


def _bias_prompt_kernel(rb_ref, bt_ref, bc_ref, *, nd, t, nc):
    h = pl.program_id(0)
    row = lax.broadcasted_iota(jnp.int32, (LANE, LANE), 0)
    col = lax.broadcasted_iota(jnp.int32, (LANE, LANE), 1)
    for dl in range(nd):
        dist = dl * LANE + col - row
        b = _bias_lookup(dist, rb_ref, h) * LOG2E
        bt_ref[0, dl] = jnp.where(dist >= 0, b, NEG_INF)
        bt_ref[1, dl] = jnp.where((dist >= 0) & (dist < WINDOW), b, NEG_INF)
    rq = lax.broadcasted_iota(jnp.int32, (LANE, nc), 0)
    ends = lax.broadcasted_iota(jnp.int32, (LANE, nc), 1) * CMP_STRIDE + (CMP_LEN - 1)

    def chunk(i, carry):
        r0 = pl.multiple_of(i * LANE, LANE)
        bc_ref[pl.ds(r0, LANE), :] = _bias_lookup(r0 + rq - ends, rb_ref, h)
        return carry

    lax.fori_loop(0, t // LANE, chunk, 0)


def _bias_prompt(rel_bias, t, nd):
    nc = t // CMP_STRIDE
    return pl.pallas_call(
        functools.partial(_bias_prompt_kernel, nd=nd, t=t, nc=nc),
        grid=(N_HEADS,),
        in_specs=[pl.BlockSpec(memory_space=pltpu.SMEM)],
        out_specs=[pl.BlockSpec((None, 2, nd, LANE, LANE), lambda h: (h, 0, 0, 0, 0)),
                   pl.BlockSpec((None, t, nc), lambda h: (h, 0, 0))],
        out_shape=[jax.ShapeDtypeStruct((N_HEADS, 2, nd, LANE, LANE), F32),
                   jax.ShapeDtypeStruct((N_HEADS, t, nc), F32)],
        compiler_params=_params("arbitrary"),
    )(rel_bias)


def _bias_sample_kernel(rb_ref, bsc_ref, bsw_ref, bss_ref, *, past, ncs, wb, n_pages):
    ends = lax.broadcasted_iota(jnp.int32, (1, ncs), 1) * CMP_STRIDE + (CMP_LEN - 1)
    jw = lax.broadcasted_iota(jnp.int32, (1, wb), 1)
    kpos = (lax.broadcasted_iota(jnp.int32, (n_pages, PAGE_SIZE), 0) * PAGE_SIZE
            + lax.broadcasted_iota(jnp.int32, (n_pages, PAGE_SIZE), 1))
    for h in range(N_HEADS):
        bsc_ref[h:h + 1, :] = _bias_lookup(past - ends, rb_ref, h)
        bsw_ref[h:h + 1, :] = _bias_lookup(wb - jw, rb_ref, h)
        bss_ref[h] = _bias_lookup(past - kpos, rb_ref, h)


def _bias_sample(rel_bias, past, wb):
    ncs = past // CMP_STRIDE
    n_pages = past // PAGE_SIZE
    return pl.pallas_call(
        functools.partial(_bias_sample_kernel, past=past, ncs=ncs, wb=wb, n_pages=n_pages),
        in_specs=[pl.BlockSpec(memory_space=pltpu.SMEM)],
        out_shape=[jax.ShapeDtypeStruct((N_HEADS, ncs), F32),
                   jax.ShapeDtypeStruct((N_HEADS, wb), F32),
                   jax.ShapeDtypeStruct((N_HEADS, n_pages, PAGE_SIZE), F32)],
        compiler_params=_params(),
        name="bias_sample",
    )(rel_bias)


IN_SPLITS = (D_RNN, D_RNN, D_ATT, KV_COLS, KV_COLS, KV_COLS, LANE)


def _inproj_kernel(x_ref, sh_ref, sc_ref, g_ref, w_ref, *out_refs):
    h = _rms(x_ref[...], g_ref[...]) * (1.0 + sc_ref[...]) + sh_ref[...]
    z = _mm(h.astype(BF16), w_ref[...])
    off = 0
    for o_ref, wdt in zip(out_refs, IN_SPLITS):
        o_ref[...] = z[:, off:off + wdt]
        off += wdt


def _inproj(x, mod, mod_spec, g1, w_in_p, tm):
    n = x.shape[0]
    cols = w_in_p.shape[1]
    return pl.pallas_call(
        _inproj_kernel,
        grid=(n // tm,),
        in_specs=[pl.BlockSpec((tm, D_MODEL), lambda i: (i, 0)),
                  mod_spec(0), mod_spec(1),
                  pl.BlockSpec((1, D_MODEL), lambda i: (0, 0)),
                  pl.BlockSpec((D_MODEL, cols), lambda i: (0, 0))],
        out_specs=[pl.BlockSpec((tm, wdt), lambda i: (i, 0)) for wdt in IN_SPLITS],
        out_shape=[jax.ShapeDtypeStruct((n, wdt), F32) for wdt in IN_SPLITS],
        compiler_params=_params("arbitrary"),
    )(x, mod, mod, g1, w_in_p)


def _lru_gates(xc, wg_ref, bg_ref, lam_ref):
    gates = _mm(xc.astype(BF16), wg_ref[...]) + bg_ref[...]
    r = jax.nn.sigmoid(gates[:, :D_RNN])
    i = jax.nn.sigmoid(gates[:, D_RNN:])
    log_a = -LRU_C * r * jax.nn.softplus(-lam_ref[...])
    a = jnp.exp(log_a)
    u = jnp.sqrt(-jnp.tanh(log_a) * (a * a + 1.0)) * i * xc
    return a, u


def _lru_prompt_kernel(xr_ref, yg_ref, cw_ref, cb_ref, wg_ref, bg_ref, lam_ref,
                       o_ref, hl_ref, cv_ref, xp_s, a_s, u_s, hs_s, h_s, *, nb, tc):
    t = pl.program_id(0)

    @pl.when(t == 0)
    def _():
        xp_s[:, 0:8, :] = jnp.zeros((nb, 8, D_RNN), F32)
        h_s[...] = jnp.zeros((nb, 1, D_RNN), F32)

    @pl.when(t > 0)
    def _():
        xp_s[:, 0:8, :] = xp_s[:, tc:tc + 8, :]

    xp_s[:, 8:, :] = xr_ref[...]
    xc = xp_s[:, 5:5 + tc, :] * cw_ref[0:1, :]
    for k in range(1, CONV_W):
        xc = xc + xp_s[:, 5 + k:5 + k + tc, :] * cw_ref[k:k + 1, :]
    xc = xc + cb_ref[...]
    a, u = _lru_gates(xc.reshape(nb * tc, D_RNN), wg_ref, bg_ref, lam_ref)
    a_s[...] = a.reshape(nb, tc, D_RNN)
    u_s[...] = u.reshape(nb, tc, D_RNN)

    def step(tt, h):
        h = a_s[:, pl.ds(tt, 1), :] * h + u_s[:, pl.ds(tt, 1), :]
        hs_s[:, pl.ds(tt, 1), :] = h
        return h

    h = lax.fori_loop(0, tc, step, h_s[...], unroll=8)
    h_s[...] = h
    o_ref[...] = hs_s[...] * jax.nn.gelu(yg_ref[...])
    hl_ref[...] = h
    cv_ref[...] = xp_s[:, tc + 8 - (CONV_W - 1):tc + 8, :]


def _lru_prompt(xr, yg, cw, cb, wg, bg, lam, tc):
    nb, t, _ = xr.shape
    blk = pl.BlockSpec((nb, tc, D_RNN), lambda i: (0, i, 0))
    full = lambda shape: pl.BlockSpec(shape, lambda i: tuple(0 for _ in shape))
    return pl.pallas_call(
        functools.partial(_lru_prompt_kernel, nb=nb, tc=tc),
        grid=(t // tc,),
        in_specs=[blk, blk, full(cw.shape), full(cb.shape), full(wg.shape), full(bg.shape),
                  full(lam.shape)],
        out_specs=[blk, full((nb, 1, D_RNN)), full((nb, CONV_W - 1, D_RNN))],
        out_shape=[jax.ShapeDtypeStruct((nb, t, D_RNN), F32),
                   jax.ShapeDtypeStruct((nb, 1, D_RNN), F32),
                   jax.ShapeDtypeStruct((nb, CONV_W - 1, D_RNN), F32)],
        scratch_shapes=[pltpu.VMEM((nb, tc + 8, D_RNN), F32), pltpu.VMEM((nb, tc, D_RNN), F32),
                        pltpu.VMEM((nb, tc, D_RNN), F32), pltpu.VMEM((nb, tc, D_RNN), F32),
                        pltpu.VMEM((nb, 1, D_RNN), F32)],
        compiler_params=_params("arbitrary"),
    )(xr, yg, cw, cb, wg, bg, lam)


def _lru_sample_kernel(x_ref, yg_ref, b0_ref, b1_ref, b2_ref, h0_ref, cw_ref, cb_ref, wg_ref,
                       bg_ref, lam_ref, o_ref, hn_ref):
    xc = b0_ref[...] * cw_ref[0:1, :]
    xc = xc + b1_ref[...] * cw_ref[1:2, :]
    xc = xc + b2_ref[...] * cw_ref[2:3, :]
    xc = xc + x_ref[...] * cw_ref[3:4, :]
    xc = xc + cb_ref[...]
    a, u = _lru_gates(xc, wg_ref, bg_ref, lam_ref)
    h = a * h0_ref[...] + u
    hn_ref[...] = h
    o_ref[...] = h * jax.nn.gelu(yg_ref[...])


def _lru_sample(x, yg, buf, h0, cw, cb, wg, bg, lam):
    s = x.shape[0]
    return pl.pallas_call(
        _lru_sample_kernel,
        out_shape=[jax.ShapeDtypeStruct((s, D_RNN), F32), jax.ShapeDtypeStruct((s, D_RNN), F32)],
        compiler_params=_params(),
    )(x, yg, buf[:, 0], buf[:, 1], buf[:, 2], h0, cw, cb, wg, bg, lam)


CHUNK_COLS = CMP_STRIDE * KV_COLS


def _poshid_kernel(pos_ref, w_ref, o_ref):
    parts = [jnp.dot(pos_ref[c], w_ref[c], precision=HIGHEST, preferred_element_type=F32)[0:1]
             for c in range(2)]
    o_ref[...] = jnp.concatenate(parts, axis=1)


def _poshid(cmp_pos, cmp_w1):
    k = CMP_LEN * HEAD_DIM
    pos = jnp.broadcast_to(cmp_pos.reshape(2, 1, k), (2, 8, k))
    return pl.pallas_call(
        _poshid_kernel,
        out_shape=jax.ShapeDtypeStruct((1, 2 * HEAD_DIM), F32),
        compiler_params=_params(),
    )(pos, cmp_w1.reshape(2, k, HEAD_DIM))


def _compress_rows(x_s, n, w1_ref, w2_ref, ph_ref):
    outs = []
    for g in range(N_KV):
        xg = jnp.concatenate(
            [x_s[g, pl.ds(s, n, stride=CMP_STRIDE), :].astype(BF16) for s in range(CMP_STRIDE)],
            axis=1)
        p = _mm(xg, w1_ref[...])
        hid = ph_ref[...] + p[:, :LANE] + pltpu.roll(p[:, LANE:], n - 1, axis=0)
        outs.append(_mm(jax.nn.gelu(hid).astype(BF16), w2_ref[...]))
    return jnp.concatenate(outs, axis=1)


def _compress_prompt_kernel(x_ref, w1_ref, w2_ref, ph_ref, o_ref, x_s, *, n):
    for g in range(N_KV):
        x_s[g] = x_ref[:, g * LANE:(g + 1) * LANE]
    o_ref[...] = _compress_rows(x_s, n, w1_ref, w2_ref, ph_ref)


def _compress_prompt(kvc, w1g, w2bd, ph):
    nb, t, _ = kvc.shape
    n = t // CMP_STRIDE
    return pl.pallas_call(
        functools.partial(_compress_prompt_kernel, n=n),
        grid=(nb,),
        in_specs=[pl.BlockSpec((None, t, KV_COLS), lambda b: (b, 0, 0)),
                  pl.BlockSpec(w1g.shape, lambda b: (0, 0)),
                  pl.BlockSpec(w2bd.shape, lambda b: (0, 0)),
                  pl.BlockSpec(ph.shape, lambda b: (0, 0))],
        out_specs=pl.BlockSpec((None, n, KV_COLS), lambda b: (b, 0, 0)),
        out_shape=jax.ShapeDtypeStruct((nb, n, KV_COLS), F32),
        scratch_shapes=[pltpu.VMEM((N_KV, t, LANE), F32)],
        compiler_params=_params("arbitrary"),
        name="compress_prompt",
    )(kvc, w1g, w2bd, ph)


MAX_PAGES_PER_STEP = 32


def _compress_sample_kernel(pt_ref, *refs, n, pps):
    page_refs = refs[:pps]
    w1_ref, w2_ref, ph_ref, o_ref, x_s = refs[pps:]
    p = pl.program_id(1)
    for k in range(pps):
        r0 = pl.multiple_of((p * pps + k) * PAGE_SIZE, PAGE_SIZE)
        for g in range(N_KV):
            x_s[g, pl.ds(r0, PAGE_SIZE), :] = page_refs[k][g].T

    @pl.when(p == pl.num_programs(1) - 1)
    def _():
        o_ref[...] = _compress_rows(x_s, n, w1_ref, w2_ref, ph_ref)


def _compress_sample(cache_t, page_table, w1g, w2bd, ph):
    s, n_pages = page_table.shape
    n = n_pages * PAGE_SIZE // CMP_STRIDE
    pps = math.gcd(n_pages, MAX_PAGES_PER_STEP)

    def page_spec(k):
        return pl.BlockSpec((None, N_KV, LANE, PAGE_SIZE),
                            lambda b, p, pt: (pt[b, p * pps + k], 0, 0, 0))

    const = lambda shape: pl.BlockSpec(shape, lambda b, p, pt: (0, 0))
    return pl.pallas_call(
        functools.partial(_compress_sample_kernel, n=n, pps=pps),
        grid_spec=pltpu.PrefetchScalarGridSpec(
            num_scalar_prefetch=1,
            grid=(s, n_pages // pps),
            in_specs=[page_spec(k) for k in range(pps)]
            + [const(w1g.shape), const(w2bd.shape), const(ph.shape)],
            out_specs=pl.BlockSpec((None, n, KV_COLS), lambda b, p, pt: (b, 0, 0)),
            scratch_shapes=[pltpu.VMEM((N_KV, n_pages * PAGE_SIZE, LANE), F32)]),
        out_shape=jax.ShapeDtypeStruct((s, n, KV_COLS), F32),
        compiler_params=_params("arbitrary", "arbitrary"),
        name="compress_sample",
    )(page_table, *([cache_t] * pps), w1g, w2bd, ph)


def _select_top(sc, top, want_idx):
    rows, nlane = sc.shape
    j = lax.broadcasted_iota(jnp.int32, (rows, nlane), 1).astype(F32)
    mask = jnp.zeros((rows, nlane), F32)
    picks = []
    for _ in range(top):
        m = jnp.max(sc, axis=-1, keepdims=True)
        idx = jnp.min(jnp.where(sc == m, j, float(nlane)), axis=-1, keepdims=True)
        pick = j == idx
        mask = jnp.where(pick, 1.0, mask)
        sc = jnp.where(pick, REMOVED, sc)
        picks.append(idx)
    if want_idx:
        return mask, jnp.concatenate(picks, axis=1).astype(jnp.int32)
    return mask


def _selection_scores(score, blk, qpos, nsel):
    cur = qpos // SEL_BLOCK
    forced = (blk == 0) | (blk == cur) | (blk == cur - 1)
    sc = jnp.where(forced, FORCE_SCORE, jnp.where(blk * SEL_BLOCK <= qpos, score, -1.0))
    return jnp.where(blk < nsel, sc, REMOVED)


def _softmax_rows(s, valid):
    s = jnp.where(valid, s, NEG_INF)
    p = jnp.exp(s - jnp.max(s, axis=-1, keepdims=True))
    return p / jnp.sum(p, axis=-1, keepdims=True)


def _cattn_prompt_kernel(q_ref, ck_ref, bc_ref, covt_ref, oc_ref, sel_ref, *, tq, nc, nsel, top):
    qpos = pl.program_id(1) * tq + lax.broadcasted_iota(jnp.int32, (tq, 1), 0)
    n_idx = lax.broadcasted_iota(jnp.int32, (1, nc), 1)
    valid = (qpos >= n_idx * CMP_STRIDE + (CMP_LEN - 1)) & (n_idx < nc - 1)
    anyv = jnp.max(valid.astype(F32), axis=-1, keepdims=True)
    blk_t = lax.broadcasted_iota(jnp.int32, (nsel, 1), 0)
    qpos_t = pl.program_id(1) * tq + lax.broadcasted_iota(jnp.int32, (1, tq), 1)
    row_t = lax.broadcasted_iota(jnp.int32, (nsel, tq), 0).astype(F32)
    outs, masks = [], []
    for g in range(N_KV):
        kv = ck_ref[:, g * LANE:(g + 1) * LANE].astype(BF16)
        k, v = kv[:, :HEAD_DIM], kv[:, HEAD_DIM:]
        psum = jnp.zeros((tq, nc), F32)
        for r in range(GROUP):
            h = g * GROUP + r
            qh = q_ref[:, h * HEAD_DIM:(h + 1) * HEAD_DIM].astype(BF16)
            p = _softmax_rows(_nt(qh, k) * SCALE + bc_ref[h], valid) * anyv
            outs.append(_mm(p.astype(BF16), v))
            psum = psum + p
        score_t = lax.dot_general(covt_ref[...], psum, (((1,), (1,)), ((), ())),
                                  precision=HIGHEST, preferred_element_type=F32)
        _, rank = _extract_top(_selection_scores(score_t, blk_t, qpos_t, nsel), top, row_t)
        masks.append(jnp.where(rank < top, 1.0, 0.0).T)
    oc_ref[...] = jnp.concatenate(outs, axis=1)
    sel_ref[...] = jnp.concatenate(masks, axis=1)


def _covers(n_cmp_rows, n_cmp, n_sel, n_sel_cols):
    start = np.arange(n_cmp_rows)[:, None] * CMP_STRIDE
    blk = np.arange(n_sel_cols)[None, :]
    cov = ((start < (blk + 1) * SEL_BLOCK) & (start + CMP_LEN > blk * SEL_BLOCK)
           & (np.arange(n_cmp_rows)[:, None] < n_cmp) & (blk < n_sel))
    return jnp.asarray(cov.astype(np.float32))


def _cattn_prompt(q, ck, bc, tq):
    nb, t, _ = q.shape
    nc = ck.shape[1]
    nsel = t // SEL_BLOCK
    top = min(SEL_TOP, nsel)
    cov = _covers(nc, nc - 1, nsel, nsel).T
    return pl.pallas_call(
        functools.partial(_cattn_prompt_kernel, tq=tq, nc=nc, nsel=nsel, top=top),
        grid=(nb, t // tq),
        in_specs=[pl.BlockSpec((None, tq, D_ATT), lambda b, i: (b, i, 0)),
                  pl.BlockSpec((None, nc, KV_COLS), lambda b, i: (b, 0, 0)),
                  pl.BlockSpec((N_HEADS, tq, nc), lambda b, i: (0, i, 0)),
                  pl.BlockSpec(cov.shape, lambda b, i: (0, 0))],
        out_specs=[pl.BlockSpec((None, tq, D_ATT), lambda b, i: (b, i, 0)),
                   pl.BlockSpec((None, tq, N_KV * nsel), lambda b, i: (b, i, 0))],
        out_shape=[jax.ShapeDtypeStruct((nb, t, D_ATT), F32),
                   jax.ShapeDtypeStruct((nb, t, N_KV * nsel), F32)],
        compiler_params=_params("arbitrary", "arbitrary"),
    )(q, ck, bc, cov)


SAMPLES_PER_STEP = 8


def _cattn_sample_kernel(q_ref, ck_ref, bsc_ref, cov_ref, oc_ref, idx_ref, *, past, nc, nsel, top):
    n_idx = lax.broadcasted_iota(jnp.int32, (1, nc), 1)
    valid = (past >= n_idx * CMP_STRIDE + (CMP_LEN - 1)) & (n_idx < nc - 1)
    anyv = jnp.max(valid.astype(F32), axis=-1, keepdims=True)
    nlane = cov_ref.shape[1]
    blk = lax.broadcasted_iota(jnp.int32, (1, nlane), 1)
    idxs = []
    for g in range(N_KV):
        psums = []
        for i in range(SAMPLES_PER_STEP):
            kv = ck_ref[i, :, g * LANE:(g + 1) * LANE].astype(BF16)
            q4 = q_ref[i, g * GROUP:(g + 1) * GROUP, :].astype(BF16)
            s = _nt(q4, kv[:, :HEAD_DIM]) * SCALE + bsc_ref[g * GROUP:(g + 1) * GROUP, :]
            p = _softmax_rows(s, valid) * anyv
            oc_ref[i, g * GROUP:(g + 1) * GROUP, :] = _mm(p.astype(BF16), kv[:, HEAD_DIM:])
            psums.append(jnp.sum(p, axis=0, keepdims=True))
        score = jnp.dot(jnp.concatenate(psums, axis=0), cov_ref[...], precision=HIGHEST,
                        preferred_element_type=F32)
        _, idx = _select_top(_selection_scores(score, blk, past, nsel), top, True)
        idxs.append(idx)
    idx_ref[...] = jnp.concatenate(idxs, axis=1)


def _cattn_sample(q, ck, bsc, past):
    s = q.shape[0]
    nc = ck.shape[1]
    nsel = -(-(past + 1) // SEL_BLOCK)
    top = min(SEL_TOP, nsel)
    nlane = -(-nsel // LANE) * LANE
    cov = _covers(nc, nc - 1, nsel, nlane)
    g = SAMPLES_PER_STEP
    return pl.pallas_call(
        functools.partial(_cattn_sample_kernel, past=past, nc=nc, nsel=nsel, top=top),
        grid=(s // g,),
        in_specs=[pl.BlockSpec((g, N_HEADS, HEAD_DIM), lambda i: (i, 0, 0)),
                  pl.BlockSpec((g, nc, KV_COLS), lambda i: (i, 0, 0)),
                  pl.BlockSpec(bsc.shape, lambda i: (0, 0)),
                  pl.BlockSpec(cov.shape, lambda i: (0, 0))],
        out_specs=[pl.BlockSpec((g, N_HEADS, HEAD_DIM), lambda i: (i, 0, 0)),
                   pl.BlockSpec((g, N_KV * top), lambda i: (i, 0))],
        out_shape=[jax.ShapeDtypeStruct((s, N_HEADS, HEAD_DIM), F32),
                   jax.ShapeDtypeStruct((s, N_KV * top), jnp.int32)],
        compiler_params=_params("arbitrary"),
    )(q.reshape(s, N_HEADS, HEAD_DIM), ck, bsc, cov)


SEL_TILES = 8


def _flash_tiles(qt, kvs, biases, pens, m_ref, acc_ref):
    kv = jnp.concatenate(kvs, axis=0)
    s = _mm(kv.astype(BF16), qt) + jnp.concatenate(biases, axis=0)
    if pens is not None:
        s = s + jnp.concatenate([jnp.concatenate(pens, axis=0)] * GROUP, axis=1)
    m_old = m_ref[...]
    m_new = jnp.maximum(m_old, jnp.max(s, axis=0, keepdims=True))
    row = lax.broadcasted_iota(jnp.int32, (LANE, kv.shape[0]), 0)
    ones_vt = jnp.where(row < HEAD_DIM, 1.0, kv.T).astype(BF16)
    acc_ref[...] = (jnp.exp2(m_old - m_new) * acc_ref[...]
                    + _mm(ones_vt, jnp.exp2(s - m_new).astype(BF16)))
    m_ref[...] = m_new


def _sattn_prompt_kernel(q_ref, sel_ref, kvs_ref, kvw_ref, bt_ref, os_ref, ow_ref,
                         m_s, acc_s, *, nsel, nd, npw):
    c = pl.program_id(1)
    ej = lax.broadcasted_iota(jnp.int32, (LANE, nsel), 1)
    ek = lax.broadcasted_iota(jnp.int32, (LANE, nsel), 0) // SEL_BLOCK
    q_t = (q_ref[...] * (SCALE * LOG2E)).T
    zpad = jnp.zeros((LANE - HEAD_DIM, GROUP * LANE), F32)
    qts = [jnp.concatenate(
        [jnp.concatenate([q_t[(g * GROUP + r) * HEAD_DIM:(g * GROUP + r + 1) * HEAD_DIM]
                          for r in range(GROUP)], axis=1), zpad], axis=0).astype(BF16)
        for g in range(N_KV)]
    selgs = [sel_ref[:, g * nsel:(g + 1) * nsel].astype(BF16) for g in range(N_KV)]

    def reset():
        m_s[...] = jnp.full(m_s.shape, NEG_INF, F32)
        acc_s[...] = jnp.zeros(acc_s.shape, F32)

    def result():
        outs = []
        for g in range(N_KV):
            acc = acc_s[g]
            o = (acc / acc[0:1, :]).T[:, HEAD_DIM:]
            outs += [o[r * LANE:(r + 1) * LANE] for r in range(GROUP)]
        return jnp.concatenate(outs, axis=1)

    def bias_tiles(kind, g, kb):
        dl = jnp.clip(c - kb, 0, nd - 1)
        if kind == 1:
            dl = jnp.where(kb >= 0, dl, 2 * npw - 1)
        return jnp.concatenate([bt_ref[g * GROUP + r, kind, dl] for r in range(GROUP)], axis=1)

    def kv_tile(ref, g, kb):
        k0 = pl.multiple_of(jnp.maximum(kb, 0) * LANE, LANE)
        return ref[pl.ds(k0, LANE), g * LANE:(g + 1) * LANE]

    def update(kind, ref, g, kbs, pens):
        _flash_tiles(qts[g], [kv_tile(ref, g, kb) for kb in kbs],
                     [bias_tiles(kind, g, kb) for kb in kbs], pens, m_s.at[g], acc_s.at[g])

    def sel_step(kp, carry):
        kbs = tuple(SEL_TILES * kp + k for k in range(SEL_TILES))
        for g in range(N_KV):
            pens = []
            for kb in kbs:
                first = jnp.where(kb <= c, (LANE // SEL_BLOCK) * kb, -(LANE // SEL_BLOCK))
                expand = (ej == first + ek).astype(BF16)
                chosen = _nt(expand, selgs[g]) > 0.5
                pens.append(jnp.where(chosen, 0.0, NEG_INF))
            update(0, kvs_ref, g, kbs, pens)
        return carry

    reset()
    lax.fori_loop(0, (c + SEL_TILES) // SEL_TILES, sel_step, 0)
    os_ref[...] = result()

    reset()
    kbs = tuple(c - (2 * npw - 1) + k for k in range(2 * npw))
    for g in range(N_KV):
        update(1, kvw_ref, g, kbs, None)
    ow_ref[...] = result()


def _sattn_prompt(q, sel, kvs, kvw, bt):
    nb, t, _ = q.shape
    nsel = t // SEL_BLOCK
    nd = bt.shape[2]
    npw = (WINDOW // LANE + 2) // 2
    qblk = pl.BlockSpec((None, LANE, D_ATT), lambda b, i: (b, i, 0))
    kvblk = pl.BlockSpec((None, t, KV_COLS), lambda b, i: (b, 0, 0))
    return pl.pallas_call(
        functools.partial(_sattn_prompt_kernel, nsel=nsel, nd=nd, npw=npw),
        grid=(nb, t // LANE),
        in_specs=[qblk, pl.BlockSpec((None, LANE, N_KV * nsel), lambda b, i: (b, i, 0)),
                  kvblk, kvblk, pl.BlockSpec(bt.shape, lambda b, i: (0, 0, 0, 0, 0))],
        out_specs=[qblk, qblk],
        out_shape=[jax.ShapeDtypeStruct((nb, t, D_ATT), F32)] * 2,
        scratch_shapes=[pltpu.VMEM((N_KV, 1, GROUP * LANE), F32),
                        pltpu.VMEM((N_KV, LANE, GROUP * LANE), F32)],
        compiler_params=_params("arbitrary", "arbitrary"),
        name="sattn_prompt",
    )(q, sel, kvs, kvw, bt)


def _sattn_sample_kernel(idx_ref, pt_ref, *refs, top, npb, n_pages, wb):
    nblk = N_KV * top
    blk_refs = refs[:nblk]
    (q_ref, ksn_ref, kwn_ref, win_ref, bss_ref, bsw_ref, b0_ref, os_ref, ow_ref) = refs[nblk:]
    b = pl.program_id(0)
    halves = PAGE_SIZE // SEL_BLOCK
    half = lax.broadcasted_iota(jnp.int32, (1, PAGE_SIZE), 1) // SEL_BLOCK
    jw = lax.broadcasted_iota(jnp.int32, (1, wb), 1)
    for g in range(N_KV):
        hs = slice(g * GROUP, (g + 1) * GROUP)
        q4f = q_ref[hs, :]
        q4 = q4f.astype(BF16)
        b0 = b0_ref[hs, :]

        def new_logit(new_ref):
            kn = new_ref[:, g * LANE:g * LANE + HEAD_DIM]
            return jnp.sum(q4f * kn, axis=-1, keepdims=True) * SCALE + b0

        def new_value(new_ref):
            return new_ref[:, g * LANE + HEAD_DIM:(g + 1) * LANE]

        logits, vts = [], []
        for j in range(top):
            ib = idx_ref[b, g * top + j]
            page = jnp.minimum(ib // halves, n_pages - 1)
            s = _mm(q4, blk_refs[g * top + j][0].astype(BF16)) * SCALE + bss_ref[page, hs, :]
            logits.append(jnp.where(half == jnp.where(ib < npb, ib % halves, -1), s, NEG_INF))
            vts.append(blk_refs[g * top + j][1].astype(BF16))
        s_new = new_logit(ksn_ref)
        m = s_new
        for s in logits:
            m = jnp.maximum(m, jnp.max(s, axis=-1, keepdims=True))
        p_new = jnp.exp(s_new - m)
        l = p_new
        acc = p_new * new_value(ksn_ref)
        for s, vt in zip(logits, vts):
            p = jnp.exp(s - m)
            l = l + jnp.sum(p, axis=-1, keepdims=True)
            acc = acc + _nt(p.astype(BF16), vt)
        os_ref[hs, :] = acc / l

        s = _mm(q4, win_ref[g, 0].astype(BF16)) * SCALE + bsw_ref[hs, :]
        s = jnp.where(wb - jw < WINDOW, s, NEG_INF)
        s_new = new_logit(kwn_ref)
        m = jnp.maximum(s_new, jnp.max(s, axis=-1, keepdims=True))
        p = jnp.exp(s - m)
        p_new = jnp.exp(s_new - m)
        l = p_new + jnp.sum(p, axis=-1, keepdims=True)
        acc = p_new * new_value(kwn_ref) + _nt(p.astype(BF16), win_ref[g, 1].astype(BF16))
        ow_ref[hs, :] = acc / l


def _sattn_sample(q, idx, page_table, sel_t, ks_new, kw_new, win_t, bss, bsw, b0, top, npb):
    s, n_pages = page_table.shape
    wb = win_t.shape[-1]
    halves = PAGE_SIZE // SEL_BLOCK

    def blk_spec(g, j):
        def imap(b, idx_r, pt_r):
            ib = idx_r[b, g * top + j]
            return (pt_r[b, jnp.minimum(ib // halves, n_pages - 1)], g, 0, 0, 0)
        return pl.BlockSpec((None, None, 2, HEAD_DIM, PAGE_SIZE), imap)

    per_b = lambda shape: pl.BlockSpec((None,) + shape,
                                       lambda b, i_r, p_r: (b,) + tuple(0 for _ in shape))
    const = lambda shape: pl.BlockSpec(shape, lambda b, i_r, p_r: tuple(0 for _ in shape))
    out = pl.pallas_call(
        functools.partial(_sattn_sample_kernel, top=top, npb=npb, n_pages=n_pages, wb=wb),
        grid_spec=pltpu.PrefetchScalarGridSpec(
            num_scalar_prefetch=2,
            grid=(s,),
            in_specs=[blk_spec(g, j) for g in range(N_KV) for j in range(top)]
            + [per_b((N_HEADS, HEAD_DIM)), per_b((1, KV_COLS)), per_b((1, KV_COLS)),
               per_b(win_t.shape[1:]), const(bss.shape), const(bsw.shape), const(b0.shape)],
            out_specs=[per_b((N_HEADS, HEAD_DIM)), per_b((N_HEADS, HEAD_DIM))]),
        out_shape=[jax.ShapeDtypeStruct((s, N_HEADS, HEAD_DIM), F32)] * 2,
        compiler_params=_params("arbitrary"),
        name="sattn_sample",
    )(idx, page_table, *([sel_t] * (N_KV * top)), q.reshape(s, N_HEADS, HEAD_DIM),
      ks_new.reshape(s, 1, KV_COLS), kw_new.reshape(s, 1, KV_COLS), win_t, bss, bsw, b0)
    return out[0].reshape(s, D_ATT), out[1].reshape(s, D_ATT)


def _outproj_kernel(x_ref, ol_ref, oc_ref, os_ref, ow_ref, gl_ref, gt1_ref, sh2_ref, sc2_ref,
                    gexp_ref, nl_ref, na_ref, n2_ref, wo_ref, x1_ref, h2_ref):
    gates = jax.nn.sigmoid(jnp.dot(gl_ref[...], gexp_ref[...], precision=HIGHEST,
                                   preferred_element_type=F32))
    o_att = (gates[:, :D_ATT] * oc_ref[...] + gates[:, D_ATT:2 * D_ATT] * os_ref[...]
             + gates[:, 2 * D_ATT:] * ow_ref[...])
    n_lru = _rms(ol_ref[...], nl_ref[...]).astype(BF16)
    n_att = _rms(o_att, na_ref[...]).astype(BF16)
    mixed = _mm(n_lru, wo_ref[:D_RNN, :]) + _mm(n_att, wo_ref[D_RNN:, :])
    x1 = x_ref[...] + gt1_ref[...] * mixed
    x1_ref[...] = x1
    h2_ref[...] = (_rms(x1, n2_ref[...]) * (1.0 + sc2_ref[...]) + sh2_ref[...]).astype(BF16)


def _gate_expand():
    e = np.zeros((LANE, 3 * D_ATT), np.float32)
    for br in range(3):
        for h in range(N_HEADS):
            e[br * N_HEADS + h, br * D_ATT + h * HEAD_DIM:br * D_ATT + (h + 1) * HEAD_DIM] = 1.0
    return jnp.asarray(e)


def _outproj(x, o_lru, o_c, o_s, o_w, gl, mod, mod_spec, nl, na, n2, wo, tm):
    n = x.shape[0]
    tok = lambda w: pl.BlockSpec((tm, w), lambda i: (i, 0))
    const = lambda shape: pl.BlockSpec(shape, lambda i: (0, 0))
    gexp = _gate_expand()
    return pl.pallas_call(
        _outproj_kernel,
        grid=(n // tm,),
        in_specs=[tok(D_MODEL), tok(D_RNN), tok(D_ATT), tok(D_ATT), tok(D_ATT), tok(LANE),
                  mod_spec(2), mod_spec(3), mod_spec(4),
                  const(gexp.shape), const(nl.shape), const(na.shape), const(n2.shape),
                  const(wo.shape)],
        out_specs=[tok(D_MODEL), tok(D_MODEL)],
        out_shape=[jax.ShapeDtypeStruct((n, D_MODEL), F32),
                   jax.ShapeDtypeStruct((n, D_MODEL), BF16)],
        compiler_params=_params("arbitrary"),
    )(x, o_lru, o_c, o_s, o_w, gl, mod, mod, mod, gexp, nl, na, n2, wo)


def _extract_top(x, top, row):
    nrow = x.shape[0]
    vals = []
    rank = jnp.full(x.shape, float(top), F32)
    for k in range(top):
        m = jnp.max(x, axis=0, keepdims=True)
        idx = jnp.min(jnp.where(x == m, row, float(nrow)), axis=0, keepdims=True)
        pick = row == idx
        rank = jnp.where(pick, float(k), rank)
        x = jnp.where(pick, REMOVED, x)
        vals.append(m)
    return jnp.concatenate(vals, axis=0), rank


PAIR_COUNTS = tuple(PEER_TOPK // (a + 1) for a in range(PEER_TOPK))
N_PAIRS = sum(PAIR_COUNTS)
PAIR_ROWS = -(-N_PAIRS // 8) * 8


def _peer_score_kernel(h2_ref, wq_ref, bq_ref, sk_ref, bc_ref, a0_ref, r1_ref, e1_ref,
                       q_s, cand_s, *, tn):
    q_s[...] = (_nt(wq_ref[...], h2_ref[...]) + bq_ref[...]).astype(BF16)
    row = lax.broadcasted_iota(jnp.int32, (PEER_KEYS, tn), 0).astype(F32)
    crow = lax.broadcasted_iota(jnp.int32, (PAIR_ROWS, tn), 0).astype(F32)

    def head(h, carry):
        r0 = pl.multiple_of(h * 2 * PEER_KEYS, 2 * PEER_KEYS)
        s0 = _mm(sk_ref[2 * h], q_s[pl.ds(r0, PEER_KEYS), :])
        s1 = _mm(sk_ref[2 * h + 1], q_s[pl.ds(r0 + PEER_KEYS, PEER_KEYS), :])
        v0, rank0 = _extract_top(s0, PEER_TOPK, row)
        v1, rank1 = _extract_top(s1, PEER_TOPK, row)
        off = 0
        for a, cnt in enumerate(PAIR_COUNTS):
            cand_s[off:off + cnt, :] = v1[0:cnt] + v0[a:a + 1]
            off += cnt
        cand_s[N_PAIRS:, :] = jnp.full((PAIR_ROWS - N_PAIRS, tn), REMOVED, F32)
        best, rank_c = _extract_top(cand_s[...], PEER_TOPK, crow)
        z = jnp.sum(jnp.exp(best - best[0:1]), axis=0, keepdims=True)
        cand_s[...] = jnp.where(rank_c < PEER_TOPK, 1.0, 0.0)
        bc = jnp.zeros((PEER_KEYS, tn), F32)
        off = 0
        for a, cnt in enumerate(PAIR_COUNTS):
            kept_a = jnp.sum(cand_s[off:off + cnt, :], axis=0, keepdims=True)
            bc = jnp.where(rank0 == float(a), kept_a, bc)
            off += cnt
        bc_ref[h] = bc
        a0_ref[h] = jnp.where(rank0 < PEER_TOPK, jnp.exp(s0 - v0[0:1]), 0.0) * (0.5 / z)
        r1_ref[h] = rank1.astype(BF16)
        e1_ref[h] = jnp.exp(s1 - v1[0:1]).astype(BF16)
        return carry

    lax.fori_loop(0, PEER_HEADS, head, 0)


def _peer_score(h2, wq_t, bq_t, subkeys, tn):
    n = h2.shape[0]
    dq = wq_t.shape[0]
    tok = pl.BlockSpec((PEER_HEADS, PEER_KEYS, tn), lambda i: (0, 0, i))
    shp = jax.ShapeDtypeStruct((PEER_HEADS, PEER_KEYS, n), F32)
    shp16 = jax.ShapeDtypeStruct((PEER_HEADS, PEER_KEYS, n), BF16)
    return pl.pallas_call(
        functools.partial(_peer_score_kernel, tn=tn),
        grid=(n // tn,),
        in_specs=[pl.BlockSpec((tn, D_MODEL), lambda i: (i, 0)),
                  pl.BlockSpec(wq_t.shape, lambda i: (0, 0)),
                  pl.BlockSpec(bq_t.shape, lambda i: (0, 0)),
                  pl.BlockSpec(subkeys.shape, lambda i: (0, 0, 0))],
        out_specs=[tok, tok, tok, tok],
        out_shape=[shp, shp, shp16, shp16],
        scratch_shapes=[pltpu.VMEM((dq, tn), BF16), pltpu.VMEM((PAIR_ROWS, tn), F32)],
        compiler_params=_params("arbitrary"),
        name="peer_score",
    )(h2, wq_t, bq_t, subkeys)


GELU_C1 = math.sqrt(2.0 / math.pi)
GELU_C2 = 0.044715 * GELU_C1


def _peer_dense_kernel(h2_ref, u_ref, vt_ref, bc_ref, a0_ref, r1_ref, e1_ref, x1_ref,
                       gt2_ref, fg_ref, y_ref, acc_s, act_s, *, tn, eb, lc):
    e = pl.program_id(1)

    @pl.when(e == 0)
    def _():
        acc_s[...] = jnp.zeros(acc_s.shape, F32)

    z = _nt(u_ref[...], h2_ref[...]).astype(BF16)
    for il in range(eb // PEER_KEYS):
        i = e * (eb // PEER_KEYS) + il
        rs = slice(il * PEER_KEYS, (il + 1) * PEER_KEYS)
        for c0 in range(0, tn, lc):
            ls = slice(c0, c0 + lc)
            w = jnp.zeros((PEER_KEYS, lc), BF16)
            for h in range(PEER_HEADS):
                keep = r1_ref[h, :, ls] < bc_ref[h, pl.ds(i, 1), ls].astype(BF16)
                w = w + (jnp.where(keep, e1_ref[h, :, ls], 0.0)
                         * a0_ref[h, pl.ds(i, 1), ls].astype(BF16))
            zi = z[rs, ls]
            t = jnp.tanh(zi * (GELU_C1 + GELU_C2 * (zi * zi)))
            act_s[rs, ls] = (zi + zi * t) * w
    acc_s[...] += _mm(vt_ref[...], act_s[...])

    @pl.when(e == pl.num_programs(1) - 1)
    def _():
        x = x1_ref[...] + gt2_ref[...] * acc_s[...].T
        y_ref[...] = _rms(x, fg_ref[...])


def _peer_dense(h2, u, vt, scores, x1, mod, mod_spec2, fg, tn, eb):
    n = h2.shape[0]
    n_exp = u.shape[0]
    bc, a0, r1, e1 = scores
    tok3 = pl.BlockSpec((PEER_HEADS, PEER_KEYS, tn), lambda i, e: (0, 0, i))
    return pl.pallas_call(
        functools.partial(_peer_dense_kernel, tn=tn, eb=eb, lc=min(tn, 2 * LANE)),
        grid=(n // tn, n_exp // eb),
        in_specs=[pl.BlockSpec((tn, D_MODEL), lambda i, e: (i, 0)),
                  pl.BlockSpec((eb, D_MODEL), lambda i, e: (e, 0)),
                  pl.BlockSpec((D_MODEL, eb), lambda i, e: (0, e)),
                  tok3, tok3, tok3, tok3,
                  pl.BlockSpec((tn, D_MODEL), lambda i, e: (i, 0)),
                  mod_spec2(5),
                  pl.BlockSpec((1, D_MODEL), lambda i, e: (0, 0))],
        out_specs=pl.BlockSpec((tn, D_MODEL), lambda i, e: (i, 0)),
        out_shape=jax.ShapeDtypeStruct((n, D_MODEL), F32),
        scratch_shapes=[pltpu.VMEM((D_MODEL, tn), F32), pltpu.VMEM((eb, tn), BF16)],
        compiler_params=_params("arbitrary", "arbitrary"),
        name="peer_dense",
    )(h2, u, vt, bc, a0, r1, e1, x1, mod, fg)


def _block_diag(w):
    nblk, bw, _ = w.shape
    eye = jnp.eye(nblk, dtype=w.dtype)
    return jnp.einsum('nde,nm->ndme', w, eye).reshape(nblk * bw, nblk * bw)


def kernel(x_prompt, x_sample, cache_cmp_kv, cache_sel_kv, cache_win_kv, state_lru_h, state_conv, page_table, c_prompt, c_sample, ada_w, ada_b, norm1_g, norm2_g, w_in, conv_w, conv_b, lru_wa, lru_ba, lru_wi, lru_bi, lru_lambda, cmp_w1, cmp_w2, cmp_pos, out_norm_lru, out_norm_att, w_out, peer_wq, peer_bq, peer_subkeys, peer_u, peer_v, rel_bias, final_g):
    nb, t, d = x_prompt.shape
    ns = x_sample.shape[0]
    n_pages = page_table.shape[1]
    past = n_pages * PAGE_SIZE
    wb = cache_win_kv.shape[2]
    assert x_sample.shape[1] == 1 and ada_w.shape[0] == 1 and d == D_MODEL
    assert t % (SEL_TILES * LANE) == 0 and ns % LANE == 0 and past % LANE == 0
    assert wb == min(WINDOW, past)
    row = lambda v: v.reshape(1, -1)

    w_in_p = jnp.pad(w_in[0], ((0, 0), (0, sum(IN_SPLITS) - w_in.shape[2]))).astype(BF16)
    wg = jnp.concatenate([_block_diag(lru_wa[0]), _block_diag(lru_wi[0])], axis=1).astype(BF16)
    bg = row(jnp.concatenate([lru_ba[0], lru_bi[0]]))
    w1r = cmp_w1[0].reshape(2, 2, CMP_STRIDE, HEAD_DIM, HEAD_DIM)
    eye2 = jnp.eye(2, dtype=F32)
    w1g = jnp.einsum('cmsdh,ce->scdmeh', w1r, eye2).reshape(CMP_STRIDE * LANE, 2 * LANE).astype(BF16)
    w2bd = jnp.einsum('che,cf->chfe', cmp_w2[0], eye2).reshape(LANE, LANE).astype(BF16)
    wq_t = peer_wq[0].T.astype(BF16)
    bq_t = peer_bq[0].reshape(-1, 1)
    subk = peer_subkeys[0].reshape(2 * PEER_HEADS, PEER_KEYS, -1).astype(BF16)
    u_b = peer_u[0].astype(BF16)
    vt_b = peer_v[0].T.astype(BF16)
    wo_b = w_out[0].astype(BF16)

    n_c = nb + ns
    c_all = jnp.pad(jnp.concatenate([c_prompt, c_sample]), ((0, -n_c % 8), (0, 0)))
    mod = _ada(c_all, ada_w[0], ada_b[0])
    mod_p = mod[:nb].reshape(nb, 1, 6 * D_MODEL)
    mod_s = mod[nb:n_c]

    tm_p, tm_s = TOKEN_TILE, ns
    tn_p, tn_s = TOKEN_TILE, ns
    tn_sc = SCORE_TILE if ns % SCORE_TILE == 0 else LANE

    def mod_spec_p(tm):
        per = t // tm
        return lambda j: pl.BlockSpec((None, 1, D_MODEL), lambda i, *_: (i // per, 0, j))

    def mod_spec_s(tm):
        return lambda j: pl.BlockSpec((tm, D_MODEL), lambda i, *_: (i, j))

    thr_last = BUCKET_THR[-1]
    nd = max(min(t // LANE, -(-(thr_last + LANE - 1) // LANE) + 1), WINDOW // LANE + 2)
    bt, bc = _bias_prompt(rel_bias, t, nd)
    bsc, bsw, bss = _bias_sample(rel_bias, past, wb)
    bss = jnp.transpose(bss, (1, 0, 2))
    b0 = rel_bias[0].reshape(N_HEADS, 1)

    ph = _poshid(cmp_pos[0], cmp_w1[0])
    lru_args = (conv_w[0], row(conv_b[0]), wg, bg, row(lru_lambda[0]))
    norms = (row(out_norm_lru[0]), row(out_norm_att[0]), row(norm2_g[0]), wo_b)

    xp = x_prompt.reshape(nb * t, d)
    xr, yg, q, kvc, kvs, kvw, gl = _inproj(xp, mod_p, mod_spec_p(tm_p), row(norm1_g[0]), w_in_p, tm_p)
    seq = lambda a: a.reshape(nb, t, a.shape[-1])
    o_lru, h_p, conv_p = _lru_prompt(seq(xr), seq(yg), *lru_args, tc=LRU_TILE)
    ck = _compress_prompt(seq(kvc), w1g, w2bd, ph)
    o_c, sel = _cattn_prompt(seq(q), ck, bc, CATTN_TILE)
    o_s, o_w = _sattn_prompt(seq(q), sel, seq(kvs), seq(kvw), bt)
    flat = lambda a: a.reshape(nb * t, a.shape[-1])
    x1, h2 = _outproj(xp, flat(o_lru), flat(o_c), flat(o_s), flat(o_w), gl, mod_p,
                      mod_spec_p(tm_p), *norms, tm_p)
    scores = _peer_score(h2, wq_t, bq_t, subk, SCORE_TILE)
    y_p = _peer_dense(h2, u_b, vt_b, scores, x1, mod_p, mod_spec_p(tn_p), row(final_g), tn_p,
                      EXPERT_TILE)

    xs = x_sample.reshape(ns, d)
    xr_s, yg_s, q_s, kvc_s, kvs_s, kvw_s, gl_s = _inproj(xs, mod_s, mod_spec_s(tm_s),
                                                          row(norm1_g[0]), w_in_p, tm_s)
    o_lru_s, h_s = _lru_sample(xr_s, yg_s, state_conv[0], state_lru_h[0], *lru_args)
    rows_minor = lambda a: jnp.transpose(a, (0, 2, 3, 4, 1))
    cmp_t = rows_minor(cache_cmp_kv[0]).reshape(-1, N_KV, LANE, PAGE_SIZE)
    ck_s = _compress_sample(cmp_t, page_table, w1g, w2bd, ph)
    o_c_s, idx = _cattn_sample(q_s, ck_s, bsc, past)
    top_s = idx.shape[1] // N_KV
    o_s_s, o_w_s = _sattn_sample(q_s, idx, page_table, rows_minor(cache_sel_kv[0]), kvs_s, kvw_s,
                                 rows_minor(cache_win_kv[0]), bss, bsw, b0, top_s,
                                 past // SEL_BLOCK)
    x1_s, h2_s = _outproj(xs, o_lru_s, o_c_s.reshape(ns, D_ATT), o_s_s, o_w_s, gl_s, mod_s,
                          mod_spec_s(tm_s), *norms, tm_s)
    scores_s = _peer_score(h2_s, wq_t, bq_t, subk, tn_sc if ns % tn_sc == 0 else ns)
    y_s = _peer_dense(h2_s, u_b, vt_b, scores_s, x1_s, mod_s, mod_spec_s(tn_s), row(final_g), tn_s,
                      EXPERT_TILE)

    kv6 = lambda a, n, tt: a.reshape(1, n, tt, N_KV, 2, HEAD_DIM)
    win_p = kv6(kvw, nb, t)[:, :, t - min(WINDOW, t):]
    win_s = jnp.concatenate([cache_win_kv[:, :, 1:], kv6(kvw_s, ns, 1)], axis=2)[:, :, -wb:]
    conv_s = jnp.concatenate([state_conv[:, :, 1:], xr_s.reshape(1, ns, 1, D_RNN)], axis=2)
    return (y_p.reshape(nb, t, d), y_s.reshape(ns, 1, d),
            kv6(kvc, nb, t), kv6(kvc_s, ns, 1),
            kv6(kvs, nb, t), kv6(kvs_s, ns, 1),
            win_p, win_s,
            h_p.reshape(1, nb, D_RNN), h_s.reshape(1, ns, D_RNN),
            conv_p.reshape(1, nb, CONV_W - 1, D_RNN), conv_s)
```
